```python
import jax
import jax.numpy as jnp
from jax import lax
import numpy as np

D_MODEL = 1024
BATCH = 8
SEQ = 4096
DEPTH = 1

MIX_WIDTH = D_MODEL
GLA_WIDTH = MIX_WIDTH // 2
POOL_WIDTH = MIX_WIDTH - GLA_WIDTH
GLA_HEADS = 4
GLA_DV = GLA_WIDTH // GLA_HEADS
GLA_DK = GLA_DV // 2
GLA_KEY_WIDTH = GLA_HEADS * GLA_DK
GLA_GATE_RANK = 16
GLA_TAU = 16.0
GLA_CHUNK = 64
POOL_WINDOWS = (2, 4, 8, 16)
POOL_GROUPS = len(POOL_WINDOWS)
POOL_GROUP_WIDTH = POOL_WIDTH // POOL_GROUPS
PEER_HEADS = 8
PEER_NKEYS = 128
PEER_EXPERTS = PEER_NKEYS * PEER_NKEYS
PEER_QDIM = 256
PEER_HALF = PEER_QDIM // 2
PEER_TOPK = 16
PEER_TOKEN_BLOCK = 128
N_MOD = 6
EPS = 1e-6

IN_SIZES = (GLA_KEY_WIDTH, GLA_KEY_WIDTH, GLA_WIDTH, GLA_WIDTH, GLA_GATE_RANK, GLA_GATE_RANK, POOL_WIDTH)
IN_COLS = sum(IN_SIZES)
IN_SPLITS = tuple(int(s) for s in np.cumsum(IN_SIZES)[:-1])

kernel_name = "hybrid_gla_pool_peer_encoder_layer"


def rmsnorm(x, g):
    xf = x.astype(jnp.float32)
    y = xf * lax.rsqrt(jnp.mean(xf * xf, axis=-1, keepdims=True) + EPS)
    return (y * g.astype(jnp.float32)).astype(x.dtype)


def gla_chunked(q, k, v, log_a, strict):
    B, H, S, dk = q.shape
    dv = v.shape[-1]
    C = GLA_CHUNK
    n = S // C
    qc = q.reshape(B, H, n, C, dk).astype(jnp.float32)
    kc = k.reshape(B, H, n, C, dk).astype(jnp.float32)
    vc = v.reshape(B, H, n, C, dv).astype(jnp.float32)
    b = jnp.cumsum(log_a.reshape(B, H, n, C, dk), axis=3)
    b_last = b[:, :, :, -1:, :]
    q_dec = qc * jnp.exp(b)
    k_inv = kc * jnp.exp(-b)
    k_end = kc * jnp.exp(b_last - b)
    scores = jnp.einsum('bhnid,bhnjd->bhnij', q_dec, k_inv)
    mask = jnp.tril(jnp.ones((C, C), dtype=bool), k=-1 if strict else 0)
    scores = jnp.where(mask, scores, 0.0)
    o_intra = jnp.einsum('bhnij,bhnjv->bhniv', scores, vc)
    u = jnp.einsum('bhnjd,bhnjv->bhndv', k_end, vc)
    gamma = jnp.exp(b_last[:, :, :, 0, :])

    def step(state, inp):
        g_n, u_n = inp
        return g_n[..., None] * state + u_n, state

    init = jnp.zeros((B, H, dk, dv), jnp.float32)
    _, s_prev = lax.scan(step, init, (jnp.moveaxis(gamma, 2, 0), jnp.moveaxis(u, 2, 0)))
    s_prev = jnp.moveaxis(s_prev, 0, 2)
    o_inter = jnp.einsum('bhnid,bhndv->bhniv', q_dec, s_prev)
    return (o_intra + o_inter).reshape(B, H, S, dv)


def gla_group(q, k, v, r, af, ab, w_af, b_af, w_ab, b_ab, g_head):
    B, S, _ = q.shape

    def heads(t, d):
        return t.reshape(B, S, GLA_HEADS, d).transpose(0, 2, 1, 3)

    qh = heads(q, GLA_DK) * (GLA_DK ** -0.5)
    kh = heads(k, GLA_DK)
    vh = heads(v, GLA_DV)
    log_af = jax.nn.log_sigmoid((af @ w_af + b_af).astype(jnp.float32)) / GLA_TAU
    log_ab = jax.nn.log_sigmoid((ab @ w_ab + b_ab).astype(jnp.float32)) / GLA_TAU
    o_f = gla_chunked(qh, kh, vh, heads(log_af, GLA_DK), strict=False)
    o_b = jnp.flip(gla_chunked(jnp.flip(qh, 2), jnp.flip(kh, 2), jnp.flip(vh, 2),
                               jnp.flip(heads(log_ab, GLA_DK), 2), strict=True), 2)
    o = jnp.transpose(o_f + o_b, (0, 2, 1, 3)).astype(q.dtype)
    o = rmsnorm(o, g_head.reshape(GLA_HEADS, GLA_DV)).reshape(B, S, GLA_WIDTH)
    return o * jax.nn.silu(r)


def pool_group(p, w_pool, pool_scale):
    B, S, W = p.shape
    pf = p.astype(jnp.float32)
    cs = jnp.concatenate([jnp.zeros((B, 1, W), jnp.float32), jnp.cumsum(pf, axis=1)], axis=1)
    pos = jnp.arange(S)
    outs = []
    for gi, w in enumerate(POOL_WINDOWS):
        sl = slice(gi * POOL_GROUP_WIDTH, (gi + 1) * POOL_GROUP_WIDTH)
        lo = jnp.clip(pos - w // 2, 0, S)
        hi = jnp.clip(pos + w // 2, 0, S)
        seg = cs[:, :, sl]
        total = seg[:, hi] - seg[:, lo]
        count = (hi - lo).astype(jnp.float32)[None, :, None]
        outs.append(total / count - pf[:, :, sl])
    pooled = jnp.stack(outs, axis=2).astype(p.dtype)
    y = jnp.einsum('bsgc,gcd->bsgd', pooled, w_pool)
    return y.reshape(B, S, POOL_WIDTH) * pool_scale


def peer_ffn(h, w_q, sub_keys, u_tab, v_tab):
    B, S, D = h.shape
    T = B * S
    K = PEER_TOPK
    ht = h.reshape(T, D)
    q = (ht @ w_q).reshape(T, PEER_HEADS, 2, PEER_HALF)
    s = jnp.einsum('thpd,hpnd->thpn', q, sub_keys).astype(jnp.float32)
    vals, idx = lax.top_k(s, K)
    cand = vals[:, :, 0, :, None] + vals[:, :, 1, None, :]
    top_s, top_pos = lax.top_k(cand.reshape(T, PEER_HEADS, K * K), K)
    i1 = jnp.take_along_axis(idx[:, :, 0], top_pos // K, axis=-1)
    i2 = jnp.take_along_axis(idx[:, :, 1], top_pos % K, axis=-1)
    experts = (i1 * PEER_NKEYS + i2).reshape(T, PEER_HEADS * K)
    gates = jax.nn.softmax(top_s, axis=-1).reshape(T, PEER_HEADS * K)
    nb = T // PEER_TOKEN_BLOCK

    def block(args):
        xb, eb, gb = args
        a = jax.nn.gelu(jnp.einsum('td,ted->te', xb, u_tab[eb]), approximate=False)
        coeff = (gb * a.astype(jnp.float32)).astype(xb.dtype)
        return jnp.einsum('te,ted->td', coeff, v_tab[eb])

    y = lax.map(block, (ht.reshape(nb, PEER_TOKEN_BLOCK, D),
                        experts.reshape(nb, PEER_TOKEN_BLOCK, PEER_HEADS * K),
                        gates.reshape(nb, PEER_TOKEN_BLOCK, PEER_HEADS * K)))
    return y.reshape(B, S, D)


def setup_inputs(seed: int = 0) -> dict:
    key = jax.random.key(seed)
    ks = jax.random.split(key, 24)
    L, D = DEPTH, D_MODEL
    f32 = jnp.float32

    def nrm(k, shape, scale):
        return jax.random.normal(k, shape, f32) * scale

    def gain(k, shape):
        return 1.0 + 0.01 * jax.random.normal(k, shape, f32)

    return {
        "x": jax.random.normal(ks[0], (BATCH, SEQ, D), f32),
        "c": jax.random.normal(ks[1], (BATCH, D), f32),
        "w_mod": nrm(ks[2], (L, D, N_MOD * D), 0.5 * D ** -0.5),
        "b_mod": nrm(ks[3], (L, N_MOD * D), 0.01),
        "g_pre_mix": gain(ks[4], (L, D)),
        "g_post_mix": gain(ks[5], (L, D)),
        "w_in": nrm(ks[6], (L, D, IN_COLS), D ** -0.5),
        "w_alpha_f": nrm(ks[7], (L, GLA_GATE_RANK, GLA_KEY_WIDTH), GLA_GATE_RANK ** -0.5),
        "b_alpha_f": nrm(ks[8], (L, GLA_KEY_WIDTH), 0.01),
        "w_alpha_b": nrm(ks[9], (L, GLA_GATE_RANK, GLA_KEY_WIDTH), GLA_GATE_RANK ** -0.5),
        "b_alpha_b": nrm(ks[10], (L, GLA_KEY_WIDTH), 0.01),
        "g_gla_head": gain(ks[11], (L, GLA_WIDTH)),
        "w_pool": nrm(ks[12], (L, POOL_GROUPS, POOL_GROUP_WIDTH, POOL_GROUP_WIDTH), POOL_GROUP_WIDTH ** -0.5),
        "pool_scale": gain(ks[13], (L, POOL_WIDTH)),
        "w_out": nrm(ks[14], (L, MIX_WIDTH, D), MIX_WIDTH ** -0.5),
        "g_pre_ffn": gain(ks[15], (L, D)),
        "g_post_ffn": gain(ks[16], (L, D)),
        "w_peer_q": nrm(ks[17], (L, D, PEER_HEADS * PEER_QDIM), D ** -0.5),
        "peer_sub_keys": nrm(ks[18], (L, PEER_HEADS, 2, PEER_NKEYS, PEER_HALF), PEER_HALF ** -0.5),
        "peer_u": nrm(ks[19], (L, PEER_EXPERTS, D), D ** -0.5),
        "peer_v": nrm(ks[20], (L, PEER_EXPERTS, D), PEER_HEADS ** -0.5),
    }


def reference(x, c, w_mod, b_mod, g_pre_mix, g_post_mix, w_in, w_alpha_f, b_alpha_f, w_alpha_b, b_alpha_b,
              g_gla_head, w_pool, pool_scale, w_out, g_pre_ffn, g_post_ffn, w_peer_q, peer_sub_keys,
              peer_u, peer_v):
    cond = jax.nn.silu(c)
    for l in range(DEPTH):
        mod = cond @ w_mod[l] + b_mod[l]
        sh1, sc1, gt1, sh2, sc2, gt2 = [m[:, None, :] for m in jnp.split(mod, N_MOD, axis=-1)]

        h = rmsnorm(x, g_pre_mix[l]) * (1 + sc1) + sh1
        z = h @ w_in[l]
        q, k, v, r, af, ab, p = jnp.split(z, IN_SPLITS, axis=-1)
        o_gla = gla_group(q, k, v, r, af, ab, w_alpha_f[l], b_alpha_f[l], w_alpha_b[l], b_alpha_b[l],
                          g_gla_head[l])
        o_pool = pool_group(p, w_pool[l], pool_scale[l])
        mix = jnp.concatenate([o_gla.astype(x.dtype), o_pool.astype(x.dtype)], axis=-1) @ w_out[l]
        x = x + gt1 * rmsnorm(mix, g_post_mix[l])

        h2 = rmsnorm(x, g_pre_ffn[l]) * (1 + sc2) + sh2
        y = peer_ffn(h2, w_peer_q[l], peer_sub_keys[l], peer_u[l], peer_v[l])
        x = x + gt2 * rmsnorm(y, g_post_ffn[l])
    return x
```

```python
import functools

import jax
import jax.numpy as jnp
import numpy as np
from jax import lax
from jax.experimental import pallas as pl
from jax.experimental.pallas import tpu as pltpu

D_MODEL = 1024
GLA_WIDTH = 512
POOL_WIDTH = 512
GLA_HEADS = 4
GLA_DV = 128
GLA_DK = 64
GLA_KEY_WIDTH = 256
GLA_GATE_RANK = 16
GLA_TAU = 16.0
GLA_CHUNK = 64
POOL_WINDOWS = (2, 4, 8, 16)
POOL_GROUP_WIDTH = 128
PEER_HEADS = 8
PEER_NKEYS = 128
PEER_HALF = 128
PEER_TOPK = 16
PEER_TOKEN_BLOCK = 128
N_MOD = 6
EPS = 1e-6
IN_SIZES = (256, 256, 512, 512, 16, 16, 512)
IN_SPLITS = tuple(int(s) for s in np.cumsum(IN_SIZES)[:-1])


def _rmsnorm(x, g):
    return x * lax.rsqrt(jnp.mean(x * x, axis=-1, keepdims=True) + EPS) * g


def _gla_chunked(q, k, v, log_a, strict):
    B, H, S, dk = q.shape
    dv = v.shape[-1]
    C = GLA_CHUNK
    n = S // C
    qc = q.reshape(B, H, n, C, dk)
    kc = k.reshape(B, H, n, C, dk)
    vc = v.reshape(B, H, n, C, dv)
    b = jnp.cumsum(log_a.reshape(B, H, n, C, dk), axis=3)
    b_last = b[:, :, :, -1:, :]
    q_dec = qc * jnp.exp(b)
    k_inv = kc * jnp.exp(-b)
    k_end = kc * jnp.exp(b_last - b)
    scores = jnp.einsum('bhnid,bhnjd->bhnij', q_dec, k_inv)
    mask = jnp.tril(jnp.ones((C, C), dtype=bool), k=-1 if strict else 0)
    scores = jnp.where(mask, scores, 0.0)
    o_intra = jnp.einsum('bhnij,bhnjv->bhniv', scores, vc)
    u = jnp.einsum('bhnjd,bhnjv->bhndv', k_end, vc)
    gamma = jnp.exp(b_last[:, :, :, 0, :])

    def step(state, inp):
        g_n, u_n = inp
        return g_n[..., None] * state + u_n, state

    init = jnp.zeros((B, H, dk, dv), jnp.float32)
    _, s_prev = lax.scan(step, init, (jnp.moveaxis(gamma, 2, 0), jnp.moveaxis(u, 2, 0)))
    s_prev = jnp.moveaxis(s_prev, 0, 2)
    o_inter = jnp.einsum('bhnid,bhndv->bhniv', q_dec, s_prev)
    return (o_intra + o_inter).reshape(B, H, S, dv)


def _gla_group(q, k, v, r, af, ab, w_af, b_af, w_ab, b_ab, g_head):
    B, S, _ = q.shape

    def heads(t, d):
        return t.reshape(B, S, GLA_HEADS, d).transpose(0, 2, 1, 3)

    qh = heads(q, GLA_DK) * (GLA_DK ** -0.5)
    kh = heads(k, GLA_DK)
    vh = heads(v, GLA_DV)
    log_af = jax.nn.log_sigmoid(af @ w_af + b_af) / GLA_TAU
    log_ab = jax.nn.log_sigmoid(ab @ w_ab + b_ab) / GLA_TAU
    o_f = _gla_chunked(qh, kh, vh, heads(log_af, GLA_DK), strict=False)
    o_b = jnp.flip(_gla_chunked(jnp.flip(qh, 2), jnp.flip(kh, 2), jnp.flip(vh, 2),
                                jnp.flip(heads(log_ab, GLA_DK), 2), strict=True), 2)
    o = jnp.transpose(o_f + o_b, (0, 2, 1, 3))
    o = _rmsnorm(o, g_head.reshape(GLA_HEADS, GLA_DV)).reshape(B, S, GLA_WIDTH)
    return o * jax.nn.silu(r)


def _pool_group(p, w_pool, pool_scale):
    B, S, W = p.shape
    cs = jnp.concatenate([jnp.zeros((B, 1, W), jnp.float32), jnp.cumsum(p, axis=1)], axis=1)
    pos = jnp.arange(S)
    outs = []
    for gi, w in enumerate(POOL_WINDOWS):
        sl = slice(gi * POOL_GROUP_WIDTH, (gi + 1) * POOL_GROUP_WIDTH)
        lo = jnp.clip(pos - w // 2, 0, S)
        hi = jnp.clip(pos + w // 2, 0, S)
        seg = cs[:, :, sl]
        total = seg[:, hi] - seg[:, lo]
        count = (hi - lo).astype(jnp.float32)[None, :, None]
        outs.append(total / count - p[:, :, sl])
    pooled = jnp.stack(outs, axis=2)
    y = jnp.einsum('bsgc,gcd->bsgd', pooled, w_pool)
    return y.reshape(B, S, POOL_WIDTH) * pool_scale


def _peer_ffn(h, w_q, sub_keys, u_tab, v_tab):
    B, S, D = h.shape
    T = B * S
    K = PEER_TOPK
    ht = h.reshape(T, D)
    q = (ht @ w_q).reshape(T, PEER_HEADS, 2, PEER_HALF)
    s = jnp.einsum('thpd,hpnd->thpn', q, sub_keys)
    vals, idx = lax.top_k(s, K)
    cand = vals[:, :, 0, :, None] + vals[:, :, 1, None, :]
    top_s, top_pos = lax.top_k(cand.reshape(T, PEER_HEADS, K * K), K)
    i1 = jnp.take_along_axis(idx[:, :, 0], top_pos // K, axis=-1)
    i2 = jnp.take_along_axis(idx[:, :, 1], top_pos % K, axis=-1)
    experts = (i1 * PEER_NKEYS + i2).reshape(T, PEER_HEADS * K)
    gates = jax.nn.softmax(top_s, axis=-1).reshape(T, PEER_HEADS * K)
    nb = T // PEER_TOKEN_BLOCK

    def block(args):
        xb, eb, gb = args
        a = jax.nn.gelu(jnp.einsum('td,ted->te', xb, u_tab[eb]), approximate=False)
        coeff = gb * a
        return jnp.einsum('te,ted->td', coeff, v_tab[eb])

    y = lax.map(block, (ht.reshape(nb, PEER_TOKEN_BLOCK, D),
                        experts.reshape(nb, PEER_TOKEN_BLOCK, PEER_HEADS * K),
                        gates.reshape(nb, PEER_TOKEN_BLOCK, PEER_HEADS * K)))
    return y.reshape(B, S, D)


def _post_residual_body(x_ref, y_ref, g_ref, gate_ref, o_ref):
    y = y_ref[0]
    yn = y * lax.rsqrt(jnp.mean(y * y, axis=-1, keepdims=True) + EPS) * g_ref[...]
    o_ref[0] = x_ref[0] + gate_ref[0] * yn


def _post_residual(x, y, g, gate):
    B, S, D = x.shape
    ts = 512
    return pl.pallas_call(
        _post_residual_body,
        grid=(B, S // ts),
        in_specs=[
            pl.BlockSpec((1, ts, D), lambda b, s: (b, s, 0)),
            pl.BlockSpec((1, ts, D), lambda b, s: (b, s, 0)),
            pl.BlockSpec((1, D), lambda b, s: (0, 0)),
            pl.BlockSpec((1, 1, D), lambda b, s: (b, 0, 0)),
        ],
        out_specs=pl.BlockSpec((1, ts, D), lambda b, s: (b, s, 0)),
        out_shape=jax.ShapeDtypeStruct((B, S, D), x.dtype),
        name="post_residual",
    )(x, y, g.reshape(1, D), gate)


def kernel(x, c, w_mod, b_mod, g_pre_mix, g_post_mix, w_in, w_alpha_f, b_alpha_f, w_alpha_b, b_alpha_b,
           g_gla_head, w_pool, pool_scale, w_out, g_pre_ffn, g_post_ffn, w_peer_q, peer_sub_keys,
           peer_u, peer_v):
    cond = jax.nn.silu(c)
    for l in range(w_mod.shape[0]):
        mod = cond @ w_mod[l] + b_mod[l]
        sh1, sc1, gt1, sh2, sc2, gt2 = [m[:, None, :] for m in jnp.split(mod, N_MOD, axis=-1)]
        h = _rmsnorm(x, g_pre_mix[l]) * (1 + sc1) + sh1
        z = h @ w_in[l]
        q, k, v, r, af, ab, p = jnp.split(z, IN_SPLITS, axis=-1)
        o_gla = _gla_group(q, k, v, r, af, ab, w_alpha_f[l], b_alpha_f[l], w_alpha_b[l], b_alpha_b[l],
                           g_gla_head[l])
        o_pool = _pool_group(p, w_pool[l], pool_scale[l])
        mix = jnp.concatenate([o_gla, o_pool], axis=-1) @ w_out[l]
        x = _post_residual(x, mix, g_post_mix[l], gt1)
        h2 = _rmsnorm(x, g_pre_ffn[l]) * (1 + sc2) + sh2
        y = _peer_ffn(h2, w_peer_q[l], peer_sub_keys[l], peer_u[l], peer_v[l])
        x = _post_residual(x, y, g_post_ffn[l], gt2)
    return x
```

```python
import functools

import jax
import jax.numpy as jnp
import numpy as np
from jax import lax
from jax.experimental import pallas as pl
from jax.experimental.pallas import tpu as pltpu

D_MODEL = 1024
GLA_WIDTH = 512
POOL_WIDTH = 512
GLA_HEADS = 4
GLA_DV = 128
GLA_DK = 64
GLA_KEY_WIDTH = 256
GLA_GATE_RANK = 16
GLA_TAU = 16.0
GLA_CHUNK = 64
POOL_WINDOWS = (2, 4, 8, 16)
POOL_GROUP_WIDTH = 128
PEER_HEADS = 8
PEER_NKEYS = 128
PEER_HALF = 128
PEER_TOPK = 16
PEER_TOKEN_BLOCK = 128
N_MOD = 6
EPS = 1e-6
IN_SIZES = (256, 256, 512, 512, 16, 16, 512)
IN_SPLITS = tuple(int(s) for s in np.cumsum(IN_SIZES)[:-1])


def _rmsnorm(x, g):
    return x * lax.rsqrt(jnp.mean(x * x, axis=-1, keepdims=True) + EPS) * g


def _gla_chunked(q, k, v, log_a, strict):
    B, H, S, dk = q.shape
    dv = v.shape[-1]
    C = GLA_CHUNK
    n = S // C
    qc = q.reshape(B, H, n, C, dk)
    kc = k.reshape(B, H, n, C, dk)
    vc = v.reshape(B, H, n, C, dv)
    b = jnp.cumsum(log_a.reshape(B, H, n, C, dk), axis=3)
    b_last = b[:, :, :, -1:, :]
    q_dec = qc * jnp.exp(b)
    k_inv = kc * jnp.exp(-b)
    k_end = kc * jnp.exp(b_last - b)
    scores = jnp.einsum('bhnid,bhnjd->bhnij', q_dec, k_inv)
    mask = jnp.tril(jnp.ones((C, C), dtype=bool), k=-1 if strict else 0)
    scores = jnp.where(mask, scores, 0.0)
    o_intra = jnp.einsum('bhnij,bhnjv->bhniv', scores, vc)
    u = jnp.einsum('bhnjd,bhnjv->bhndv', k_end, vc)
    gamma = jnp.exp(b_last[:, :, :, 0, :])

    def step(state, inp):
        g_n, u_n = inp
        return g_n[..., None] * state + u_n, state

    init = jnp.zeros((B, H, dk, dv), jnp.float32)
    _, s_prev = lax.scan(step, init, (jnp.moveaxis(gamma, 2, 0), jnp.moveaxis(u, 2, 0)))
    s_prev = jnp.moveaxis(s_prev, 0, 2)
    o_inter = jnp.einsum('bhnid,bhndv->bhniv', q_dec, s_prev)
    return (o_intra + o_inter).reshape(B, H, S, dv)


def _gla_group(q, k, v, r, af, ab, w_af, b_af, w_ab, b_ab, g_head):
    B, S, _ = q.shape

    def heads(t, d):
        return t.reshape(B, S, GLA_HEADS, d).transpose(0, 2, 1, 3)

    qh = heads(q, GLA_DK) * (GLA_DK ** -0.5)
    kh = heads(k, GLA_DK)
    vh = heads(v, GLA_DV)
    log_af = jax.nn.log_sigmoid(af @ w_af + b_af) / GLA_TAU
    log_ab = jax.nn.log_sigmoid(ab @ w_ab + b_ab) / GLA_TAU
    o_f = _gla_chunked(qh, kh, vh, heads(log_af, GLA_DK), strict=False)
    o_b = jnp.flip(_gla_chunked(jnp.flip(qh, 2), jnp.flip(kh, 2), jnp.flip(vh, 2),
                                jnp.flip(heads(log_ab, GLA_DK), 2), strict=True), 2)
    o = jnp.transpose(o_f + o_b, (0, 2, 1, 3))
    o = _rmsnorm(o, g_head.reshape(GLA_HEADS, GLA_DV)).reshape(B, S, GLA_WIDTH)
    return o * jax.nn.silu(r)


def _pool_group(p, w_pool, pool_scale):
    B, S, W = p.shape
    cs = jnp.concatenate([jnp.zeros((B, 1, W), jnp.float32), jnp.cumsum(p, axis=1)], axis=1)
    pos = jnp.arange(S)
    outs = []
    for gi, w in enumerate(POOL_WINDOWS):
        sl = slice(gi * POOL_GROUP_WIDTH, (gi + 1) * POOL_GROUP_WIDTH)
        lo = jnp.clip(pos - w // 2, 0, S)
        hi = jnp.clip(pos + w // 2, 0, S)
        seg = cs[:, :, sl]
        total = seg[:, hi] - seg[:, lo]
        count = (hi - lo).astype(jnp.float32)[None, :, None]
        outs.append(total / count - p[:, :, sl])
    pooled = jnp.stack(outs, axis=2)
    y = jnp.einsum('bsgc,gcd->bsgd', pooled, w_pool)
    return y.reshape(B, S, POOL_WIDTH) * pool_scale


def _peer_ffn(h, w_q, sub_keys, u_tab, v_tab):
    B, S, D = h.shape
    T = B * S
    K = PEER_TOPK
    ht = h.reshape(T, D)
    q = (ht @ w_q).reshape(T, PEER_HEADS, 2, PEER_HALF)
    s = jnp.einsum('thpd,hpnd->thpn', q, sub_keys)
    vals, idx = lax.top_k(s, K)
    cand = vals[:, :, 0, :, None] + vals[:, :, 1, None, :]
    top_s, top_pos = lax.top_k(cand.reshape(T, PEER_HEADS, K * K), K)
    i1 = jnp.take_along_axis(idx[:, :, 0], top_pos // K, axis=-1)
    i2 = jnp.take_along_axis(idx[:, :, 1], top_pos % K, axis=-1)
    experts = (i1 * PEER_NKEYS + i2).reshape(T, PEER_HEADS * K)
    gates = jax.nn.softmax(top_s, axis=-1).reshape(T, PEER_HEADS * K)
    gates_dup = jnp.repeat(gates.T, 2, axis=0)
    coef_dup = _peer_u_pass(experts, ht, gates_dup, _pack_table(u_tab))
    y = _peer_v_pass(experts, coef_dup, _pack_table(v_tab))
    return y.reshape(B, S, D)


PEER_SLOTS = PEER_HEADS * PEER_TOPK
PEER_CHUNKS = D_MODEL // 256
PEER_TB = 256
TILE_STRIDE = PEER_SLOTS + 8
LANES = 128
PEER_VMEM_LIMIT = 48 * 1024 * 1024


def _pack_table(tab):
    E, D = tab.shape
    t = tab.astype(jnp.bfloat16).reshape(E, 2, D // 256, LANES)
    t = jnp.transpose(t, (0, 2, 3, 1))
    return lax.bitcast_convert_type(t, jnp.uint32).reshape(E * (D // 256), LANES)


def _gather_token(idx_ref, tab_ref, tile_ref, t):
    for k in range(PEER_SLOTS):
        row = pl.multiple_of(idx_ref[t, k] * PEER_CHUNKS, PEER_CHUNKS)
        tile_ref[pl.ds(k, PEER_CHUNKS, stride=TILE_STRIDE), :] = tab_ref[pl.ds(row, PEER_CHUNKS), :]


def _tile_chunk_f32(tile_ref, c):
    w = tile_ref[pl.ds(c * TILE_STRIDE, PEER_SLOTS), :]
    return pltpu.bitcast(w, jnp.bfloat16).astype(jnp.float32)


def _peer_u_body(idx_ref, h_ref, gate_ref, tab_ref, coef_ref, tile_ref, sblk_ref):
    lane = lax.broadcasted_iota(jnp.int32, (2 * PEER_SLOTS, LANES), 1)
    odd = (lax.broadcasted_iota(jnp.int32, (8, LANES), 0) & 1) == 1
    for blk in range(PEER_TB // LANES):
        def token(tl, carry):
            t = blk * LANES + tl
            _gather_token(idx_ref, tab_ref, tile_ref, t)
            acc = jnp.zeros((2 * PEER_SLOTS // 8, 8, LANES), jnp.float32)
            for c in range(PEER_CHUNKS):
                h_lo = jnp.broadcast_to(h_ref[t, pl.ds(c, 1), :], (8, LANES))
                h_hi = jnp.broadcast_to(h_ref[t, pl.ds(PEER_CHUNKS + c, 1), :], (8, LANES))
                pattern = jnp.where(odd, h_hi, h_lo)
                x = _tile_chunk_f32(tile_ref, c).reshape(2 * PEER_SLOTS // 8, 8, LANES)
                acc = acc + x * pattern[None]
            s = jnp.sum(acc.reshape(2 * PEER_SLOTS, LANES), axis=1, keepdims=True)
            sblk_ref[...] = jnp.where(lane == tl, s, sblk_ref[...])
            return carry
        lax.fori_loop(0, LANES, token, 0)
        even_rows = sblk_ref[pl.ds(0, PEER_SLOTS, stride=2), :]
        odd_rows = sblk_ref[pl.ds(1, PEER_SLOTS, stride=2), :]
        a = even_rows + odd_rows
        act = 0.5 * a * (1.0 + lax.erf(a * (2.0 ** -0.5)))
        sblk_ref[pl.ds(0, PEER_SLOTS, stride=2), :] = act
        sblk_ref[pl.ds(1, PEER_SLOTS, stride=2), :] = act
        cols = pl.ds(blk * LANES, LANES)
        coef_ref[:, cols] = gate_ref[:, cols] * sblk_ref[...]


def _peer_u_pass(experts, ht, gates_dup, tab):
    T = ht.shape[0]
    return pl.pallas_call(
        _peer_u_body,
        grid=(T // PEER_TB,),
        in_specs=[
            pl.BlockSpec((PEER_TB, PEER_SLOTS), lambda i: (i, 0), memory_space=pltpu.SMEM),
            pl.BlockSpec((PEER_TB, 8, LANES), lambda i: (i, 0, 0)),
            pl.BlockSpec((2 * PEER_SLOTS, PEER_TB), lambda i: (0, i)),
            pl.BlockSpec(memory_space=pltpu.VMEM),
        ],
        out_specs=pl.BlockSpec((2 * PEER_SLOTS, PEER_TB), lambda i: (0, i)),
        out_shape=jax.ShapeDtypeStruct((2 * PEER_SLOTS, T), jnp.float32),
        scratch_shapes=[
            pltpu.VMEM((PEER_CHUNKS * TILE_STRIDE, LANES), jnp.uint32),
            pltpu.VMEM((2 * PEER_SLOTS, LANES), jnp.float32),
        ],
        compiler_params=pltpu.CompilerParams(vmem_limit_bytes=PEER_VMEM_LIMIT),
        name="peer_u_pass",
    )(experts, ht.reshape(T, 8, LANES), gates_dup, tab)


def _peer_v_body(idx_ref, coef_ref, tab_ref, y_ref, tile_ref):
    lane = lax.broadcasted_iota(jnp.int32, (2 * PEER_SLOTS, LANES), 1)
    for blk in range(PEER_TB // LANES):
        cols = pl.ds(blk * LANES, LANES)

        def token(tl, carry):
            t = blk * LANES + tl
            _gather_token(idx_ref, tab_ref, tile_ref, t)
            ccol = jnp.sum(jnp.where(lane == tl, coef_ref[:, cols], 0.0), axis=1, keepdims=True)
            ccol = ccol.reshape(2 * PEER_SLOTS // 8, 8, 1)
            for c in range(PEER_CHUNKS):
                x = _tile_chunk_f32(tile_ref, c).reshape(2 * PEER_SLOTS // 8, 8, LANES)
                part = jnp.sum(x * ccol, axis=0)
                part = part + pltpu.roll(part, 4, axis=0)
                part = part + pltpu.roll(part, 2, axis=0)
                y_ref[t, pl.ds(c, 1), :] = part[0:1]
                y_ref[t, pl.ds(PEER_CHUNKS + c, 1), :] = part[1:2]
            return carry
        lax.fori_loop(0, LANES, token, 0)


def _peer_v_pass(experts, coef_dup, tab):
    T = experts.shape[0]
    y = pl.pallas_call(
        _peer_v_body,
        grid=(T // PEER_TB,),
        in_specs=[
            pl.BlockSpec((PEER_TB, PEER_SLOTS), lambda i: (i, 0), memory_space=pltpu.SMEM),
            pl.BlockSpec((2 * PEER_SLOTS, PEER_TB), lambda i: (0, i)),
            pl.BlockSpec(memory_space=pltpu.VMEM),
        ],
        out_specs=pl.BlockSpec((PEER_TB, 8, LANES), lambda i: (i, 0, 0)),
        out_shape=jax.ShapeDtypeStruct((T, 8, LANES), jnp.float32),
        scratch_shapes=[pltpu.VMEM((PEER_CHUNKS * TILE_STRIDE, LANES), jnp.uint32)],
        compiler_params=pltpu.CompilerParams(vmem_limit_bytes=PEER_VMEM_LIMIT),
        name="peer_v_pass",
    )(experts, coef_dup, tab)
    return y.reshape(T, D_MODEL)


def _post_residual_body(x_ref, y_ref, g_ref, gate_ref, o_ref):
    y = y_ref[0]
    yn = y * lax.rsqrt(jnp.mean(y * y, axis=-1, keepdims=True) + EPS) * g_ref[...]
    o_ref[0] = x_ref[0] + gate_ref[0] * yn


def _post_residual(x, y, g, gate):
    B, S, D = x.shape
    ts = 512
    return pl.pallas_call(
        _post_residual_body,
        grid=(B, S // ts),
        in_specs=[
            pl.BlockSpec((1, ts, D), lambda b, s: (b, s, 0)),
            pl.BlockSpec((1, ts, D), lambda b, s: (b, s, 0)),
            pl.BlockSpec((1, D), lambda b, s: (0, 0)),
            pl.BlockSpec((1, 1, D), lambda b, s: (b, 0, 0)),
        ],
        out_specs=pl.BlockSpec((1, ts, D), lambda b, s: (b, s, 0)),
        out_shape=jax.ShapeDtypeStruct((B, S, D), x.dtype),
        name="post_residual",
    )(x, y, g.reshape(1, D), gate)


def kernel(x, c, w_mod, b_mod, g_pre_mix, g_post_mix, w_in, w_alpha_f, b_alpha_f, w_alpha_b, b_alpha_b,
           g_gla_head, w_pool, pool_scale, w_out, g_pre_ffn, g_post_ffn, w_peer_q, peer_sub_keys,
           peer_u, peer_v):
    cond = jax.nn.silu(c)
    for l in range(w_mod.shape[0]):
        mod = cond @ w_mod[l] + b_mod[l]
        sh1, sc1, gt1, sh2, sc2, gt2 = [m[:, None, :] for m in jnp.split(mod, N_MOD, axis=-1)]
        h = _rmsnorm(x, g_pre_mix[l]) * (1 + sc1) + sh1
        z = h @ w_in[l]
        q, k, v, r, af, ab, p = jnp.split(z, IN_SPLITS, axis=-1)
        o_gla = _gla_group(q, k, v, r, af, ab, w_alpha_f[l], b_alpha_f[l], w_alpha_b[l], b_alpha_b[l],
                           g_gla_head[l])
        o_pool = _pool_group(p, w_pool[l], pool_scale[l])
        mix = jnp.concatenate([o_gla, o_pool], axis=-1) @ w_out[l]
        x = _post_residual(x, mix, g_post_mix[l], gt1)
        h2 = _rmsnorm(x, g_pre_ffn[l]) * (1 + sc2) + sh2
        y = _peer_ffn(h2, w_peer_q[l], peer_sub_keys[l], peer_u[l], peer_v[l])
        x = _post_residual(x, y, g_post_ffn[l], gt2)
    return x
```

```python
import functools

import jax
import jax.numpy as jnp
import numpy as np
from jax import lax
from jax.experimental import pallas as pl
from jax.experimental.pallas import tpu as pltpu

D_MODEL = 1024
GLA_WIDTH = 512
POOL_WIDTH = 512
GLA_HEADS = 4
GLA_DV = 128
GLA_DK = 64
GLA_KEY_WIDTH = 256
GLA_GATE_RANK = 16
GLA_TAU = 16.0
GLA_CHUNK = 64
POOL_WINDOWS = (2, 4, 8, 16)
POOL_GROUP_WIDTH = 128
PEER_HEADS = 8
PEER_NKEYS = 128
PEER_HALF = 128
PEER_TOPK = 16
PEER_TOKEN_BLOCK = 128
N_MOD = 6
EPS = 1e-6
IN_SIZES = (256, 256, 512, 512, 16, 16, 512)
IN_SPLITS = tuple(int(s) for s in np.cumsum(IN_SIZES)[:-1])


def _rmsnorm(x, g):
    return x * lax.rsqrt(jnp.mean(x * x, axis=-1, keepdims=True) + EPS) * g


def _gla_chunked(q, k, v, log_a, strict):
    B, H, S, dk = q.shape
    dv = v.shape[-1]
    C = GLA_CHUNK
    n = S // C
    qc = q.reshape(B, H, n, C, dk)
    kc = k.reshape(B, H, n, C, dk)
    vc = v.reshape(B, H, n, C, dv)
    b = jnp.cumsum(log_a.reshape(B, H, n, C, dk), axis=3)
    b_last = b[:, :, :, -1:, :]
    q_dec = qc * jnp.exp(b)
    k_inv = kc * jnp.exp(-b)
    k_end = kc * jnp.exp(b_last - b)
    scores = jnp.einsum('bhnid,bhnjd->bhnij', q_dec, k_inv)
    mask = jnp.tril(jnp.ones((C, C), dtype=bool), k=-1 if strict else 0)
    scores = jnp.where(mask, scores, 0.0)
    o_intra = jnp.einsum('bhnij,bhnjv->bhniv', scores, vc)
    u = jnp.einsum('bhnjd,bhnjv->bhndv', k_end, vc)
    gamma = jnp.exp(b_last[:, :, :, 0, :])

    def step(state, inp):
        g_n, u_n = inp
        return g_n[..., None] * state + u_n, state

    init = jnp.zeros((B, H, dk, dv), jnp.float32)
    _, s_prev = lax.scan(step, init, (jnp.moveaxis(gamma, 2, 0), jnp.moveaxis(u, 2, 0)))
    s_prev = jnp.moveaxis(s_prev, 0, 2)
    o_inter = jnp.einsum('bhnid,bhndv->bhniv', q_dec, s_prev)
    return (o_intra + o_inter).reshape(B, H, S, dv)


def _gla_group(q, k, v, r, af, ab, w_af, b_af, w_ab, b_ab, g_head):
    B, S, _ = q.shape

    def heads(t, d):
        return t.reshape(B, S, GLA_HEADS, d).transpose(0, 2, 1, 3)

    qh = heads(q, GLA_DK) * (GLA_DK ** -0.5)
    kh = heads(k, GLA_DK)
    vh = heads(v, GLA_DV)
    log_af = jax.nn.log_sigmoid(af @ w_af + b_af) / GLA_TAU
    log_ab = jax.nn.log_sigmoid(ab @ w_ab + b_ab) / GLA_TAU
    o_f = _gla_chunked(qh, kh, vh, heads(log_af, GLA_DK), strict=False)
    o_b = jnp.flip(_gla_chunked(jnp.flip(qh, 2), jnp.flip(kh, 2), jnp.flip(vh, 2),
                                jnp.flip(heads(log_ab, GLA_DK), 2), strict=True), 2)
    o = jnp.transpose(o_f + o_b, (0, 2, 1, 3))
    o = _rmsnorm(o, g_head.reshape(GLA_HEADS, GLA_DV)).reshape(B, S, GLA_WIDTH)
    return o * jax.nn.silu(r)


def _pool_group(p, w_pool, pool_scale):
    B, S, W = p.shape
    cs = jnp.concatenate([jnp.zeros((B, 1, W), jnp.float32), jnp.cumsum(p, axis=1)], axis=1)
    pos = jnp.arange(S)
    outs = []
    for gi, w in enumerate(POOL_WINDOWS):
        sl = slice(gi * POOL_GROUP_WIDTH, (gi + 1) * POOL_GROUP_WIDTH)
        lo = jnp.clip(pos - w // 2, 0, S)
        hi = jnp.clip(pos + w // 2, 0, S)
        seg = cs[:, :, sl]
        total = seg[:, hi] - seg[:, lo]
        count = (hi - lo).astype(jnp.float32)[None, :, None]
        outs.append(total / count - p[:, :, sl])
    pooled = jnp.stack(outs, axis=2)
    y = jnp.einsum('bsgc,gcd->bsgd', pooled, w_pool)
    return y.reshape(B, S, POOL_WIDTH) * pool_scale


def _peer_ffn(x, g_pre, sc2, sh2, w_q, sub_keys, u_tab, v_tab):
    B, S, D = x.shape
    T = B * S
    h2, experts_t, gates_t = _peer_retrieve(x, g_pre, sc2, sh2, w_q, sub_keys)
    coef_dup = _peer_u_pass(experts_t, h2.reshape(T, D), gates_t, _pack_table(u_tab))
    y = _peer_v_pass(experts_t, coef_dup, _pack_table(v_tab))
    return y.reshape(B, S, D)


RETR_TM = 512
NEG_INF = float("-inf")


def _top16_rows(s, row_iota):
    n_rows = s.shape[0]
    vals, rows = [], []
    for _ in range(PEER_TOPK):
        m = jnp.max(s, axis=0, keepdims=True)
        r = jnp.min(jnp.where(s == m, row_iota, n_rows), axis=0, keepdims=True)
        vals.append(m)
        rows.append(r)
        s = jnp.where(row_iota == r, NEG_INF, s)
    return vals, rows


def _peer_retrieve_body(x_ref, g_ref, sc_ref, sh_ref, wq_ref, keys_ref, h_ref, idx_ref, gate_ref, q_ref):
    x = x_ref[0]
    h = x * lax.rsqrt(jnp.mean(x * x, axis=-1, keepdims=True) + EPS) * g_ref[...]
    h = h * (1.0 + sc_ref[0]) + sh_ref[0]
    h_ref[0] = h
    q_ref[...] = jnp.dot(h.astype(jnp.bfloat16), wq_ref[...],
                         preferred_element_type=jnp.float32).astype(jnp.bfloat16)
    key_iota = lax.broadcasted_iota(jnp.int32, (PEER_NKEYS, LANES), 0)
    cand_iota = lax.broadcasted_iota(jnp.int32, (PEER_TOPK * PEER_TOPK, LANES), 0)
    chunks = RETR_TM // LANES

    def head_chunk(i, carry):
        hd = lax.shift_right_logical(i, chunks.bit_length() - 1)
        ch = i & (chunks - 1)
        tok = pl.ds(pl.multiple_of(ch * LANES, LANES), LANES)
        vals, rows = [], []
        for p in range(2):
            col = pl.multiple_of((hd * 2 + p) * PEER_HALF, PEER_HALF)
            qs = q_ref[tok, pl.ds(col, PEER_HALF)]
            s = lax.dot_general(keys_ref[hd, p], qs, (((1,), (1,)), ((), ())),
                                preferred_element_type=jnp.float32)
            v, r = _top16_rows(s, key_iota)
            vals.append(v)
            rows.append(r)
        v2 = jnp.concatenate(vals[1], axis=0)
        cand = jnp.concatenate([vals[0][a] + v2 for a in range(PEER_TOPK)], axis=0)
        top_s, pos = _top16_rows(cand, cand_iota)
        pos = jnp.concatenate(pos, axis=0)
        pa = pos >> 4
        pb = pos & (PEER_TOPK - 1)
        i1 = jnp.zeros_like(pos)
        i2 = jnp.zeros_like(pos)
        for a in range(PEER_TOPK):
            i1 = jnp.where(pa == a, rows[0][a], i1)
            i2 = jnp.where(pb == a, rows[1][a], i2)
        ts = jnp.concatenate(top_s, axis=0)
        e = jnp.exp(ts - top_s[0])
        gates = e / jnp.sum(e, axis=0, keepdims=True)
        out_rows = pl.ds(pl.multiple_of(hd * PEER_TOPK, PEER_TOPK), PEER_TOPK)
        idx_ref[out_rows, tok] = i1 * PEER_NKEYS + i2
        gate_ref[out_rows, tok] = gates
        return carry

    lax.fori_loop(0, PEER_HEADS * chunks, head_chunk, 0)


def _peer_retrieve(x, g_pre, sc2, sh2, w_q, sub_keys):
    B, S, D = x.shape
    T = B * S
    nq = w_q.shape[1]
    spb = S // RETR_TM
    tok_spec = pl.BlockSpec((1, RETR_TM, D), lambda b, s: (b, s, 0))
    mod_spec = pl.BlockSpec((1, 1, D), lambda b, s: (b, 0, 0))
    slot_spec = pl.BlockSpec((PEER_SLOTS, RETR_TM), lambda b, s: (0, b * spb + s))
    return pl.pallas_call(
        _peer_retrieve_body,
        grid=(B, spb),
        in_specs=[
            tok_spec,
            pl.BlockSpec((1, D), lambda b, s: (0, 0)),
            mod_spec,
            mod_spec,
            pl.BlockSpec((D, nq), lambda b, s: (0, 0)),
            pl.BlockSpec(sub_keys.shape, lambda b, s: (0, 0, 0, 0)),
        ],
        out_specs=[tok_spec, slot_spec, slot_spec],
        out_shape=[
            jax.ShapeDtypeStruct((B, S, D), jnp.float32),
            jax.ShapeDtypeStruct((PEER_SLOTS, T), jnp.int32),
            jax.ShapeDtypeStruct((PEER_SLOTS, T), jnp.float32),
        ],
        scratch_shapes=[pltpu.VMEM((RETR_TM, nq), jnp.bfloat16)],
        compiler_params=pltpu.CompilerParams(vmem_limit_bytes=PEER_VMEM_LIMIT),
        name="peer_retrieve",
    )(x, g_pre.reshape(1, D), sc2, sh2, w_q.astype(jnp.bfloat16), sub_keys.astype(jnp.bfloat16))


PEER_SLOTS = PEER_HEADS * PEER_TOPK
PEER_CHUNKS = D_MODEL // 256
PEER_TB = 256
TILE_STRIDE = PEER_SLOTS + 8
LANES = 128
PEER_VMEM_LIMIT = 48 * 1024 * 1024


def _pack_table(tab):
    E, D = tab.shape
    t = tab.astype(jnp.bfloat16).reshape(E, 2, D // 256, LANES)
    t = jnp.transpose(t, (0, 2, 3, 1))
    return lax.bitcast_convert_type(t, jnp.uint32).reshape(E * (D // 256), LANES)


def _gather_token(idx_ref, tab_ref, tile_ref, t):
    for k in range(PEER_SLOTS):
        row = pl.multiple_of(idx_ref[k, t] * PEER_CHUNKS, PEER_CHUNKS)
        tile_ref[pl.ds(k, PEER_CHUNKS, stride=TILE_STRIDE), :] = tab_ref[pl.ds(row, PEER_CHUNKS), :]


def _tile_chunk_f32(tile_ref, c):
    w = tile_ref[pl.ds(c * TILE_STRIDE, PEER_SLOTS), :]
    return pltpu.bitcast(w, jnp.bfloat16).astype(jnp.float32)


def _peer_u_body(idx_ref, h_ref, gate_ref, tab_ref, coef_ref, tile_ref, sblk_ref):
    lane = lax.broadcasted_iota(jnp.int32, (2 * PEER_SLOTS, LANES), 1)
    odd = (lax.broadcasted_iota(jnp.int32, (8, LANES), 0) & 1) == 1
    for blk in range(PEER_TB // LANES):
        def token(tl, carry):
            t = blk * LANES + tl
            _gather_token(idx_ref, tab_ref, tile_ref, t)
            acc = jnp.zeros((2 * PEER_SLOTS // 8, 8, LANES), jnp.float32)
            for c in range(PEER_CHUNKS):
                h_lo = jnp.broadcast_to(h_ref[t, pl.ds(c, 1), :], (8, LANES))
                h_hi = jnp.broadcast_to(h_ref[t, pl.ds(PEER_CHUNKS + c, 1), :], (8, LANES))
                pattern = jnp.where(odd, h_hi, h_lo)
                x = _tile_chunk_f32(tile_ref, c).reshape(2 * PEER_SLOTS // 8, 8, LANES)
                acc = acc + x * pattern[None]
            s = jnp.sum(acc.reshape(2 * PEER_SLOTS, LANES), axis=1, keepdims=True)
            sblk_ref[...] = jnp.where(lane == tl, s, sblk_ref[...])
            return carry
        lax.fori_loop(0, LANES, token, 0)
        even_rows = sblk_ref[pl.ds(0, PEER_SLOTS, stride=2), :]
        odd_rows = sblk_ref[pl.ds(1, PEER_SLOTS, stride=2), :]
        a = even_rows + odd_rows
        cols = pl.ds(blk * LANES, LANES)
        coef = gate_ref[:, cols] * (0.5 * a * (1.0 + lax.erf(a * (2.0 ** -0.5))))
        sblk_ref[pl.ds(0, PEER_SLOTS, stride=2), :] = coef
        sblk_ref[pl.ds(1, PEER_SLOTS, stride=2), :] = coef
        coef_ref[:, cols] = sblk_ref[...]


def _peer_u_pass(experts_t, ht, gates_t, tab):
    T = ht.shape[0]
    return pl.pallas_call(
        _peer_u_body,
        grid=(T // PEER_TB,),
        in_specs=[
            pl.BlockSpec((PEER_SLOTS, PEER_TB), lambda i: (0, i), memory_space=pltpu.SMEM),
            pl.BlockSpec((PEER_TB, 8, LANES), lambda i: (i, 0, 0)),
            pl.BlockSpec((PEER_SLOTS, PEER_TB), lambda i: (0, i)),
            pl.BlockSpec(memory_space=pltpu.VMEM),
        ],
        out_specs=pl.BlockSpec((2 * PEER_SLOTS, PEER_TB), lambda i: (0, i)),
        out_shape=jax.ShapeDtypeStruct((2 * PEER_SLOTS, T), jnp.float32),
        scratch_shapes=[
            pltpu.VMEM((PEER_CHUNKS * TILE_STRIDE, LANES), jnp.uint32),
            pltpu.VMEM((2 * PEER_SLOTS, LANES), jnp.float32),
        ],
        compiler_params=pltpu.CompilerParams(vmem_limit_bytes=PEER_VMEM_LIMIT),
        name="peer_u_pass",
    )(experts_t, ht.reshape(T, 8, LANES), gates_t, tab)


def _peer_v_body(idx_ref, coef_ref, tab_ref, y_ref, tile_ref):
    lane = lax.broadcasted_iota(jnp.int32, (2 * PEER_SLOTS, LANES), 1)
    for blk in range(PEER_TB // LANES):
        cols = pl.ds(blk * LANES, LANES)

        def token(tl, carry):
            t = blk * LANES + tl
            _gather_token(idx_ref, tab_ref, tile_ref, t)
            ccol = jnp.sum(jnp.where(lane == tl, coef_ref[:, cols], 0.0), axis=1, keepdims=True)
            ccol = ccol.reshape(2 * PEER_SLOTS // 8, 8, 1)
            for c in range(PEER_CHUNKS):
                x = _tile_chunk_f32(tile_ref, c).reshape(2 * PEER_SLOTS // 8, 8, LANES)
                part = jnp.sum(x * ccol, axis=0)
                part = part + pltpu.roll(part, 4, axis=0)
                part = part + pltpu.roll(part, 2, axis=0)
                y_ref[t, pl.ds(c, 1), :] = part[0:1]
                y_ref[t, pl.ds(PEER_CHUNKS + c, 1), :] = part[1:2]
            return carry
        lax.fori_loop(0, LANES, token, 0)


def _peer_v_pass(experts_t, coef_dup, tab):
    T = experts_t.shape[1]
    y = pl.pallas_call(
        _peer_v_body,
        grid=(T // PEER_TB,),
        in_specs=[
            pl.BlockSpec((PEER_SLOTS, PEER_TB), lambda i: (0, i), memory_space=pltpu.SMEM),
            pl.BlockSpec((2 * PEER_SLOTS, PEER_TB), lambda i: (0, i)),
            pl.BlockSpec(memory_space=pltpu.VMEM),
        ],
        out_specs=pl.BlockSpec((PEER_TB, 8, LANES), lambda i: (i, 0, 0)),
        out_shape=jax.ShapeDtypeStruct((T, 8, LANES), jnp.float32),
        scratch_shapes=[pltpu.VMEM((PEER_CHUNKS * TILE_STRIDE, LANES), jnp.uint32)],
        compiler_params=pltpu.CompilerParams(vmem_limit_bytes=PEER_VMEM_LIMIT),
        name="peer_v_pass",
    )(experts_t, coef_dup, tab)
    return y.reshape(T, D_MODEL)


def _post_residual_body(x_ref, y_ref, g_ref, gate_ref, o_ref):
    y = y_ref[0]
    yn = y * lax.rsqrt(jnp.mean(y * y, axis=-1, keepdims=True) + EPS) * g_ref[...]
    o_ref[0] = x_ref[0] + gate_ref[0] * yn


def _post_residual(x, y, g, gate):
    B, S, D = x.shape
    ts = 512
    return pl.pallas_call(
        _post_residual_body,
        grid=(B, S // ts),
        in_specs=[
            pl.BlockSpec((1, ts, D), lambda b, s: (b, s, 0)),
            pl.BlockSpec((1, ts, D), lambda b, s: (b, s, 0)),
            pl.BlockSpec((1, D), lambda b, s: (0, 0)),
            pl.BlockSpec((1, 1, D), lambda b, s: (b, 0, 0)),
        ],
        out_specs=pl.BlockSpec((1, ts, D), lambda b, s: (b, s, 0)),
        out_shape=jax.ShapeDtypeStruct((B, S, D), x.dtype),
        name="post_residual",
    )(x, y, g.reshape(1, D), gate)


def kernel(x, c, w_mod, b_mod, g_pre_mix, g_post_mix, w_in, w_alpha_f, b_alpha_f, w_alpha_b, b_alpha_b,
           g_gla_head, w_pool, pool_scale, w_out, g_pre_ffn, g_post_ffn, w_peer_q, peer_sub_keys,
           peer_u, peer_v):
    cond = jax.nn.silu(c)
    for l in range(w_mod.shape[0]):
        mod = cond @ w_mod[l] + b_mod[l]
        sh1, sc1, gt1, sh2, sc2, gt2 = [m[:, None, :] for m in jnp.split(mod, N_MOD, axis=-1)]
        h = _rmsnorm(x, g_pre_mix[l]) * (1 + sc1) + sh1
        z = h @ w_in[l]
        q, k, v, r, af, ab, p = jnp.split(z, IN_SPLITS, axis=-1)
        o_gla = _gla_group(q, k, v, r, af, ab, w_alpha_f[l], b_alpha_f[l], w_alpha_b[l], b_alpha_b[l],
                           g_gla_head[l])
        o_pool = _pool_group(p, w_pool[l], pool_scale[l])
        mix = jnp.concatenate([o_gla, o_pool], axis=-1) @ w_out[l]
        x = _post_residual(x, mix, g_post_mix[l], gt1)
        y = _peer_ffn(x, g_pre_ffn[l], sc2, sh2, w_peer_q[l], peer_sub_keys[l], peer_u[l], peer_v[l])
        x = _post_residual(x, y, g_post_ffn[l], gt2)
    return x
```

```python
import jax
import jax.numpy as jnp
from jax import lax
from jax.experimental import pallas as pl
from jax.experimental.pallas import tpu as pltpu

D_MODEL = 1024
GLA_WIDTH = 512
POOL_WIDTH = 512
GLA_HEADS = 4
GLA_DV = 128
GLA_DK = 64
GLA_KEY_WIDTH = 256
GLA_GATE_RANK = 16
GLA_TAU = 16.0
GLA_CHUNK = 64
POOL_WINDOWS = (2, 4, 8, 16)
POOL_GROUP_WIDTH = 128
PEER_HEADS = 8
PEER_NKEYS = 128
PEER_HALF = 128
PEER_TOPK = 16
N_MOD = 6
EPS = 1e-6

LANES = 128
VMEM_LIMIT = 48 * 1024 * 1024
ROW_TILE = 512

Z_Q, Z_K, Z_V, Z_R, Z_P, Z_A = 0, 256, 512, 1024, 1536, 2048
Z_COLS = 2176
POOL_HALO = 8


def _mod_body(c_ref, w_ref, b_ref, o_ref):
    c = c_ref[...]
    cond = c * (1.0 / (1.0 + jnp.exp(-c)))
    o_ref[...] = jnp.dot(cond.astype(jnp.bfloat16), w_ref[...].astype(jnp.bfloat16),
                         preferred_element_type=jnp.float32) + b_ref[...]


def _modulation(c, w_mod, b_mod):
    B, D = c.shape
    n = w_mod.shape[1]
    return pl.pallas_call(
        _mod_body,
        grid=(n // D,),
        in_specs=[
            pl.BlockSpec((B, D), lambda j: (0, 0)),
            pl.BlockSpec((D, D), lambda j: (0, j)),
            pl.BlockSpec((1, D), lambda j: (0, j)),
        ],
        out_specs=pl.BlockSpec((B, D), lambda j: (0, j)),
        out_shape=jax.ShapeDtypeStruct((B, n), jnp.float32),
        name="modulation",
    )(c, w_mod, b_mod.reshape(1, n))


def _in_proj_body(x_ref, g_ref, sc_ref, sh_ref, w_ref, z_ref):
    x = x_ref[0]
    h = x * lax.rsqrt(jnp.mean(x * x, axis=-1, keepdims=True) + EPS) * g_ref[...]
    h = h * (1.0 + sc_ref[0]) + sh_ref[0]
    z_ref[0] = jnp.dot(h.astype(jnp.bfloat16), w_ref[...], preferred_element_type=jnp.float32)


def _in_proj(x, g, sc, sh, w_in):
    B, S, D = x.shape
    w = jnp.concatenate([w_in[:, :Z_P], w_in[:, Z_P + 32:], w_in[:, Z_P:Z_P + 32],
                         jnp.zeros((D, Z_COLS - Z_A - 32), w_in.dtype)], axis=1).astype(jnp.bfloat16)
    mod_spec = pl.BlockSpec((1, 1, D), lambda b, s: (b, 0, 0))
    return pl.pallas_call(
        _in_proj_body,
        grid=(B, S // ROW_TILE),
        in_specs=[
            pl.BlockSpec((1, ROW_TILE, D), lambda b, s: (b, s, 0)),
            pl.BlockSpec((1, D), lambda b, s: (0, 0)),
            mod_spec,
            mod_spec,
            pl.BlockSpec((D, Z_COLS), lambda b, s: (0, 0)),
        ],
        out_specs=pl.BlockSpec((1, ROW_TILE, Z_COLS), lambda b, s: (b, s, 0)),
        out_shape=jax.ShapeDtypeStruct((B, S, Z_COLS), jnp.float32),
        compiler_params=pltpu.CompilerParams(vmem_limit_bytes=VMEM_LIMIT),
        name="in_proj",
    )(x, g.reshape(1, D), sc, sh, w)


def _log_sigmoid(x):
    return jnp.minimum(x, 0.0) - jnp.log1p(jnp.exp(-jnp.abs(x)))


def _bf16_dot(a, b, dims):
    return lax.dot_general(a.astype(jnp.bfloat16), b.astype(jnp.bfloat16), (dims, ((), ())),
                           preferred_element_type=jnp.float32)


def _gla_body(q_ref, k_ref, v_ref, r_ref, a_ref, wf_ref, bf_ref, wb_ref, bb_ref, gh_ref, o_ref,
              of_ref, st_ref):
    C = GLA_CHUNK
    n_chunks = q_ref.shape[1] // C
    row = lax.broadcasted_iota(jnp.int32, (C, C), 0)
    col = lax.broadcasted_iota(jnp.int32, (C, C), 1)
    lane = lax.broadcasted_iota(jnp.int32, (1, LANES), 1)
    head_lanes = [lane < GLA_DK, lane >= GLA_DK]

    def chunk(rows, w_ref, b_ref, cum_mask, keep_mask, total_row, emit):
        a = a_ref[0, rows, :]
        log_a = _log_sigmoid(_bf16_dot(a, w_ref[0], ((1,), (0,))) + b_ref[0]) * (1.0 / GLA_TAU)
        b = jnp.dot(cum_mask, log_a, precision=lax.Precision.HIGHEST,
                    preferred_element_type=jnp.float32)
        b_tot = b[total_row:total_row + 1, :]
        q = q_ref[0, rows, :] * (GLA_DK ** -0.5)
        k = k_ref[0, rows, :]
        q_dec = q * jnp.exp(b)
        k_inv = k * jnp.exp(-b)
        k_end = k * jnp.exp(b_tot - b)
        gamma = jnp.exp(b_tot)
        for hd in range(2):
            qm = jnp.where(head_lanes[hd], q_dec, 0.0)
            v = v_ref[0, rows, hd * GLA_DV:(hd + 1) * GLA_DV]
            scores = jnp.where(keep_mask, _bf16_dot(qm, k_inv, ((1,), (1,))), 0.0)
            state = st_ref[hd]
            o = _bf16_dot(scores, v, ((1,), (0,))) + _bf16_dot(qm, state, ((1,), (1,)))
            st_ref[hd] = state * gamma + _bf16_dot(v, k_end, ((0,), (0,)))
            emit(hd, o)

    st_ref[...] = jnp.zeros_like(st_ref)
    prefix = (row >= col).astype(jnp.float32)

    def fwd(i, carry):
        rows = pl.ds(pl.multiple_of(i * C, C), C)

        def emit(hd, o):
            of_ref[rows, hd * GLA_DV:(hd + 1) * GLA_DV] = o
        chunk(rows, wf_ref, bf_ref, prefix, row >= col, C - 1, emit)
        return carry
    lax.fori_loop(0, n_chunks, fwd, 0)

    st_ref[...] = jnp.zeros_like(st_ref)
    suffix = (row <= col).astype(jnp.float32)

    def bwd(i, carry):
        rows = pl.ds(pl.multiple_of((n_chunks - 1 - i) * C, C), C)

        def emit(hd, o):
            cols = slice(hd * GLA_DV, (hd + 1) * GLA_DV)
            tot = of_ref[rows, cols] + o
            y = tot * lax.rsqrt(jnp.mean(tot * tot, axis=-1, keepdims=True) + EPS) * gh_ref[0, :, cols]
            r = r_ref[0, rows, cols]
            o_ref[0, rows, cols] = y * (r * (1.0 / (1.0 + jnp.exp(-r))))
        chunk(rows, wb_ref, bb_ref, suffix, row < col, 0, emit)
        return carry
    lax.fori_loop(0, n_chunks, bwd, 0)


def _gla(z, w_af, b_af, w_ab, b_ab, g_head):
    B, S, _ = z.shape
    pairs = GLA_HEADS // 2

    def pad_w(w, first_row):
        wp = jnp.zeros((pairs, LANES, LANES), jnp.float32)
        return wp.at[:, first_row:first_row + GLA_GATE_RANK, :].set(
            jnp.transpose(w.reshape(GLA_GATE_RANK, pairs, LANES), (1, 0, 2)))

    def lane_blk(off):
        return pl.BlockSpec((1, S, LANES), lambda b, j: (b, 0, off // LANES + j))

    def wide_blk(off):
        return pl.BlockSpec((1, S, 2 * LANES), lambda b, j: (b, 0, off // (2 * LANES) + j))

    w_spec = pl.BlockSpec((1, LANES, LANES), lambda b, j: (j, 0, 0))
    b_spec = pl.BlockSpec((1, 1, LANES), lambda b, j: (j, 0, 0))
    return pl.pallas_call(
        _gla_body,
        grid=(B, pairs),
        in_specs=[
            lane_blk(Z_Q), lane_blk(Z_K), wide_blk(Z_V), wide_blk(Z_R),
            pl.BlockSpec((1, S, LANES), lambda b, j: (b, 0, Z_A // LANES)),
            w_spec, b_spec, w_spec, b_spec,
            pl.BlockSpec((1, 1, 2 * LANES), lambda b, j: (j, 0, 0)),
        ],
        out_specs=pl.BlockSpec((1, S, 2 * LANES), lambda b, j: (b, 0, j)),
        out_shape=jax.ShapeDtypeStruct((B, S, GLA_WIDTH), jnp.float32),
        scratch_shapes=[
            pltpu.VMEM((S, 2 * LANES), jnp.float32),
            pltpu.VMEM((2, GLA_DV, LANES), jnp.float32),
        ],
        compiler_params=pltpu.CompilerParams(vmem_limit_bytes=VMEM_LIMIT),
        name="gla",
    )(z, z, z, z, z,
      pad_w(w_af, 0), b_af.reshape(pairs, 1, LANES),
      pad_w(w_ab, GLA_GATE_RANK), b_ab.reshape(pairs, 1, LANES),
      g_head.reshape(pairs, 1, 2 * LANES))


def _mix_out_body(x_ref, o_ref, p_ref, pp_ref, pn_ref, wp_ref, ps_ref, wo_ref, g_ref, gate_ref, out_ref, pe_ref):
    s = pl.program_id(1)
    n_s = pl.num_programs(1)
    seq_len = n_s * ROW_TILE
    pe_ref[pl.ds(0, POOL_HALO), :] = jnp.where(s > 0, pp_ref[0], 0.0)
    pe_ref[pl.ds(POOL_HALO, ROW_TILE), :] = p_ref[0]
    pe_ref[pl.ds(POOL_HALO + ROW_TILE, POOL_HALO), :] = jnp.where(s < n_s - 1, pn_ref[0], 0.0)
    pos = s * ROW_TILE + lax.broadcasted_iota(jnp.int32, (ROW_TILE, POOL_GROUP_WIDTH), 0)
    mix = jnp.dot(o_ref[0].astype(jnp.bfloat16), wo_ref[pl.ds(0, GLA_WIDTH), :],
                  preferred_element_type=jnp.float32)
    for gi, w in enumerate(POOL_WINDOWS):
        cols = pl.ds(gi * POOL_GROUP_WIDTH, POOL_GROUP_WIDTH)
        total = pe_ref[pl.ds(POOL_HALO - w // 2, ROW_TILE), cols]
        for d in range(-w // 2 + 1, w // 2):
            total = total + pe_ref[pl.ds(POOL_HALO + d, ROW_TILE), cols]
        count = (jnp.minimum(pos + w // 2, seq_len) - jnp.maximum(pos - w // 2, 0)).astype(jnp.float32)
        pooled = total / count - pe_ref[pl.ds(POOL_HALO, ROW_TILE), cols]
        y = jnp.dot(pooled.astype(jnp.bfloat16), wp_ref[gi], preferred_element_type=jnp.float32)
        y = y * ps_ref[:, gi * POOL_GROUP_WIDTH:(gi + 1) * POOL_GROUP_WIDTH]
        mix = mix + jnp.dot(y.astype(jnp.bfloat16),
                            wo_ref[pl.ds(GLA_WIDTH + gi * POOL_GROUP_WIDTH, POOL_GROUP_WIDTH), :],
                            preferred_element_type=jnp.float32)
    yn = mix * lax.rsqrt(jnp.mean(mix * mix, axis=-1, keepdims=True) + EPS) * g_ref[...]
    out_ref[0] = x_ref[0] + gate_ref[0] * yn


def _mix_out(x, o_gla, z, w_pool, pool_scale, w_out, g, gate):
    B, S, D = x.shape
    halo_per_tile = ROW_TILE // POOL_HALO
    last_halo = S // POOL_HALO - 1
    row_spec = pl.BlockSpec((1, ROW_TILE, D), lambda b, s: (b, s, 0))
    return pl.pallas_call(
        _mix_out_body,
        grid=(B, S // ROW_TILE),
        in_specs=[
            row_spec,
            pl.BlockSpec((1, ROW_TILE, GLA_WIDTH), lambda b, s: (b, s, 0)),
            pl.BlockSpec((1, ROW_TILE, POOL_WIDTH), lambda b, s: (b, s, Z_P // POOL_WIDTH)),
            pl.BlockSpec((1, POOL_HALO, POOL_WIDTH),
                         lambda b, s: (b, jnp.maximum(s * halo_per_tile - 1, 0), Z_P // POOL_WIDTH)),
            pl.BlockSpec((1, POOL_HALO, POOL_WIDTH),
                         lambda b, s: (b, jnp.minimum((s + 1) * halo_per_tile, last_halo), Z_P // POOL_WIDTH)),
            pl.BlockSpec(w_pool.shape, lambda b, s: (0, 0, 0)),
            pl.BlockSpec((1, POOL_WIDTH), lambda b, s: (0, 0)),
            pl.BlockSpec((D, D), lambda b, s: (0, 0)),
            pl.BlockSpec((1, D), lambda b, s: (0, 0)),
            pl.BlockSpec((1, 1, D), lambda b, s: (b, 0, 0)),
        ],
        out_specs=row_spec,
        out_shape=jax.ShapeDtypeStruct((B, S, D), jnp.float32),
        scratch_shapes=[pltpu.VMEM((ROW_TILE + 2 * POOL_HALO, POOL_WIDTH), jnp.float32)],
        compiler_params=pltpu.CompilerParams(vmem_limit_bytes=VMEM_LIMIT),
        name="mix_out",
    )(x, o_gla, z, z, z, w_pool.astype(jnp.bfloat16), pool_scale.reshape(1, POOL_WIDTH),
      w_out.astype(jnp.bfloat16), g.reshape(1, D), gate)


def _peer_ffn(x, g_pre, sc2, sh2, w_q, sub_keys, u_tab, v_tab):
    B, S, D = x.shape
    T = B * S
    h2, experts_t, gates_t = _peer_retrieve(x, g_pre, sc2, sh2, w_q, sub_keys)
    coef_dup = _peer_u_pass(experts_t, h2.reshape(T, D), gates_t, _pack_table(u_tab))
    y = _peer_v_pass(experts_t, coef_dup, _pack_table(v_tab))
    return y.reshape(B, S, D)


NEG_INF = float("-inf")


def _top16_rows(s, row_iota):
    n_rows = s.shape[0]
    vals, rows = [], []
    for _ in range(PEER_TOPK):
        m = jnp.max(s, axis=0, keepdims=True)
        r = jnp.min(jnp.where(s == m, row_iota, n_rows), axis=0, keepdims=True)
        vals.append(m)
        rows.append(r)
        s = jnp.where(row_iota == r, NEG_INF, s)
    return vals, rows


def _peer_retrieve_body(x_ref, g_ref, sc_ref, sh_ref, wq_ref, keys_ref, h_ref, idx_ref, gate_ref, q_ref):
    x = x_ref[0]
    h = x * lax.rsqrt(jnp.mean(x * x, axis=-1, keepdims=True) + EPS) * g_ref[...]
    h = h * (1.0 + sc_ref[0]) + sh_ref[0]
    h_ref[0] = h
    q_ref[...] = jnp.dot(h.astype(jnp.bfloat16), wq_ref[...],
                         preferred_element_type=jnp.float32).astype(jnp.bfloat16)
    key_iota = lax.broadcasted_iota(jnp.int32, (PEER_NKEYS, LANES), 0)
    cand_iota = lax.broadcasted_iota(jnp.int32, (PEER_TOPK * PEER_TOPK, LANES), 0)
    chunks = ROW_TILE // LANES

    def head_chunk(i, carry):
        hd = lax.shift_right_logical(i, chunks.bit_length() - 1)
        ch = i & (chunks - 1)
        tok = pl.ds(pl.multiple_of(ch * LANES, LANES), LANES)
        vals, rows = [], []
        for p in range(2):
            col = pl.multiple_of((hd * 2 + p) * PEER_HALF, PEER_HALF)
            qs = q_ref[tok, pl.ds(col, PEER_HALF)]
            s = lax.dot_general(keys_ref[hd, p], qs, (((1,), (1,)), ((), ())),
                                preferred_element_type=jnp.float32)
            v, r = _top16_rows(s, key_iota)
            vals.append(v)
            rows.append(r)
        v2 = jnp.concatenate(vals[1], axis=0)
        cand = jnp.concatenate([vals[0][a] + v2 for a in range(PEER_TOPK)], axis=0)
        top_s, pos = _top16_rows(cand, cand_iota)
        pos = jnp.concatenate(pos, axis=0)
        pa = pos >> 4
        pb = pos & (PEER_TOPK - 1)
        i1 = jnp.zeros_like(pos)
        i2 = jnp.zeros_like(pos)
        for a in range(PEER_TOPK):
            i1 = jnp.where(pa == a, rows[0][a], i1)
            i2 = jnp.where(pb == a, rows[1][a], i2)
        ts = jnp.concatenate(top_s, axis=0)
        e = jnp.exp(ts - top_s[0])
        gates = e / jnp.sum(e, axis=0, keepdims=True)
        out_rows = pl.ds(pl.multiple_of(hd * PEER_TOPK, PEER_TOPK), PEER_TOPK)
        idx_ref[out_rows, tok] = i1 * PEER_NKEYS + i2
        gate_ref[out_rows, tok] = gates
        return carry

    lax.fori_loop(0, PEER_HEADS * chunks, head_chunk, 0)


def _peer_retrieve(x, g_pre, sc2, sh2, w_q, sub_keys):
    B, S, D = x.shape
    T = B * S
    nq = w_q.shape[1]
    spb = S // ROW_TILE
    tok_spec = pl.BlockSpec((1, ROW_TILE, D), lambda b, s: (b, s, 0))
    mod_spec = pl.BlockSpec((1, 1, D), lambda b, s: (b, 0, 0))
    slot_spec = pl.BlockSpec((PEER_SLOTS, ROW_TILE), lambda b, s: (0, b * spb + s))
    return pl.pallas_call(
        _peer_retrieve_body,
        grid=(B, spb),
        in_specs=[
            tok_spec,
            pl.BlockSpec((1, D), lambda b, s: (0, 0)),
            mod_spec,
            mod_spec,
            pl.BlockSpec((D, nq), lambda b, s: (0, 0)),
            pl.BlockSpec(sub_keys.shape, lambda b, s: (0, 0, 0, 0)),
        ],
        out_specs=[tok_spec, slot_spec, slot_spec],
        out_shape=[
            jax.ShapeDtypeStruct((B, S, D), jnp.float32),
            jax.ShapeDtypeStruct((PEER_SLOTS, T), jnp.int32),
            jax.ShapeDtypeStruct((PEER_SLOTS, T), jnp.float32),
        ],
        scratch_shapes=[pltpu.VMEM((ROW_TILE, nq), jnp.bfloat16)],
        compiler_params=pltpu.CompilerParams(vmem_limit_bytes=VMEM_LIMIT),
        name="peer_retrieve",
    )(x, g_pre.reshape(1, D), sc2, sh2, w_q.astype(jnp.bfloat16), sub_keys.astype(jnp.bfloat16))


PEER_SLOTS = PEER_HEADS * PEER_TOPK
PEER_CHUNKS = D_MODEL // 256
PEER_TB = 256
TILE_STRIDE = PEER_SLOTS + 8


def _pack_table(tab):
    E, D = tab.shape
    t = tab.astype(jnp.bfloat16).reshape(E, 2, D // 256, LANES)
    t = jnp.transpose(t, (0, 2, 3, 1))
    return lax.bitcast_convert_type(t, jnp.uint32).reshape(E * (D // 256), LANES)


def _gather_token(idx_ref, tab_ref, tile_ref, t):
    for k in range(PEER_SLOTS):
        row = pl.multiple_of(idx_ref[k, t] * PEER_CHUNKS, PEER_CHUNKS)
        tile_ref[pl.ds(k, PEER_CHUNKS, stride=TILE_STRIDE), :] = tab_ref[pl.ds(row, PEER_CHUNKS), :]


def _tile_chunk_f32(tile_ref, c):
    w = tile_ref[pl.ds(c * TILE_STRIDE, PEER_SLOTS), :]
    return pltpu.bitcast(w, jnp.bfloat16).astype(jnp.float32)


def _peer_u_body(idx_ref, h_ref, gate_ref, tab_ref, coef_ref, tile_ref, sblk_ref):
    lane = lax.broadcasted_iota(jnp.int32, (2 * PEER_SLOTS, LANES), 1)
    odd = (lax.broadcasted_iota(jnp.int32, (8, LANES), 0) & 1) == 1
    for blk in range(PEER_TB // LANES):
        def token(tl, carry):
            t = blk * LANES + tl
            _gather_token(idx_ref, tab_ref, tile_ref, t)
            acc = jnp.zeros((2 * PEER_SLOTS // 8, 8, LANES), jnp.float32)
            for c in range(PEER_CHUNKS):
                h_lo = jnp.broadcast_to(h_ref[t, pl.ds(c, 1), :], (8, LANES))
                h_hi = jnp.broadcast_to(h_ref[t, pl.ds(PEER_CHUNKS + c, 1), :], (8, LANES))
                pattern = jnp.where(odd, h_hi, h_lo)
                x = _tile_chunk_f32(tile_ref, c).reshape(2 * PEER_SLOTS // 8, 8, LANES)
                acc = acc + x * pattern[None]
            s = jnp.sum(acc.reshape(2 * PEER_SLOTS, LANES), axis=1, keepdims=True)
            sblk_ref[...] = jnp.where(lane == tl, s, sblk_ref[...])
            return carry
        lax.fori_loop(0, LANES, token, 0)
        even_rows = sblk_ref[pl.ds(0, PEER_SLOTS, stride=2), :]
        odd_rows = sblk_ref[pl.ds(1, PEER_SLOTS, stride=2), :]
        a = even_rows + odd_rows
        cols = pl.ds(blk * LANES, LANES)
        coef = gate_ref[:, cols] * (0.5 * a * (1.0 + lax.erf(a * (2.0 ** -0.5))))
        sblk_ref[pl.ds(0, PEER_SLOTS, stride=2), :] = coef
        sblk_ref[pl.ds(1, PEER_SLOTS, stride=2), :] = coef
        coef_ref[:, cols] = sblk_ref[...]


def _peer_u_pass(experts_t, ht, gates_t, tab):
    T = ht.shape[0]
    return pl.pallas_call(
        _peer_u_body,
        grid=(T // PEER_TB,),
        in_specs=[
            pl.BlockSpec((PEER_SLOTS, PEER_TB), lambda i: (0, i), memory_space=pltpu.SMEM),
            pl.BlockSpec((PEER_TB, 8, LANES), lambda i: (i, 0, 0)),
            pl.BlockSpec((PEER_SLOTS, PEER_TB), lambda i: (0, i)),
            pl.BlockSpec(memory_space=pltpu.VMEM),
        ],
        out_specs=pl.BlockSpec((2 * PEER_SLOTS, PEER_TB), lambda i: (0, i)),
        out_shape=jax.ShapeDtypeStruct((2 * PEER_SLOTS, T), jnp.float32),
        scratch_shapes=[
            pltpu.VMEM((PEER_CHUNKS * TILE_STRIDE, LANES), jnp.uint32),
            pltpu.VMEM((2 * PEER_SLOTS, LANES), jnp.float32),
        ],
        compiler_params=pltpu.CompilerParams(vmem_limit_bytes=VMEM_LIMIT),
        name="peer_u_pass",
    )(experts_t, ht.reshape(T, 8, LANES), gates_t, tab)


def _peer_v_body(idx_ref, coef_ref, tab_ref, y_ref, tile_ref):
    lane = lax.broadcasted_iota(jnp.int32, (2 * PEER_SLOTS, LANES), 1)
    for blk in range(PEER_TB // LANES):
        cols = pl.ds(blk * LANES, LANES)

        def token(tl, carry):
            t = blk * LANES + tl
            _gather_token(idx_ref, tab_ref, tile_ref, t)
            ccol = jnp.sum(jnp.where(lane == tl, coef_ref[:, cols], 0.0), axis=1, keepdims=True)
            ccol = ccol.reshape(2 * PEER_SLOTS // 8, 8, 1)
            for c in range(PEER_CHUNKS):
                x = _tile_chunk_f32(tile_ref, c).reshape(2 * PEER_SLOTS // 8, 8, LANES)
                part = jnp.sum(x * ccol, axis=0)
                part = part + pltpu.roll(part, 4, axis=0)
                part = part + pltpu.roll(part, 2, axis=0)
                y_ref[t, pl.ds(c, 1), :] = part[0:1]
                y_ref[t, pl.ds(PEER_CHUNKS + c, 1), :] = part[1:2]
            return carry
        lax.fori_loop(0, LANES, token, 0)


def _peer_v_pass(experts_t, coef_dup, tab):
    T = experts_t.shape[1]
    y = pl.pallas_call(
        _peer_v_body,
        grid=(T // PEER_TB,),
        in_specs=[
            pl.BlockSpec((PEER_SLOTS, PEER_TB), lambda i: (0, i), memory_space=pltpu.SMEM),
            pl.BlockSpec((2 * PEER_SLOTS, PEER_TB), lambda i: (0, i)),
            pl.BlockSpec(memory_space=pltpu.VMEM),
        ],
        out_specs=pl.BlockSpec((PEER_TB, 8, LANES), lambda i: (i, 0, 0)),
        out_shape=jax.ShapeDtypeStruct((T, 8, LANES), jnp.float32),
        scratch_shapes=[pltpu.VMEM((PEER_CHUNKS * TILE_STRIDE, LANES), jnp.uint32)],
        compiler_params=pltpu.CompilerParams(vmem_limit_bytes=VMEM_LIMIT),
        name="peer_v_pass",
    )(experts_t, coef_dup, tab)
    return y.reshape(T, D_MODEL)


def _post_residual_body(x_ref, y_ref, g_ref, gate_ref, o_ref):
    y = y_ref[0]
    yn = y * lax.rsqrt(jnp.mean(y * y, axis=-1, keepdims=True) + EPS) * g_ref[...]
    o_ref[0] = x_ref[0] + gate_ref[0] * yn


def _post_residual(x, y, g, gate):
    B, S, D = x.shape
    row_spec = pl.BlockSpec((1, ROW_TILE, D), lambda b, s: (b, s, 0))
    return pl.pallas_call(
        _post_residual_body,
        grid=(B, S // ROW_TILE),
        in_specs=[
            row_spec,
            row_spec,
            pl.BlockSpec((1, D), lambda b, s: (0, 0)),
            pl.BlockSpec((1, 1, D), lambda b, s: (b, 0, 0)),
        ],
        out_specs=row_spec,
        out_shape=jax.ShapeDtypeStruct((B, S, D), x.dtype),
        name="post_residual",
    )(x, y, g.reshape(1, D), gate)


def kernel(x, c, w_mod, b_mod, g_pre_mix, g_post_mix, w_in, w_alpha_f, b_alpha_f, w_alpha_b, b_alpha_b,
           g_gla_head, w_pool, pool_scale, w_out, g_pre_ffn, g_post_ffn, w_peer_q, peer_sub_keys,
           peer_u, peer_v):
    for l in range(w_mod.shape[0]):
        mod = _modulation(c, w_mod[l], b_mod[l])
        sh1, sc1, gt1, sh2, sc2, gt2 = [m[:, None, :] for m in jnp.split(mod, N_MOD, axis=-1)]
        z = _in_proj(x, g_pre_mix[l], sc1, sh1, w_in[l])
        o_gla = _gla(z, w_alpha_f[l], b_alpha_f[l], w_alpha_b[l], b_alpha_b[l], g_gla_head[l])
        x = _mix_out(x, o_gla, z, w_pool[l], pool_scale[l], w_out[l], g_post_mix[l], gt1)
        y = _peer_ffn(x, g_pre_ffn[l], sc2, sh2, w_peer_q[l], peer_sub_keys[l], peer_u[l], peer_v[l])
        x = _post_residual(x, y, g_post_ffn[l], gt2)
    return x
```

```python
import jax
import jax.numpy as jnp
from jax import lax
from jax.experimental import pallas as pl
from jax.experimental.pallas import tpu as pltpu

D_MODEL = 1024
GLA_WIDTH = 512
POOL_WIDTH = 512
GLA_HEADS = 4
GLA_DV = 128
GLA_DK = 64
GLA_KEY_WIDTH = 256
GLA_GATE_RANK = 16
GLA_TAU = 16.0
GLA_CHUNK = 64
POOL_WINDOWS = (2, 4, 8, 16)
POOL_GROUP_WIDTH = 128
PEER_HEADS = 8
PEER_NKEYS = 128
PEER_HALF = 128
PEER_TOPK = 16
N_MOD = 6
EPS = 1e-6

LANES = 128
VMEM_LIMIT = 48 * 1024 * 1024
ROW_TILE = 512

Z_Q, Z_K, Z_V, Z_R, Z_P, Z_A = 0, 256, 512, 1024, 1536, 2048
Z_COLS = 2176
POOL_HALO = 8


def _mod_body(c_ref, w_ref, b_ref, o_ref):
    c = c_ref[...]
    cond = c * (1.0 / (1.0 + jnp.exp(-c)))
    o_ref[...] = jnp.dot(cond.astype(jnp.bfloat16), w_ref[...].astype(jnp.bfloat16),
                         preferred_element_type=jnp.float32) + b_ref[...]


def _modulation(c, w_mod, b_mod):
    B, D = c.shape
    n = w_mod.shape[1]
    return pl.pallas_call(
        _mod_body,
        grid=(n // D,),
        in_specs=[
            pl.BlockSpec((B, D), lambda j: (0, 0)),
            pl.BlockSpec((D, D), lambda j: (0, j)),
            pl.BlockSpec((1, D), lambda j: (0, j)),
        ],
        out_specs=pl.BlockSpec((B, D), lambda j: (0, j)),
        out_shape=jax.ShapeDtypeStruct((B, n), jnp.float32),
        name="modulation",
    )(c, w_mod, b_mod.reshape(1, n))


def _in_proj_body(x_ref, g_ref, sc_ref, sh_ref, w_ref, z_ref):
    x = x_ref[0]
    h = x * lax.rsqrt(jnp.mean(x * x, axis=-1, keepdims=True) + EPS) * g_ref[...]
    h = h * (1.0 + sc_ref[0]) + sh_ref[0]
    z_ref[0] = jnp.dot(h.astype(jnp.bfloat16), w_ref[...], preferred_element_type=jnp.float32)


def _in_proj(x, g, sc, sh, w_in):
    B, S, D = x.shape
    w = jnp.concatenate([w_in[:, :Z_P], w_in[:, Z_P + 32:], w_in[:, Z_P:Z_P + 32],
                         jnp.zeros((D, Z_COLS - Z_A - 32), w_in.dtype)], axis=1).astype(jnp.bfloat16)
    mod_spec = pl.BlockSpec((1, 1, D), lambda b, s: (b, 0, 0))
    return pl.pallas_call(
        _in_proj_body,
        grid=(B, S // ROW_TILE),
        in_specs=[
            pl.BlockSpec((1, ROW_TILE, D), lambda b, s: (b, s, 0)),
            pl.BlockSpec((1, D), lambda b, s: (0, 0)),
            mod_spec,
            mod_spec,
            pl.BlockSpec((D, Z_COLS), lambda b, s: (0, 0)),
        ],
        out_specs=pl.BlockSpec((1, ROW_TILE, Z_COLS), lambda b, s: (b, s, 0)),
        out_shape=jax.ShapeDtypeStruct((B, S, Z_COLS), jnp.float32),
        compiler_params=pltpu.CompilerParams(vmem_limit_bytes=VMEM_LIMIT),
        name="in_proj",
    )(x, g.reshape(1, D), sc, sh, w)


def _log_sigmoid(x):
    return jnp.minimum(x, 0.0) - jnp.log1p(jnp.exp(-jnp.abs(x)))


def _bf16_dot(a, b, dims):
    return lax.dot_general(a.astype(jnp.bfloat16), b.astype(jnp.bfloat16), (dims, ((), ())),
                           preferred_element_type=jnp.float32)


def _gla_body(q_ref, k_ref, v_ref, r_ref, a_ref, wf_ref, bf_ref, wb_ref, bb_ref, gh_ref, o_ref,
              of_ref, st_ref):
    C = GLA_CHUNK
    n_chunks = q_ref.shape[1] // C
    row = lax.broadcasted_iota(jnp.int32, (C, C), 0)
    col = lax.broadcasted_iota(jnp.int32, (C, C), 1)
    lane = lax.broadcasted_iota(jnp.int32, (1, LANES), 1)
    head_lanes = [lane < GLA_DK, lane >= GLA_DK]

    def chunk(rows, w_ref, b_ref, cum_mask, keep_mask, total_row, emit):
        a = a_ref[0, rows, :]
        log_a = _log_sigmoid(_bf16_dot(a, w_ref[0], ((1,), (0,))) + b_ref[0]) * (1.0 / GLA_TAU)
        b = jnp.dot(cum_mask, log_a, precision=lax.Precision.HIGHEST,
                    preferred_element_type=jnp.float32)
        b_tot = b[total_row:total_row + 1, :]
        q = q_ref[0, rows, :] * (GLA_DK ** -0.5)
        k = k_ref[0, rows, :]
        q_dec = q * jnp.exp(b)
        k_inv = k * jnp.exp(-b)
        k_end = k * jnp.exp(b_tot - b)
        gamma = jnp.exp(b_tot)
        for hd in range(2):
            qm = jnp.where(head_lanes[hd], q_dec, 0.0)
            v = v_ref[0, rows, hd * GLA_DV:(hd + 1) * GLA_DV]
            scores = jnp.where(keep_mask, _bf16_dot(qm, k_inv, ((1,), (1,))), 0.0)
            state = st_ref[hd]
            o = _bf16_dot(scores, v, ((1,), (0,))) + _bf16_dot(qm, state, ((1,), (1,)))
            st_ref[hd] = state * gamma + _bf16_dot(v, k_end, ((0,), (0,)))
            emit(hd, o)

    st_ref[...] = jnp.zeros_like(st_ref)
    prefix = (row >= col).astype(jnp.float32)

    def fwd(i, carry):
        rows = pl.ds(pl.multiple_of(i * C, C), C)

        def emit(hd, o):
            of_ref[rows, hd * GLA_DV:(hd + 1) * GLA_DV] = o
        chunk(rows, wf_ref, bf_ref, prefix, row >= col, C - 1, emit)
        return carry
    lax.fori_loop(0, n_chunks, fwd, 0)

    st_ref[...] = jnp.zeros_like(st_ref)
    suffix = (row <= col).astype(jnp.float32)

    def bwd(i, carry):
        rows = pl.ds(pl.multiple_of((n_chunks - 1 - i) * C, C), C)

        def emit(hd, o):
            cols = slice(hd * GLA_DV, (hd + 1) * GLA_DV)
            tot = of_ref[rows, cols] + o
            y = tot * lax.rsqrt(jnp.mean(tot * tot, axis=-1, keepdims=True) + EPS) * gh_ref[0, :, cols]
            r = r_ref[0, rows, cols]
            o_ref[0, rows, cols] = y * (r * (1.0 / (1.0 + jnp.exp(-r))))
        chunk(rows, wb_ref, bb_ref, suffix, row < col, 0, emit)
        return carry
    lax.fori_loop(0, n_chunks, bwd, 0)


def _gla(z, w_af, b_af, w_ab, b_ab, g_head):
    B, S, _ = z.shape
    pairs = GLA_HEADS // 2

    def pad_w(w, first_row):
        wp = jnp.zeros((pairs, LANES, LANES), jnp.float32)
        return wp.at[:, first_row:first_row + GLA_GATE_RANK, :].set(
            jnp.transpose(w.reshape(GLA_GATE_RANK, pairs, LANES), (1, 0, 2)))

    def lane_blk(off):
        return pl.BlockSpec((1, S, LANES), lambda b, j: (b, 0, off // LANES + j))

    def wide_blk(off):
        return pl.BlockSpec((1, S, 2 * LANES), lambda b, j: (b, 0, off // (2 * LANES) + j))

    w_spec = pl.BlockSpec((1, LANES, LANES), lambda b, j: (j, 0, 0))
    b_spec = pl.BlockSpec((1, 1, LANES), lambda b, j: (j, 0, 0))
    return pl.pallas_call(
        _gla_body,
        grid=(B, pairs),
        in_specs=[
            lane_blk(Z_Q), lane_blk(Z_K), wide_blk(Z_V), wide_blk(Z_R),
            pl.BlockSpec((1, S, LANES), lambda b, j: (b, 0, Z_A // LANES)),
            w_spec, b_spec, w_spec, b_spec,
            pl.BlockSpec((1, 1, 2 * LANES), lambda b, j: (j, 0, 0)),
        ],
        out_specs=pl.BlockSpec((1, S, 2 * LANES), lambda b, j: (b, 0, j)),
        out_shape=jax.ShapeDtypeStruct((B, S, GLA_WIDTH), jnp.float32),
        scratch_shapes=[
            pltpu.VMEM((S, 2 * LANES), jnp.float32),
            pltpu.VMEM((2, GLA_DV, LANES), jnp.float32),
        ],
        compiler_params=pltpu.CompilerParams(vmem_limit_bytes=VMEM_LIMIT),
        name="gla",
    )(z, z, z, z, z,
      pad_w(w_af, 0), b_af.reshape(pairs, 1, LANES),
      pad_w(w_ab, GLA_GATE_RANK), b_ab.reshape(pairs, 1, LANES),
      g_head.reshape(pairs, 1, 2 * LANES))


def _mix_out_body(x_ref, o_ref, p_ref, pp_ref, pn_ref, wp_ref, ps_ref, wo_ref, g_ref, gate_ref, out_ref, pe_ref):
    s = pl.program_id(1)
    n_s = pl.num_programs(1)
    seq_len = n_s * ROW_TILE
    pe_ref[pl.ds(0, POOL_HALO), :] = jnp.where(s > 0, pp_ref[0], 0.0)
    pe_ref[pl.ds(POOL_HALO, ROW_TILE), :] = p_ref[0]
    pe_ref[pl.ds(POOL_HALO + ROW_TILE, POOL_HALO), :] = jnp.where(s < n_s - 1, pn_ref[0], 0.0)
    pos = s * ROW_TILE + lax.broadcasted_iota(jnp.int32, (ROW_TILE, POOL_GROUP_WIDTH), 0)
    mix = jnp.dot(o_ref[0].astype(jnp.bfloat16), wo_ref[pl.ds(0, GLA_WIDTH), :],
                  preferred_element_type=jnp.float32)
    for gi, w in enumerate(POOL_WINDOWS):
        cols = pl.ds(gi * POOL_GROUP_WIDTH, POOL_GROUP_WIDTH)
        total = pe_ref[pl.ds(POOL_HALO - w // 2, ROW_TILE), cols]
        for d in range(-w // 2 + 1, w // 2):
            total = total + pe_ref[pl.ds(POOL_HALO + d, ROW_TILE), cols]
        count = (jnp.minimum(pos + w // 2, seq_len) - jnp.maximum(pos - w // 2, 0)).astype(jnp.float32)
        pooled = total / count - pe_ref[pl.ds(POOL_HALO, ROW_TILE), cols]
        y = jnp.dot(pooled.astype(jnp.bfloat16), wp_ref[gi], preferred_element_type=jnp.float32)
        y = y * ps_ref[:, gi * POOL_GROUP_WIDTH:(gi + 1) * POOL_GROUP_WIDTH]
        mix = mix + jnp.dot(y.astype(jnp.bfloat16),
                            wo_ref[pl.ds(GLA_WIDTH + gi * POOL_GROUP_WIDTH, POOL_GROUP_WIDTH), :],
                            preferred_element_type=jnp.float32)
    yn = mix * lax.rsqrt(jnp.mean(mix * mix, axis=-1, keepdims=True) + EPS) * g_ref[...]
    out_ref[0] = x_ref[0] + gate_ref[0] * yn


def _mix_out(x, o_gla, z, w_pool, pool_scale, w_out, g, gate):
    B, S, D = x.shape
    halo_per_tile = ROW_TILE // POOL_HALO
    last_halo = S // POOL_HALO - 1
    row_spec = pl.BlockSpec((1, ROW_TILE, D), lambda b, s: (b, s, 0))
    return pl.pallas_call(
        _mix_out_body,
        grid=(B, S // ROW_TILE),
        in_specs=[
            row_spec,
            pl.BlockSpec((1, ROW_TILE, GLA_WIDTH), lambda b, s: (b, s, 0)),
            pl.BlockSpec((1, ROW_TILE, POOL_WIDTH), lambda b, s: (b, s, Z_P // POOL_WIDTH)),
            pl.BlockSpec((1, POOL_HALO, POOL_WIDTH),
                         lambda b, s: (b, jnp.maximum(s * halo_per_tile - 1, 0), Z_P // POOL_WIDTH)),
            pl.BlockSpec((1, POOL_HALO, POOL_WIDTH),
                         lambda b, s: (b, jnp.minimum((s + 1) * halo_per_tile, last_halo), Z_P // POOL_WIDTH)),
            pl.BlockSpec(w_pool.shape, lambda b, s: (0, 0, 0)),
            pl.BlockSpec((1, POOL_WIDTH), lambda b, s: (0, 0)),
            pl.BlockSpec((D, D), lambda b, s: (0, 0)),
            pl.BlockSpec((1, D), lambda b, s: (0, 0)),
            pl.BlockSpec((1, 1, D), lambda b, s: (b, 0, 0)),
        ],
        out_specs=row_spec,
        out_shape=jax.ShapeDtypeStruct((B, S, D), jnp.float32),
        scratch_shapes=[pltpu.VMEM((ROW_TILE + 2 * POOL_HALO, POOL_WIDTH), jnp.float32)],
        compiler_params=pltpu.CompilerParams(vmem_limit_bytes=VMEM_LIMIT),
        name="mix_out",
    )(x, o_gla, z, z, z, w_pool.astype(jnp.bfloat16), pool_scale.reshape(1, POOL_WIDTH),
      w_out.astype(jnp.bfloat16), g.reshape(1, D), gate)


def _peer_ffn(x, g_pre, sc2, sh2, w_q, sub_keys, u_tab, v_tab):
    B, S, D = x.shape
    T = B * S
    h2, experts_t, gates_t = _peer_retrieve(x, g_pre, sc2, sh2, w_q, sub_keys)
    coef_dup = _peer_u_pass(experts_t, h2.reshape(T, D), gates_t, _pack_table(u_tab))
    y = _peer_v_pass(experts_t, coef_dup, _pack_table(v_tab))
    return y.reshape(B, S, D)


NEG_INF = float("-inf")


def _top16_rows(s, row_iota):
    n_rows = s.shape[0]
    vals, rows = [], []
    for _ in range(PEER_TOPK):
        m = jnp.max(s, axis=0, keepdims=True)
        r = jnp.min(jnp.where(s == m, row_iota, n_rows), axis=0, keepdims=True)
        vals.append(m)
        rows.append(r)
        s = jnp.where(row_iota == r, NEG_INF, s)
    return vals, rows


def _peer_retrieve_body(x_ref, g_ref, sc_ref, sh_ref, wq_ref, keys_ref, h_ref, idx_ref, gate_ref, q_ref):
    x = x_ref[0]
    h = x * lax.rsqrt(jnp.mean(x * x, axis=-1, keepdims=True) + EPS) * g_ref[...]
    h = h * (1.0 + sc_ref[0]) + sh_ref[0]
    h_ref[0] = h
    q_ref[...] = jnp.dot(h.astype(jnp.bfloat16), wq_ref[...],
                         preferred_element_type=jnp.float32).astype(jnp.bfloat16)
    key_iota = lax.broadcasted_iota(jnp.int32, (PEER_NKEYS, LANES), 0)
    cand_iota = lax.broadcasted_iota(jnp.int32, (PEER_TOPK * PEER_TOPK, LANES), 0)
    chunks = ROW_TILE // LANES

    def head_chunk(i, carry):
        hd = lax.shift_right_logical(i, chunks.bit_length() - 1)
        ch = i & (chunks - 1)
        tok = pl.ds(pl.multiple_of(ch * LANES, LANES), LANES)
        vals, rows = [], []
        for p in range(2):
            col = pl.multiple_of((hd * 2 + p) * PEER_HALF, PEER_HALF)
            qs = q_ref[tok, pl.ds(col, PEER_HALF)]
            s = lax.dot_general(keys_ref[hd, p], qs, (((1,), (1,)), ((), ())),
                                preferred_element_type=jnp.float32)
            v, r = _top16_rows(s, key_iota)
            vals.append(v)
            rows.append(r)
        v2 = jnp.concatenate(vals[1], axis=0)
        cand = jnp.concatenate([vals[0][a] + v2 for a in range(PEER_TOPK)], axis=0)
        top_s, pos = _top16_rows(cand, cand_iota)
        pos = jnp.concatenate(pos, axis=0)
        pa = pos >> 4
        pb = pos & (PEER_TOPK - 1)
        i1 = jnp.zeros_like(pos)
        i2 = jnp.zeros_like(pos)
        for a in range(PEER_TOPK):
            i1 = jnp.where(pa == a, rows[0][a], i1)
            i2 = jnp.where(pb == a, rows[1][a], i2)
        ts = jnp.concatenate(top_s, axis=0)
        e = jnp.exp(ts - top_s[0])
        gates = e / jnp.sum(e, axis=0, keepdims=True)
        out_rows = pl.ds(pl.multiple_of(hd * PEER_TOPK, PEER_TOPK), PEER_TOPK)
        idx_ref[out_rows, tok] = (i1 * PEER_NKEYS + i2) * PEER_CHUNKS
        gate_ref[out_rows, tok] = gates
        return carry

    lax.fori_loop(0, PEER_HEADS * chunks, head_chunk, 0)


def _peer_retrieve(x, g_pre, sc2, sh2, w_q, sub_keys):
    B, S, D = x.shape
    T = B * S
    nq = w_q.shape[1]
    spb = S // ROW_TILE
    tok_spec = pl.BlockSpec((1, ROW_TILE, D), lambda b, s: (b, s, 0))
    mod_spec = pl.BlockSpec((1, 1, D), lambda b, s: (b, 0, 0))
    slot_spec = pl.BlockSpec((PEER_SLOTS, ROW_TILE), lambda b, s: (0, b * spb + s))
    return pl.pallas_call(
        _peer_retrieve_body,
        grid=(B, spb),
        in_specs=[
            tok_spec,
            pl.BlockSpec((1, D), lambda b, s: (0, 0)),
            mod_spec,
            mod_spec,
            pl.BlockSpec((D, nq), lambda b, s: (0, 0)),
            pl.BlockSpec(sub_keys.shape, lambda b, s: (0, 0, 0, 0)),
        ],
        out_specs=[tok_spec, slot_spec, slot_spec],
        out_shape=[
            jax.ShapeDtypeStruct((B, S, D), jnp.float32),
            jax.ShapeDtypeStruct((PEER_SLOTS, T), jnp.int32),
            jax.ShapeDtypeStruct((PEER_SLOTS, T), jnp.float32),
        ],
        scratch_shapes=[pltpu.VMEM((ROW_TILE, nq), jnp.bfloat16)],
        compiler_params=pltpu.CompilerParams(vmem_limit_bytes=VMEM_LIMIT),
        name="peer_retrieve",
    )(x, g_pre.reshape(1, D), sc2, sh2, w_q.astype(jnp.bfloat16), sub_keys.astype(jnp.bfloat16))


PEER_SLOTS = PEER_HEADS * PEER_TOPK
PEER_CHUNKS = D_MODEL // 256
PEER_TB = 256
TILE_STRIDE = PEER_SLOTS + 8


def _pack_table(tab):
    E, D = tab.shape
    t = tab.astype(jnp.bfloat16).reshape(E, 2, D // 256, LANES)
    t = jnp.transpose(t, (0, 2, 3, 1))
    return lax.bitcast_convert_type(t, jnp.uint32).reshape(E * (D // 256), LANES)


def _gather_token(idx_ref, tab_ref, tile_ref, t):
    for k in range(PEER_SLOTS):
        row = pl.multiple_of(idx_ref[k, t], PEER_CHUNKS)
        tile_ref[pl.ds(k, PEER_CHUNKS, stride=TILE_STRIDE), :] = tab_ref[pl.ds(row, PEER_CHUNKS), :]


def _tile_chunk_f32(tile_ref, c):
    w = tile_ref[pl.ds(c * TILE_STRIDE, PEER_SLOTS), :]
    return pltpu.bitcast(w, jnp.bfloat16).astype(jnp.float32)


def _token_pairs(idx_ref, tab_ref, tile_a, tile_b, consume):
    _gather_token(idx_ref, tab_ref, tile_a, 0)

    def pair(j, carry):
        t0 = 2 * j
        _gather_token(idx_ref, tab_ref, tile_b, t0 + 1)
        consume(tile_a, t0)
        _gather_token(idx_ref, tab_ref, tile_a, jnp.minimum(t0 + 2, PEER_TB - 1))
        consume(tile_b, t0 + 1)
        return carry
    lax.fori_loop(0, PEER_TB // 2, pair, 0)


def _token_block_cols(t):
    return pl.ds(pl.multiple_of(lax.shift_right_logical(t, 7) * LANES, LANES), LANES)


def _peer_u_body(idx_ref, h_ref, gate_ref, tab_ref, coef_ref, tile_a, tile_b, sblk_ref):
    lane = lax.broadcasted_iota(jnp.int32, (2 * PEER_SLOTS, LANES), 1)
    odd = (lax.broadcasted_iota(jnp.int32, (8, LANES), 0) & 1) == 1

    def consume(tile_ref, t):
        acc = jnp.zeros((2 * PEER_SLOTS // 8, 8, LANES), jnp.float32)
        for c in range(PEER_CHUNKS):
            h_lo = jnp.broadcast_to(h_ref[t, pl.ds(c, 1), :], (8, LANES))
            h_hi = jnp.broadcast_to(h_ref[t, pl.ds(PEER_CHUNKS + c, 1), :], (8, LANES))
            pattern = jnp.where(odd, h_hi, h_lo)
            x = _tile_chunk_f32(tile_ref, c).reshape(2 * PEER_SLOTS // 8, 8, LANES)
            acc = acc + x * pattern[None]
        s = jnp.sum(acc.reshape(2 * PEER_SLOTS, LANES), axis=1, keepdims=True)
        tl = t & (LANES - 1)
        sblk_ref[...] = jnp.where(lane == tl, s, sblk_ref[...])

        @pl.when(tl == LANES - 1)
        def _():
            even_rows = sblk_ref[pl.ds(0, PEER_SLOTS, stride=2), :]
            odd_rows = sblk_ref[pl.ds(1, PEER_SLOTS, stride=2), :]
            a = even_rows + odd_rows
            cols = _token_block_cols(t)
            coef = gate_ref[:, cols] * (0.5 * a * (1.0 + lax.erf(a * (2.0 ** -0.5))))
            sblk_ref[pl.ds(0, PEER_SLOTS, stride=2), :] = coef
            sblk_ref[pl.ds(1, PEER_SLOTS, stride=2), :] = coef
            coef_ref[:, cols] = sblk_ref[...]

    _token_pairs(idx_ref, tab_ref, tile_a, tile_b, consume)


def _peer_u_pass(experts_t, ht, gates_t, tab):
    T = ht.shape[0]
    return pl.pallas_call(
        _peer_u_body,
        grid=(T // PEER_TB,),
        in_specs=[
            pl.BlockSpec((PEER_SLOTS, PEER_TB), lambda i: (0, i), memory_space=pltpu.SMEM),
            pl.BlockSpec((PEER_TB, 8, LANES), lambda i: (i, 0, 0)),
            pl.BlockSpec((PEER_SLOTS, PEER_TB), lambda i: (0, i)),
            pl.BlockSpec(memory_space=pltpu.VMEM),
        ],
        out_specs=pl.BlockSpec((2 * PEER_SLOTS, PEER_TB), lambda i: (0, i)),
        out_shape=jax.ShapeDtypeStruct((2 * PEER_SLOTS, T), jnp.float32),
        scratch_shapes=[
            pltpu.VMEM((PEER_CHUNKS * TILE_STRIDE, LANES), jnp.uint32),
            pltpu.VMEM((PEER_CHUNKS * TILE_STRIDE, LANES), jnp.uint32),
            pltpu.VMEM((2 * PEER_SLOTS, LANES), jnp.float32),
        ],
        compiler_params=pltpu.CompilerParams(vmem_limit_bytes=VMEM_LIMIT),
        name="peer_u_pass",
    )(experts_t, ht.reshape(T, 8, LANES), gates_t, tab)


def _peer_v_body(idx_ref, coef_ref, tab_ref, y_ref, tile_a, tile_b):
    lane = lax.broadcasted_iota(jnp.int32, (2 * PEER_SLOTS, LANES), 1)

    def consume(tile_ref, t):
        tl = t & (LANES - 1)
        coef_blk = coef_ref[:, _token_block_cols(t)]
        ccol = jnp.sum(jnp.where(lane == tl, coef_blk, 0.0), axis=1, keepdims=True)
        ccol = ccol.reshape(2 * PEER_SLOTS // 8, 8, 1)
        for c in range(PEER_CHUNKS):
            x = _tile_chunk_f32(tile_ref, c).reshape(2 * PEER_SLOTS // 8, 8, LANES)
            part = jnp.sum(x * ccol, axis=0)
            part = part + pltpu.roll(part, 4, axis=0)
            part = part + pltpu.roll(part, 2, axis=0)
            y_ref[t, pl.ds(c, 1), :] = part[0:1]
            y_ref[t, pl.ds(PEER_CHUNKS + c, 1), :] = part[1:2]

    _token_pairs(idx_ref, tab_ref, tile_a, tile_b, consume)


def _peer_v_pass(experts_t, coef_dup, tab):
    T = experts_t.shape[1]
    y = pl.pallas_call(
        _peer_v_body,
        grid=(T // PEER_TB,),
        in_specs=[
            pl.BlockSpec((PEER_SLOTS, PEER_TB), lambda i: (0, i), memory_space=pltpu.SMEM),
            pl.BlockSpec((2 * PEER_SLOTS, PEER_TB), lambda i: (0, i)),
            pl.BlockSpec(memory_space=pltpu.VMEM),
        ],
        out_specs=pl.BlockSpec((PEER_TB, 8, LANES), lambda i: (i, 0, 0)),
        out_shape=jax.ShapeDtypeStruct((T, 8, LANES), jnp.float32),
        scratch_shapes=[pltpu.VMEM((PEER_CHUNKS * TILE_STRIDE, LANES), jnp.uint32)] * 2,
        compiler_params=pltpu.CompilerParams(vmem_limit_bytes=VMEM_LIMIT),
        name="peer_v_pass",
    )(experts_t, coef_dup, tab)
    return y.reshape(T, D_MODEL)


def _post_residual_body(x_ref, y_ref, g_ref, gate_ref, o_ref):
    y = y_ref[0]
    yn = y * lax.rsqrt(jnp.mean(y * y, axis=-1, keepdims=True) + EPS) * g_ref[...]
    o_ref[0] = x_ref[0] + gate_ref[0] * yn


def _post_residual(x, y, g, gate):
    B, S, D = x.shape
    row_spec = pl.BlockSpec((1, ROW_TILE, D), lambda b, s: (b, s, 0))
    return pl.pallas_call(
        _post_residual_body,
        grid=(B, S // ROW_TILE),
        in_specs=[
            row_spec,
            row_spec,
            pl.BlockSpec((1, D), lambda b, s: (0, 0)),
            pl.BlockSpec((1, 1, D), lambda b, s: (b, 0, 0)),
        ],
        out_specs=row_spec,
        out_shape=jax.ShapeDtypeStruct((B, S, D), x.dtype),
        name="post_residual",
    )(x, y, g.reshape(1, D), gate)


def kernel(x, c, w_mod, b_mod, g_pre_mix, g_post_mix, w_in, w_alpha_f, b_alpha_f, w_alpha_b, b_alpha_b,
           g_gla_head, w_pool, pool_scale, w_out, g_pre_ffn, g_post_ffn, w_peer_q, peer_sub_keys,
           peer_u, peer_v):
    for l in range(w_mod.shape[0]):
        mod = _modulation(c, w_mod[l], b_mod[l])
        sh1, sc1, gt1, sh2, sc2, gt2 = [m[:, None, :] for m in jnp.split(mod, N_MOD, axis=-1)]
        z = _in_proj(x, g_pre_mix[l], sc1, sh1, w_in[l])
        o_gla = _gla(z, w_alpha_f[l], b_alpha_f[l], w_alpha_b[l], b_alpha_b[l], g_gla_head[l])
        x = _mix_out(x, o_gla, z, w_pool[l], pool_scale[l], w_out[l], g_post_mix[l], gt1)
        y = _peer_ffn(x, g_pre_ffn[l], sc2, sh2, w_peer_q[l], peer_sub_keys[l], peer_u[l], peer_v[l])
        x = _post_residual(x, y, g_post_ffn[l], gt2)
    return x
```

```python
import jax
import jax.numpy as jnp
from jax import lax
from jax.experimental import pallas as pl
from jax.experimental.pallas import tpu as pltpu

D_MODEL = 1024
GLA_WIDTH = 512
POOL_WIDTH = 512
GLA_HEADS = 4
GLA_DV = 128
GLA_DK = 64
GLA_KEY_WIDTH = 256
GLA_GATE_RANK = 16
GLA_TAU = 16.0
GLA_CHUNK = 64
POOL_WINDOWS = (2, 4, 8, 16)
POOL_GROUP_WIDTH = 128
PEER_HEADS = 8
PEER_NKEYS = 128
PEER_HALF = 128
PEER_TOPK = 16
N_MOD = 6
EPS = 1e-6

LANES = 128
VMEM_LIMIT = 48 * 1024 * 1024
ROW_TILE = 512

Z_Q, Z_K, Z_V, Z_R, Z_P, Z_A = 0, 256, 512, 1024, 1536, 2048
Z_COLS = 2176
POOL_HALO = 8
GLA_UNROLL = 4


def _mod_body(c_ref, w_ref, b_ref, o_ref):
    c = c_ref[...]
    cond = c * (1.0 / (1.0 + jnp.exp(-c)))
    o_ref[...] = jnp.dot(cond.astype(jnp.bfloat16), w_ref[...].astype(jnp.bfloat16),
                         preferred_element_type=jnp.float32) + b_ref[...]


def _modulation(c, w_mod, b_mod):
    B, D = c.shape
    n = w_mod.shape[1]
    return pl.pallas_call(
        _mod_body,
        grid=(n // D,),
        in_specs=[
            pl.BlockSpec((B, D), lambda j: (0, 0)),
            pl.BlockSpec((D, D), lambda j: (0, j)),
            pl.BlockSpec((1, D), lambda j: (0, j)),
        ],
        out_specs=pl.BlockSpec((B, D), lambda j: (0, j)),
        out_shape=jax.ShapeDtypeStruct((B, n), jnp.float32),
        name="modulation",
    )(c, w_mod, b_mod.reshape(1, n))


def _in_proj_body(x_ref, g_ref, sc_ref, sh_ref, w_ref, z_ref):
    x = x_ref[0]
    h = x * lax.rsqrt(jnp.mean(x * x, axis=-1, keepdims=True) + EPS) * g_ref[...]
    h = h * (1.0 + sc_ref[0]) + sh_ref[0]
    z_ref[0] = jnp.dot(h.astype(jnp.bfloat16), w_ref[...], preferred_element_type=jnp.float32)


def _in_proj(x, g, sc, sh, w_in):
    B, S, D = x.shape
    w = jnp.concatenate([w_in[:, :Z_P], w_in[:, Z_P + 32:], w_in[:, Z_P:Z_P + 32],
                         jnp.zeros((D, Z_COLS - Z_A - 32), w_in.dtype)], axis=1).astype(jnp.bfloat16)
    mod_spec = pl.BlockSpec((1, 1, D), lambda b, s: (b, 0, 0))
    return pl.pallas_call(
        _in_proj_body,
        grid=(B, S // ROW_TILE),
        in_specs=[
            pl.BlockSpec((1, ROW_TILE, D), lambda b, s: (b, s, 0)),
            pl.BlockSpec((1, D), lambda b, s: (0, 0)),
            mod_spec,
            mod_spec,
            pl.BlockSpec((D, Z_COLS), lambda b, s: (0, 0)),
        ],
        out_specs=pl.BlockSpec((1, ROW_TILE, Z_COLS), lambda b, s: (b, s, 0)),
        out_shape=jax.ShapeDtypeStruct((B, S, Z_COLS), jnp.float32),
        compiler_params=pltpu.CompilerParams(vmem_limit_bytes=VMEM_LIMIT),
        name="in_proj",
    )(x, g.reshape(1, D), sc, sh, w)


def _log_sigmoid(x):
    return jnp.minimum(x, 0.0) - jnp.log1p(jnp.exp(-jnp.abs(x)))


def _bf16_dot(a, b, dims):
    return lax.dot_general(a.astype(jnp.bfloat16), b.astype(jnp.bfloat16), (dims, ((), ())),
                           preferred_element_type=jnp.float32)


def _gla_body(q_ref, k_ref, v_ref, r_ref, a_ref, wf_ref, bf_ref, wb_ref, bb_ref, gh_ref, o_ref,
              of_ref, st_ref):
    C = GLA_CHUNK
    n_chunks = q_ref.shape[1] // C
    row = lax.broadcasted_iota(jnp.int32, (C, C), 0)
    col = lax.broadcasted_iota(jnp.int32, (C, C), 1)
    lane = lax.broadcasted_iota(jnp.int32, (1, LANES), 1)
    head_lanes = [lane < GLA_DK, lane >= GLA_DK]

    def chunk(rows, w_ref, b_ref, cum_mask, keep_mask, total_row, emit):
        a = a_ref[0, rows, :]
        log_a = _log_sigmoid(_bf16_dot(a, w_ref[0], ((1,), (0,))) + b_ref[0]) * (1.0 / GLA_TAU)
        b = jnp.dot(cum_mask, log_a, precision=lax.Precision.HIGHEST,
                    preferred_element_type=jnp.float32)
        b_tot = b[total_row:total_row + 1, :]
        q = q_ref[0, rows, :] * (GLA_DK ** -0.5)
        k = k_ref[0, rows, :]
        q_dec = q * jnp.exp(b)
        k_inv = k * jnp.exp(-b)
        k_end = k * jnp.exp(b_tot - b)
        gamma = jnp.exp(b_tot)
        for hd in range(2):
            qm = jnp.where(head_lanes[hd], q_dec, 0.0)
            v = v_ref[0, rows, hd * GLA_DV:(hd + 1) * GLA_DV]
            scores = jnp.where(keep_mask, _bf16_dot(qm, k_inv, ((1,), (1,))), 0.0)
            state = st_ref[hd]
            o = _bf16_dot(scores, v, ((1,), (0,))) + _bf16_dot(qm, state, ((1,), (1,)))
            st_ref[hd] = state * gamma + _bf16_dot(v, k_end, ((0,), (0,)))
            emit(hd, o)

    def chunk_rows(n):
        return pl.ds(pl.multiple_of(n * C, C), C)

    st_ref[...] = jnp.zeros_like(st_ref)
    prefix = (row >= col).astype(jnp.float32)

    def fwd(i, carry):
        for u in range(GLA_UNROLL):
            rows = chunk_rows(i * GLA_UNROLL + u)

            def emit(hd, o, rows=rows):
                of_ref[rows, hd * GLA_DV:(hd + 1) * GLA_DV] = o
            chunk(rows, wf_ref, bf_ref, prefix, row >= col, C - 1, emit)
        return carry
    lax.fori_loop(0, n_chunks // GLA_UNROLL, fwd, 0)

    st_ref[...] = jnp.zeros_like(st_ref)
    suffix = (row <= col).astype(jnp.float32)

    def bwd(i, carry):
        for u in range(GLA_UNROLL):
            rows = chunk_rows(n_chunks - 1 - (i * GLA_UNROLL + u))

            def emit(hd, o, rows=rows):
                cols = slice(hd * GLA_DV, (hd + 1) * GLA_DV)
                tot = of_ref[rows, cols] + o
                y = tot * lax.rsqrt(jnp.mean(tot * tot, axis=-1, keepdims=True) + EPS) * gh_ref[0, :, cols]
                r = r_ref[0, rows, cols]
                o_ref[0, rows, cols] = y * (r * (1.0 / (1.0 + jnp.exp(-r))))
            chunk(rows, wb_ref, bb_ref, suffix, row < col, 0, emit)
        return carry
    lax.fori_loop(0, n_chunks // GLA_UNROLL, bwd, 0)


def _gla(z, w_af, b_af, w_ab, b_ab, g_head):
    B, S, _ = z.shape
    pairs = GLA_HEADS // 2

    def pad_w(w, first_row):
        wp = jnp.zeros((pairs, LANES, LANES), jnp.float32)
        return wp.at[:, first_row:first_row + GLA_GATE_RANK, :].set(
            jnp.transpose(w.reshape(GLA_GATE_RANK, pairs, LANES), (1, 0, 2)))

    def lane_blk(off):
        return pl.BlockSpec((1, S, LANES), lambda b, j: (b, 0, off // LANES + j))

    def wide_blk(off):
        return pl.BlockSpec((1, S, 2 * LANES), lambda b, j: (b, 0, off // (2 * LANES) + j))

    w_spec = pl.BlockSpec((1, LANES, LANES), lambda b, j: (j, 0, 0))
    b_spec = pl.BlockSpec((1, 1, LANES), lambda b, j: (j, 0, 0))
    return pl.pallas_call(
        _gla_body,
        grid=(B, pairs),
        in_specs=[
            lane_blk(Z_Q), lane_blk(Z_K), wide_blk(Z_V), wide_blk(Z_R),
            pl.BlockSpec((1, S, LANES), lambda b, j: (b, 0, Z_A // LANES)),
            w_spec, b_spec, w_spec, b_spec,
            pl.BlockSpec((1, 1, 2 * LANES), lambda b, j: (j, 0, 0)),
        ],
        out_specs=pl.BlockSpec((1, S, 2 * LANES), lambda b, j: (b, 0, j)),
        out_shape=jax.ShapeDtypeStruct((B, S, GLA_WIDTH), jnp.float32),
        scratch_shapes=[
            pltpu.VMEM((S, 2 * LANES), jnp.float32),
            pltpu.VMEM((2, GLA_DV, LANES), jnp.float32),
        ],
        compiler_params=pltpu.CompilerParams(vmem_limit_bytes=VMEM_LIMIT),
        name="gla",
    )(z, z, z, z, z,
      pad_w(w_af, 0), b_af.reshape(pairs, 1, LANES),
      pad_w(w_ab, GLA_GATE_RANK), b_ab.reshape(pairs, 1, LANES),
      g_head.reshape(pairs, 1, 2 * LANES))


def _mix_out_body(x_ref, o_ref, p_ref, pp_ref, pn_ref, wp_ref, ps_ref, wo_ref, g_ref, gate_ref, out_ref, pe_ref):
    s = pl.program_id(1)
    n_s = pl.num_programs(1)
    seq_len = n_s * ROW_TILE
    pe_ref[pl.ds(0, POOL_HALO), :] = jnp.where(s > 0, pp_ref[0], 0.0)
    pe_ref[pl.ds(POOL_HALO, ROW_TILE), :] = p_ref[0]
    pe_ref[pl.ds(POOL_HALO + ROW_TILE, POOL_HALO), :] = jnp.where(s < n_s - 1, pn_ref[0], 0.0)
    pos = s * ROW_TILE + lax.broadcasted_iota(jnp.int32, (ROW_TILE, POOL_GROUP_WIDTH), 0)
    mix = jnp.dot(o_ref[0].astype(jnp.bfloat16), wo_ref[pl.ds(0, GLA_WIDTH), :],
                  preferred_element_type=jnp.float32)
    for gi, w in enumerate(POOL_WINDOWS):
        cols = pl.ds(gi * POOL_GROUP_WIDTH, POOL_GROUP_WIDTH)
        total = pe_ref[pl.ds(POOL_HALO - w // 2, ROW_TILE), cols]
        for d in range(-w // 2 + 1, w // 2):
            total = total + pe_ref[pl.ds(POOL_HALO + d, ROW_TILE), cols]
        count = (jnp.minimum(pos + w // 2, seq_len) - jnp.maximum(pos - w // 2, 0)).astype(jnp.float32)
        pooled = total / count - pe_ref[pl.ds(POOL_HALO, ROW_TILE), cols]
        y = jnp.dot(pooled.astype(jnp.bfloat16), wp_ref[gi], preferred_element_type=jnp.float32)
        y = y * ps_ref[:, gi * POOL_GROUP_WIDTH:(gi + 1) * POOL_GROUP_WIDTH]
        mix = mix + jnp.dot(y.astype(jnp.bfloat16),
                            wo_ref[pl.ds(GLA_WIDTH + gi * POOL_GROUP_WIDTH, POOL_GROUP_WIDTH), :],
                            preferred_element_type=jnp.float32)
    yn = mix * lax.rsqrt(jnp.mean(mix * mix, axis=-1, keepdims=True) + EPS) * g_ref[...]
    out_ref[0] = x_ref[0] + gate_ref[0] * yn


def _mix_out(x, o_gla, z, w_pool, pool_scale, w_out, g, gate):
    B, S, D = x.shape
    halo_per_tile = ROW_TILE // POOL_HALO
    last_halo = S // POOL_HALO - 1
    row_spec = pl.BlockSpec((1, ROW_TILE, D), lambda b, s: (b, s, 0))
    return pl.pallas_call(
        _mix_out_body,
        grid=(B, S // ROW_TILE),
        in_specs=[
            row_spec,
            pl.BlockSpec((1, ROW_TILE, GLA_WIDTH), lambda b, s: (b, s, 0)),
            pl.BlockSpec((1, ROW_TILE, POOL_WIDTH), lambda b, s: (b, s, Z_P // POOL_WIDTH)),
            pl.BlockSpec((1, POOL_HALO, POOL_WIDTH),
                         lambda b, s: (b, jnp.maximum(s * halo_per_tile - 1, 0), Z_P // POOL_WIDTH)),
            pl.BlockSpec((1, POOL_HALO, POOL_WIDTH),
                         lambda b, s: (b, jnp.minimum((s + 1) * halo_per_tile, last_halo), Z_P // POOL_WIDTH)),
            pl.BlockSpec(w_pool.shape, lambda b, s: (0, 0, 0)),
            pl.BlockSpec((1, POOL_WIDTH), lambda b, s: (0, 0)),
            pl.BlockSpec((D, D), lambda b, s: (0, 0)),
            pl.BlockSpec((1, D), lambda b, s: (0, 0)),
            pl.BlockSpec((1, 1, D), lambda b, s: (b, 0, 0)),
        ],
        out_specs=row_spec,
        out_shape=jax.ShapeDtypeStruct((B, S, D), jnp.float32),
        scratch_shapes=[pltpu.VMEM((ROW_TILE + 2 * POOL_HALO, POOL_WIDTH), jnp.float32)],
        compiler_params=pltpu.CompilerParams(vmem_limit_bytes=VMEM_LIMIT),
        name="mix_out",
    )(x, o_gla, z, z, z, w_pool.astype(jnp.bfloat16), pool_scale.reshape(1, POOL_WIDTH),
      w_out.astype(jnp.bfloat16), g.reshape(1, D), gate)


def _peer_ffn(x, g_pre, sc2, sh2, w_q, sub_keys, u_tab, v_tab):
    B, S, D = x.shape
    T = B * S
    h2, experts_t, gates_t = _peer_retrieve(x, g_pre, sc2, sh2, w_q, sub_keys)
    coef_dup = _peer_u_pass(experts_t, h2.reshape(T, D), gates_t, _pack_table(u_tab))
    y = _peer_v_pass(experts_t, coef_dup, _pack_table(v_tab))
    return y.reshape(B, S, D)


NEG_INF = float("-inf")
RETR_TOKENS = 256


def _top16_rows(s, row_id):
    vals, ids = [], []
    for _ in range(PEER_TOPK):
        m = jnp.max(s, axis=0, keepdims=True)
        r = jnp.min(jnp.where(s == m, row_id, jnp.iinfo(jnp.int32).max), axis=0, keepdims=True)
        vals.append(m)
        ids.append(r)
        s = jnp.where(row_id == r, NEG_INF, s)
    return vals, ids


def _peer_retrieve_body(x_ref, g_ref, sc_ref, sh_ref, wq_ref, keys_ref, h_ref, idx_ref, gate_ref, q_ref):
    x = x_ref[0]
    h = x * lax.rsqrt(jnp.mean(x * x, axis=-1, keepdims=True) + EPS) * g_ref[...]
    h = h * (1.0 + sc_ref[0]) + sh_ref[0]
    h_ref[0] = h
    q_ref[...] = jnp.dot(h.astype(jnp.bfloat16), wq_ref[...],
                         preferred_element_type=jnp.float32).astype(jnp.bfloat16)
    key_iota = lax.broadcasted_iota(jnp.int32, (PEER_NKEYS, RETR_TOKENS), 0)
    sub8 = lax.broadcasted_iota(jnp.int32, (8, RETR_TOKENS), 0)
    cand_id = jnp.concatenate([sub8, sub8 + 8] + [sub8 + a * PEER_TOPK for a in range(1, 8)]
                              + [(sub8 + 8) * PEER_TOPK], axis=0)
    chunks = ROW_TILE // RETR_TOKENS

    def head_chunk(i, carry):
        hd = lax.shift_right_logical(i, chunks.bit_length() - 1)
        ch = i & (chunks - 1)
        tok = pl.ds(pl.multiple_of(ch * RETR_TOKENS, RETR_TOKENS), RETR_TOKENS)
        vals, rows = [], []
        for p in range(2):
            col = pl.multiple_of((hd * 2 + p) * PEER_HALF, PEER_HALF)
            qs = q_ref[tok, pl.ds(col, PEER_HALF)]
            s = lax.dot_general(keys_ref[hd, p], qs, (((1,), (1,)), ((), ())),
                                preferred_element_type=jnp.float32)
            v, r = _top16_rows(s, key_iota)
            vals.append(v)
            rows.append(r)
        v1_hi = jnp.concatenate(vals[0][8:], axis=0)
        v2_lo = jnp.concatenate(vals[1][:8], axis=0)
        v2_hi = jnp.concatenate(vals[1][8:], axis=0)
        cand = jnp.concatenate([vals[0][0] + v2_lo, vals[0][0] + v2_hi]
                               + [vals[0][a] + v2_lo for a in range(1, 8)]
                               + [v1_hi + vals[1][0]], axis=0)
        top_s, pos = _top16_rows(cand, cand_id)
        pos = jnp.concatenate(pos, axis=0)
        pa = pos >> 4
        pb = pos & (PEER_TOPK - 1)
        i1 = jnp.zeros_like(pos)
        i2 = jnp.zeros_like(pos)
        for a in range(PEER_TOPK):
            i1 = jnp.where(pa == a, rows[0][a], i1)
            i2 = jnp.where(pb == a, rows[1][a], i2)
        ts = jnp.concatenate(top_s, axis=0)
        e = jnp.exp(ts - top_s[0])
        gates = e / jnp.sum(e, axis=0, keepdims=True)
        out_rows = pl.ds(pl.multiple_of(hd * PEER_TOPK, PEER_TOPK), PEER_TOPK)
        idx_ref[out_rows, tok] = (i1 * PEER_NKEYS + i2) * PEER_CHUNKS
        gate_ref[out_rows, tok] = gates
        return carry

    lax.fori_loop(0, PEER_HEADS * chunks, head_chunk, 0)


def _peer_retrieve(x, g_pre, sc2, sh2, w_q, sub_keys):
    B, S, D = x.shape
    T = B * S
    nq = w_q.shape[1]
    spb = S // ROW_TILE
    tok_spec = pl.BlockSpec((1, ROW_TILE, D), lambda b, s: (b, s, 0))
    mod_spec = pl.BlockSpec((1, 1, D), lambda b, s: (b, 0, 0))
    slot_spec = pl.BlockSpec((PEER_SLOTS, ROW_TILE), lambda b, s: (0, b * spb + s))
    return pl.pallas_call(
        _peer_retrieve_body,
        grid=(B, spb),
        in_specs=[
            tok_spec,
            pl.BlockSpec((1, D), lambda b, s: (0, 0)),
            mod_spec,
            mod_spec,
            pl.BlockSpec((D, nq), lambda b, s: (0, 0)),
            pl.BlockSpec(sub_keys.shape, lambda b, s: (0, 0, 0, 0)),
        ],
        out_specs=[tok_spec, slot_spec, slot_spec],
        out_shape=[
            jax.ShapeDtypeStruct((B, S, D), jnp.float32),
            jax.ShapeDtypeStruct((PEER_SLOTS, T), jnp.int32),
            jax.ShapeDtypeStruct((PEER_SLOTS, T), jnp.float32),
        ],
        scratch_shapes=[pltpu.VMEM((ROW_TILE, nq), jnp.bfloat16)],
        compiler_params=pltpu.CompilerParams(vmem_limit_bytes=VMEM_LIMIT),
        name="peer_retrieve",
    )(x, g_pre.reshape(1, D), sc2, sh2, w_q.astype(jnp.bfloat16), sub_keys.astype(jnp.bfloat16))


PEER_SLOTS = PEER_HEADS * PEER_TOPK
PEER_CHUNKS = D_MODEL // 256
PEER_TB = 256
TILE_STRIDE = PEER_SLOTS + 8


def _pack_table(tab):
    E, D = tab.shape
    t = tab.astype(jnp.bfloat16).reshape(E, 2, D // 256, LANES)
    t = jnp.transpose(t, (0, 2, 3, 1))
    return lax.bitcast_convert_type(t, jnp.uint32).reshape(E * (D // 256), LANES)


def _gather_token(idx_ref, tab_ref, tile_ref, t):
    for k in range(PEER_SLOTS):
        row = pl.multiple_of(idx_ref[k, t], PEER_CHUNKS)
        tile_ref[pl.ds(k, PEER_CHUNKS, stride=TILE_STRIDE), :] = tab_ref[pl.ds(row, PEER_CHUNKS), :]


def _tile_chunk_f32(tile_ref, c):
    w = tile_ref[pl.ds(c * TILE_STRIDE, PEER_SLOTS), :]
    return pltpu.bitcast(w, jnp.bfloat16).astype(jnp.float32)


def _token_pairs(idx_ref, tab_ref, tile_a, tile_b, consume):
    _gather_token(idx_ref, tab_ref, tile_a, 0)

    def pair(j, carry):
        t0 = 2 * j
        _gather_token(idx_ref, tab_ref, tile_b, t0 + 1)
        consume(tile_a, t0)
        _gather_token(idx_ref, tab_ref, tile_a, jnp.minimum(t0 + 2, PEER_TB - 1))
        consume(tile_b, t0 + 1)
        return carry
    lax.fori_loop(0, PEER_TB // 2, pair, 0)


def _token_block_cols(t):
    return pl.ds(pl.multiple_of(lax.shift_right_logical(t, 7) * LANES, LANES), LANES)


def _peer_u_body(idx_ref, h_ref, gate_ref, tab_ref, coef_ref, tile_a, tile_b, sblk_ref):
    lane = lax.broadcasted_iota(jnp.int32, (2 * PEER_SLOTS, LANES), 1)
    odd = (lax.broadcasted_iota(jnp.int32, (8, LANES), 0) & 1) == 1

    def consume(tile_ref, t):
        acc = jnp.zeros((2 * PEER_SLOTS // 8, 8, LANES), jnp.float32)
        for c in range(PEER_CHUNKS):
            h_lo = jnp.broadcast_to(h_ref[t, pl.ds(c, 1), :], (8, LANES))
            h_hi = jnp.broadcast_to(h_ref[t, pl.ds(PEER_CHUNKS + c, 1), :], (8, LANES))
            pattern = jnp.where(odd, h_hi, h_lo)
            x = _tile_chunk_f32(tile_ref, c).reshape(2 * PEER_SLOTS // 8, 8, LANES)
            acc = acc + x * pattern[None]
        s = jnp.sum(acc.reshape(2 * PEER_SLOTS, LANES), axis=1, keepdims=True)
        tl = t & (LANES - 1)
        sblk_ref[...] = jnp.where(lane == tl, s, sblk_ref[...])

        @pl.when(tl == LANES - 1)
        def _():
            even_rows = sblk_ref[pl.ds(0, PEER_SLOTS, stride=2), :]
            odd_rows = sblk_ref[pl.ds(1, PEER_SLOTS, stride=2), :]
            a = even_rows + odd_rows
            cols = _token_block_cols(t)
            coef = gate_ref[:, cols] * (0.5 * a * (1.0 + lax.erf(a * (2.0 ** -0.5))))
            sblk_ref[pl.ds(0, PEER_SLOTS, stride=2), :] = coef
            sblk_ref[pl.ds(1, PEER_SLOTS, stride=2), :] = coef
            coef_ref[:, cols] = sblk_ref[...]

    _token_pairs(idx_ref, tab_ref, tile_a, tile_b, consume)


def _peer_u_pass(experts_t, ht, gates_t, tab):
    T = ht.shape[0]
    return pl.pallas_call(
        _peer_u_body,
        grid=(T // PEER_TB,),
        in_specs=[
            pl.BlockSpec((PEER_SLOTS, PEER_TB), lambda i: (0, i), memory_space=pltpu.SMEM),
            pl.BlockSpec((PEER_TB, 8, LANES), lambda i: (i, 0, 0)),
            pl.BlockSpec((PEER_SLOTS, PEER_TB), lambda i: (0, i)),
            pl.BlockSpec(memory_space=pltpu.VMEM),
        ],
        out_specs=pl.BlockSpec((2 * PEER_SLOTS, PEER_TB), lambda i: (0, i)),
        out_shape=jax.ShapeDtypeStruct((2 * PEER_SLOTS, T), jnp.float32),
        scratch_shapes=[
            pltpu.VMEM((PEER_CHUNKS * TILE_STRIDE, LANES), jnp.uint32),
            pltpu.VMEM((PEER_CHUNKS * TILE_STRIDE, LANES), jnp.uint32),
            pltpu.VMEM((2 * PEER_SLOTS, LANES), jnp.float32),
        ],
        compiler_params=pltpu.CompilerParams(vmem_limit_bytes=VMEM_LIMIT),
        name="peer_u_pass",
    )(experts_t, ht.reshape(T, 8, LANES), gates_t, tab)


def _peer_v_body(idx_ref, coef_ref, tab_ref, y_ref, tile_a, tile_b):
    lane = lax.broadcasted_iota(jnp.int32, (2 * PEER_SLOTS, LANES), 1)

    def consume(tile_ref, t):
        tl = t & (LANES - 1)
        coef_blk = coef_ref[:, _token_block_cols(t)]
        ccol = jnp.sum(jnp.where(lane == tl, coef_blk, 0.0), axis=1, keepdims=True)
        ccol = ccol.reshape(2 * PEER_SLOTS // 8, 8, 1)
        for c in range(PEER_CHUNKS):
            x = _tile_chunk_f32(tile_ref, c).reshape(2 * PEER_SLOTS // 8, 8, LANES)
            part = jnp.sum(x * ccol, axis=0)
            part = part + pltpu.roll(part, 4, axis=0)
            part = part + pltpu.roll(part, 2, axis=0)
            y_ref[t, pl.ds(c, 1), :] = part[0:1]
            y_ref[t, pl.ds(PEER_CHUNKS + c, 1), :] = part[1:2]

    _token_pairs(idx_ref, tab_ref, tile_a, tile_b, consume)


def _peer_v_pass(experts_t, coef_dup, tab):
    T = experts_t.shape[1]
    y = pl.pallas_call(
        _peer_v_body,
        grid=(T // PEER_TB,),
        in_specs=[
            pl.BlockSpec((PEER_SLOTS, PEER_TB), lambda i: (0, i), memory_space=pltpu.SMEM),
            pl.BlockSpec((2 * PEER_SLOTS, PEER_TB), lambda i: (0, i)),
            pl.BlockSpec(memory_space=pltpu.VMEM),
        ],
        out_specs=pl.BlockSpec((PEER_TB, 8, LANES), lambda i: (i, 0, 0)),
        out_shape=jax.ShapeDtypeStruct((T, 8, LANES), jnp.float32),
        scratch_shapes=[pltpu.VMEM((PEER_CHUNKS * TILE_STRIDE, LANES), jnp.uint32)] * 2,
        compiler_params=pltpu.CompilerParams(vmem_limit_bytes=VMEM_LIMIT),
        name="peer_v_pass",
    )(experts_t, coef_dup, tab)
    return y.reshape(T, D_MODEL)


def _post_residual_body(x_ref, y_ref, g_ref, gate_ref, o_ref):
    y = y_ref[0]
    yn = y * lax.rsqrt(jnp.mean(y * y, axis=-1, keepdims=True) + EPS) * g_ref[...]
    o_ref[0] = x_ref[0] + gate_ref[0] * yn


def _post_residual(x, y, g, gate):
    B, S, D = x.shape
    row_spec = pl.BlockSpec((1, ROW_TILE, D), lambda b, s: (b, s, 0))
    return pl.pallas_call(
        _post_residual_body,
        grid=(B, S // ROW_TILE),
        in_specs=[
            row_spec,
            row_spec,
            pl.BlockSpec((1, D), lambda b, s: (0, 0)),
            pl.BlockSpec((1, 1, D), lambda b, s: (b, 0, 0)),
        ],
        out_specs=row_spec,
        out_shape=jax.ShapeDtypeStruct((B, S, D), x.dtype),
        name="post_residual",
    )(x, y, g.reshape(1, D), gate)


def kernel(x, c, w_mod, b_mod, g_pre_mix, g_post_mix, w_in, w_alpha_f, b_alpha_f, w_alpha_b, b_alpha_b,
           g_gla_head, w_pool, pool_scale, w_out, g_pre_ffn, g_post_ffn, w_peer_q, peer_sub_keys,
           peer_u, peer_v):
    for l in range(w_mod.shape[0]):
        mod = _modulation(c, w_mod[l], b_mod[l])
        sh1, sc1, gt1, sh2, sc2, gt2 = [m[:, None, :] for m in jnp.split(mod, N_MOD, axis=-1)]
        z = _in_proj(x, g_pre_mix[l], sc1, sh1, w_in[l])
        o_gla = _gla(z, w_alpha_f[l], b_alpha_f[l], w_alpha_b[l], b_alpha_b[l], g_gla_head[l])
        x = _mix_out(x, o_gla, z, w_pool[l], pool_scale[l], w_out[l], g_post_mix[l], gt1)
        y = _peer_ffn(x, g_pre_ffn[l], sc2, sh2, w_peer_q[l], peer_sub_keys[l], peer_u[l], peer_v[l])
        x = _post_residual(x, y, g_post_ffn[l], gt2)
    return x
```

```python
import jax
import jax.numpy as jnp
from jax import lax
from jax.experimental import pallas as pl
from jax.experimental.pallas import tpu as pltpu

D_MODEL = 1024
GLA_WIDTH = 512
POOL_WIDTH = 512
GLA_HEADS = 4
GLA_DV = 128
GLA_DK = 64
GLA_KEY_WIDTH = 256
GLA_GATE_RANK = 16
GLA_TAU = 16.0
GLA_CHUNK = 64
POOL_WINDOWS = (2, 4, 8, 16)
POOL_GROUP_WIDTH = 128
PEER_HEADS = 8
PEER_NKEYS = 128
PEER_HALF = 128
PEER_TOPK = 16
N_MOD = 6
EPS = 1e-6

LANES = 128
VMEM_LIMIT = 48 * 1024 * 1024
ROW_TILE = 512

Z_Q, Z_K, Z_V, Z_R, Z_P, Z_A = 0, 256, 512, 1024, 1536, 2048
Z_COLS = 2176
POOL_HALO = 8
GLA_UNROLL = 4


def _mod_body(c_ref, w_ref, b_ref, o_ref):
    c = c_ref[...]
    cond = c * (1.0 / (1.0 + jnp.exp(-c)))
    o_ref[...] = jnp.dot(cond.astype(jnp.bfloat16), w_ref[...].astype(jnp.bfloat16),
                         preferred_element_type=jnp.float32) + b_ref[...]


def _modulation(c, w_mod, b_mod):
    B, D = c.shape
    n = w_mod.shape[1]
    return pl.pallas_call(
        _mod_body,
        grid=(n // D,),
        in_specs=[
            pl.BlockSpec((B, D), lambda j: (0, 0)),
            pl.BlockSpec((D, D), lambda j: (0, j)),
            pl.BlockSpec((1, D), lambda j: (0, j)),
        ],
        out_specs=pl.BlockSpec((B, D), lambda j: (0, j)),
        out_shape=jax.ShapeDtypeStruct((B, n), jnp.float32),
        name="modulation",
    )(c, w_mod, b_mod.reshape(1, n))


def _in_proj_body(x_ref, g_ref, sc_ref, sh_ref, w_ref, z_ref):
    x = x_ref[0]
    h = x * lax.rsqrt(jnp.mean(x * x, axis=-1, keepdims=True) + EPS) * g_ref[...]
    h = h * (1.0 + sc_ref[0]) + sh_ref[0]
    z_ref[0] = jnp.dot(h.astype(jnp.bfloat16), w_ref[...], preferred_element_type=jnp.float32)


def _in_proj(x, g, sc, sh, w_in):
    B, S, D = x.shape
    w = jnp.concatenate([w_in[:, :Z_P], w_in[:, Z_P + 32:], w_in[:, Z_P:Z_P + 32],
                         jnp.zeros((D, Z_COLS - Z_A - 32), w_in.dtype)], axis=1).astype(jnp.bfloat16)
    mod_spec = pl.BlockSpec((1, 1, D), lambda b, s: (b, 0, 0))
    return pl.pallas_call(
        _in_proj_body,
        grid=(B, S // ROW_TILE),
        in_specs=[
            pl.BlockSpec((1, ROW_TILE, D), lambda b, s: (b, s, 0)),
            pl.BlockSpec((1, D), lambda b, s: (0, 0)),
            mod_spec,
            mod_spec,
            pl.BlockSpec((D, Z_COLS), lambda b, s: (0, 0)),
        ],
        out_specs=pl.BlockSpec((1, ROW_TILE, Z_COLS), lambda b, s: (b, s, 0)),
        out_shape=jax.ShapeDtypeStruct((B, S, Z_COLS), jnp.float32),
        compiler_params=pltpu.CompilerParams(vmem_limit_bytes=VMEM_LIMIT),
        name="in_proj",
    )(x, g.reshape(1, D), sc, sh, w)


def _log_sigmoid(x):
    return jnp.minimum(x, 0.0) - jnp.log1p(jnp.exp(-jnp.abs(x)))


def _bf16_dot(a, b, dims):
    return lax.dot_general(a.astype(jnp.bfloat16), b.astype(jnp.bfloat16), (dims, ((), ())),
                           preferred_element_type=jnp.float32)


def _gla_body(q_ref, k_ref, v_ref, r_ref, a_ref, wf_ref, bf_ref, wb_ref, bb_ref, gh_ref, o_ref,
              of_ref, st_ref):
    C = GLA_CHUNK
    n_chunks = q_ref.shape[1] // C
    row = lax.broadcasted_iota(jnp.int32, (C, C), 0)
    col = lax.broadcasted_iota(jnp.int32, (C, C), 1)
    lane = lax.broadcasted_iota(jnp.int32, (1, LANES), 1)
    head_lanes = [lane < GLA_DK, lane >= GLA_DK]

    def chunk(rows, w_ref, b_ref, cum_mask, keep_mask, total_row, emit):
        a = a_ref[0, rows, :]
        log_a = _log_sigmoid(_bf16_dot(a, w_ref[0], ((1,), (0,))) + b_ref[0]) * (1.0 / GLA_TAU)
        b = jnp.dot(cum_mask, log_a, precision=lax.Precision.HIGHEST,
                    preferred_element_type=jnp.float32)
        b_tot = b[total_row:total_row + 1, :]
        q = q_ref[0, rows, :] * (GLA_DK ** -0.5)
        k = k_ref[0, rows, :]
        q_dec = q * jnp.exp(b)
        k_inv = k * jnp.exp(-b)
        k_end = k * jnp.exp(b_tot - b)
        gamma = jnp.exp(b_tot)
        for hd in range(2):
            qm = jnp.where(head_lanes[hd], q_dec, 0.0)
            v = v_ref[0, rows, hd * GLA_DV:(hd + 1) * GLA_DV]
            scores = jnp.where(keep_mask, _bf16_dot(qm, k_inv, ((1,), (1,))), 0.0)
            state = st_ref[hd]
            o = _bf16_dot(scores, v, ((1,), (0,))) + _bf16_dot(qm, state, ((1,), (1,)))
            st_ref[hd] = state * gamma + _bf16_dot(v, k_end, ((0,), (0,)))
            emit(hd, o)

    def chunk_rows(n):
        return pl.ds(pl.multiple_of(n * C, C), C)

    st_ref[...] = jnp.zeros_like(st_ref)
    prefix = (row >= col).astype(jnp.float32)

    def fwd(i, carry):
        for u in range(GLA_UNROLL):
            rows = chunk_rows(i * GLA_UNROLL + u)

            def emit(hd, o, rows=rows):
                of_ref[rows, hd * GLA_DV:(hd + 1) * GLA_DV] = o
            chunk(rows, wf_ref, bf_ref, prefix, row >= col, C - 1, emit)
        return carry
    lax.fori_loop(0, n_chunks // GLA_UNROLL, fwd, 0)

    st_ref[...] = jnp.zeros_like(st_ref)
    suffix = (row <= col).astype(jnp.float32)

    def bwd(i, carry):
        for u in range(GLA_UNROLL):
            rows = chunk_rows(n_chunks - 1 - (i * GLA_UNROLL + u))

            def emit(hd, o, rows=rows):
                cols = slice(hd * GLA_DV, (hd + 1) * GLA_DV)
                tot = of_ref[rows, cols] + o
                y = tot * lax.rsqrt(jnp.mean(tot * tot, axis=-1, keepdims=True) + EPS) * gh_ref[0, :, cols]
                r = r_ref[0, rows, cols]
                o_ref[0, rows, cols] = y * (r * (1.0 / (1.0 + jnp.exp(-r))))
            chunk(rows, wb_ref, bb_ref, suffix, row < col, 0, emit)
        return carry
    lax.fori_loop(0, n_chunks // GLA_UNROLL, bwd, 0)


def _gla(z, w_af, b_af, w_ab, b_ab, g_head):
    B, S, _ = z.shape
    pairs = GLA_HEADS // 2

    def pad_w(w, first_row):
        wp = jnp.zeros((pairs, LANES, LANES), jnp.float32)
        return wp.at[:, first_row:first_row + GLA_GATE_RANK, :].set(
            jnp.transpose(w.reshape(GLA_GATE_RANK, pairs, LANES), (1, 0, 2)))

    def lane_blk(off):
        return pl.BlockSpec((1, S, LANES), lambda b, j: (b, 0, off // LANES + j))

    def wide_blk(off):
        return pl.BlockSpec((1, S, 2 * LANES), lambda b, j: (b, 0, off // (2 * LANES) + j))

    w_spec = pl.BlockSpec((1, LANES, LANES), lambda b, j: (j, 0, 0))
    b_spec = pl.BlockSpec((1, 1, LANES), lambda b, j: (j, 0, 0))
    return pl.pallas_call(
        _gla_body,
        grid=(B, pairs),
        in_specs=[
            lane_blk(Z_Q), lane_blk(Z_K), wide_blk(Z_V), wide_blk(Z_R),
            pl.BlockSpec((1, S, LANES), lambda b, j: (b, 0, Z_A // LANES)),
            w_spec, b_spec, w_spec, b_spec,
            pl.BlockSpec((1, 1, 2 * LANES), lambda b, j: (j, 0, 0)),
        ],
        out_specs=pl.BlockSpec((1, S, 2 * LANES), lambda b, j: (b, 0, j)),
        out_shape=jax.ShapeDtypeStruct((B, S, GLA_WIDTH), jnp.float32),
        scratch_shapes=[
            pltpu.VMEM((S, 2 * LANES), jnp.float32),
            pltpu.VMEM((2, GLA_DV, LANES), jnp.float32),
        ],
        compiler_params=pltpu.CompilerParams(vmem_limit_bytes=VMEM_LIMIT),
        name="gla",
    )(z, z, z, z, z,
      pad_w(w_af, 0), b_af.reshape(pairs, 1, LANES),
      pad_w(w_ab, GLA_GATE_RANK), b_ab.reshape(pairs, 1, LANES),
      g_head.reshape(pairs, 1, 2 * LANES))


def _mix_out_body(x_ref, o_ref, p_ref, pp_ref, pn_ref, wp_ref, ps_ref, wo_ref, g_ref, gate_ref, out_ref, pe_ref):
    s = pl.program_id(1)
    n_s = pl.num_programs(1)
    seq_len = n_s * ROW_TILE
    pe_ref[pl.ds(0, POOL_HALO), :] = jnp.where(s > 0, pp_ref[0], 0.0)
    pe_ref[pl.ds(POOL_HALO, ROW_TILE), :] = p_ref[0]
    pe_ref[pl.ds(POOL_HALO + ROW_TILE, POOL_HALO), :] = jnp.where(s < n_s - 1, pn_ref[0], 0.0)
    pos = s * ROW_TILE + lax.broadcasted_iota(jnp.int32, (ROW_TILE, POOL_GROUP_WIDTH), 0)
    mix = jnp.dot(o_ref[0].astype(jnp.bfloat16), wo_ref[pl.ds(0, GLA_WIDTH), :],
                  preferred_element_type=jnp.float32)
    for gi, w in enumerate(POOL_WINDOWS):
        cols = pl.ds(gi * POOL_GROUP_WIDTH, POOL_GROUP_WIDTH)
        total = pe_ref[pl.ds(POOL_HALO - w // 2, ROW_TILE), cols]
        for d in range(-w // 2 + 1, w // 2):
            total = total + pe_ref[pl.ds(POOL_HALO + d, ROW_TILE), cols]
        count = (jnp.minimum(pos + w // 2, seq_len) - jnp.maximum(pos - w // 2, 0)).astype(jnp.float32)
        pooled = total / count - pe_ref[pl.ds(POOL_HALO, ROW_TILE), cols]
        y = jnp.dot(pooled.astype(jnp.bfloat16), wp_ref[gi], preferred_element_type=jnp.float32)
        y = y * ps_ref[:, gi * POOL_GROUP_WIDTH:(gi + 1) * POOL_GROUP_WIDTH]
        mix = mix + jnp.dot(y.astype(jnp.bfloat16),
                            wo_ref[pl.ds(GLA_WIDTH + gi * POOL_GROUP_WIDTH, POOL_GROUP_WIDTH), :],
                            preferred_element_type=jnp.float32)
    yn = mix * lax.rsqrt(jnp.mean(mix * mix, axis=-1, keepdims=True) + EPS) * g_ref[...]
    out_ref[0] = x_ref[0] + gate_ref[0] * yn


def _mix_out(x, o_gla, z, w_pool, pool_scale, w_out, g, gate):
    B, S, D = x.shape
    halo_per_tile = ROW_TILE // POOL_HALO
    last_halo = S // POOL_HALO - 1
    row_spec = pl.BlockSpec((1, ROW_TILE, D), lambda b, s: (b, s, 0))
    return pl.pallas_call(
        _mix_out_body,
        grid=(B, S // ROW_TILE),
        in_specs=[
            row_spec,
            pl.BlockSpec((1, ROW_TILE, GLA_WIDTH), lambda b, s: (b, s, 0)),
            pl.BlockSpec((1, ROW_TILE, POOL_WIDTH), lambda b, s: (b, s, Z_P // POOL_WIDTH)),
            pl.BlockSpec((1, POOL_HALO, POOL_WIDTH),
                         lambda b, s: (b, jnp.maximum(s * halo_per_tile - 1, 0), Z_P // POOL_WIDTH)),
            pl.BlockSpec((1, POOL_HALO, POOL_WIDTH),
                         lambda b, s: (b, jnp.minimum((s + 1) * halo_per_tile, last_halo), Z_P // POOL_WIDTH)),
            pl.BlockSpec(w_pool.shape, lambda b, s: (0, 0, 0)),
            pl.BlockSpec((1, POOL_WIDTH), lambda b, s: (0, 0)),
            pl.BlockSpec((D, D), lambda b, s: (0, 0)),
            pl.BlockSpec((1, D), lambda b, s: (0, 0)),
            pl.BlockSpec((1, 1, D), lambda b, s: (b, 0, 0)),
        ],
        out_specs=row_spec,
        out_shape=jax.ShapeDtypeStruct((B, S, D), jnp.float32),
        scratch_shapes=[pltpu.VMEM((ROW_TILE + 2 * POOL_HALO, POOL_WIDTH), jnp.float32)],
        compiler_params=pltpu.CompilerParams(vmem_limit_bytes=VMEM_LIMIT),
        name="mix_out",
    )(x, o_gla, z, z, z, w_pool.astype(jnp.bfloat16), pool_scale.reshape(1, POOL_WIDTH),
      w_out.astype(jnp.bfloat16), g.reshape(1, D), gate)


def _peer_ffn(x, g_pre, sc2, sh2, w_q, sub_keys, u_tab, v_tab):
    B, S, D = x.shape
    T = B * S
    h2, experts_t, gates_t = _peer_retrieve(x, g_pre, sc2, sh2, w_q, sub_keys)
    coef_dup = _peer_u_pass(experts_t, h2.reshape(T, D), gates_t, _pack_table(u_tab))
    y = _peer_v_pass(experts_t, coef_dup, _pack_table(v_tab))
    return y.reshape(B, S, D)


NEG_INF = float("-inf")
RETR_TOKENS = 256


def _top16_rows(s, row_id):
    vals, ids = [], []
    for _ in range(PEER_TOPK):
        m = jnp.max(s, axis=0, keepdims=True)
        r = jnp.min(jnp.where(s == m, row_id, jnp.iinfo(jnp.int32).max), axis=0, keepdims=True)
        vals.append(m)
        ids.append(r)
        s = jnp.where(row_id == r, NEG_INF, s)
    return vals, ids


def _peer_retrieve_body(x_ref, g_ref, sc_ref, sh_ref, wq_ref, keys_ref, h_ref, idx_ref, gate_ref, q_ref):
    x = x_ref[0]
    h = x * lax.rsqrt(jnp.mean(x * x, axis=-1, keepdims=True) + EPS) * g_ref[...]
    h = h * (1.0 + sc_ref[0]) + sh_ref[0]
    h_ref[0] = h
    q_ref[...] = jnp.dot(h.astype(jnp.bfloat16), wq_ref[...],
                         preferred_element_type=jnp.float32).astype(jnp.bfloat16)
    key_iota = lax.broadcasted_iota(jnp.int32, (PEER_NKEYS, RETR_TOKENS), 0)
    sub8 = lax.broadcasted_iota(jnp.int32, (8, RETR_TOKENS), 0)
    cand_id = jnp.concatenate([sub8, sub8 + 8] + [sub8 + a * PEER_TOPK for a in range(1, 8)]
                              + [(sub8 + 8) * PEER_TOPK], axis=0)
    chunks = ROW_TILE // RETR_TOKENS

    def head_chunk(i, carry):
        hd = lax.shift_right_logical(i, chunks.bit_length() - 1)
        ch = i & (chunks - 1)
        tok = pl.ds(pl.multiple_of(ch * RETR_TOKENS, RETR_TOKENS), RETR_TOKENS)
        vals, rows = [], []
        for p in range(2):
            col = pl.multiple_of((hd * 2 + p) * PEER_HALF, PEER_HALF)
            qs = q_ref[tok, pl.ds(col, PEER_HALF)]
            s = lax.dot_general(keys_ref[hd, p], qs, (((1,), (1,)), ((), ())),
                                preferred_element_type=jnp.float32)
            v, r = _top16_rows(s, key_iota)
            vals.append(v)
            rows.append(r)
        v1_hi = jnp.concatenate(vals[0][8:], axis=0)
        v2_lo = jnp.concatenate(vals[1][:8], axis=0)
        v2_hi = jnp.concatenate(vals[1][8:], axis=0)
        cand = jnp.concatenate([vals[0][0] + v2_lo, vals[0][0] + v2_hi]
                               + [vals[0][a] + v2_lo for a in range(1, 8)]
                               + [v1_hi + vals[1][0]], axis=0)
        top_s, pos = _top16_rows(cand, cand_id)
        pos = jnp.concatenate(pos, axis=0)
        pa = pos >> 4
        pb = pos & (PEER_TOPK - 1)
        i1 = jnp.zeros_like(pos)
        i2 = jnp.zeros_like(pos)
        for a in range(PEER_TOPK):
            i1 = jnp.where(pa == a, rows[0][a], i1)
            i2 = jnp.where(pb == a, rows[1][a], i2)
        ts = jnp.concatenate(top_s, axis=0)
        e = jnp.exp(ts - top_s[0])
        gates = e / jnp.sum(e, axis=0, keepdims=True)
        out_rows = pl.ds(pl.multiple_of(hd * PEER_TOPK, PEER_TOPK), PEER_TOPK)
        idx_ref[out_rows, tok] = (i1 * PEER_NKEYS + i2) * PEER_CHUNKS
        gate_ref[out_rows, tok] = gates
        return carry

    lax.fori_loop(0, PEER_HEADS * chunks, head_chunk, 0)


def _peer_retrieve(x, g_pre, sc2, sh2, w_q, sub_keys):
    B, S, D = x.shape
    T = B * S
    nq = w_q.shape[1]
    spb = S // ROW_TILE
    tok_spec = pl.BlockSpec((1, ROW_TILE, D), lambda b, s: (b, s, 0))
    mod_spec = pl.BlockSpec((1, 1, D), lambda b, s: (b, 0, 0))
    slot_spec = pl.BlockSpec((PEER_SLOTS, ROW_TILE), lambda b, s: (0, b * spb + s))
    return pl.pallas_call(
        _peer_retrieve_body,
        grid=(B, spb),
        in_specs=[
            tok_spec,
            pl.BlockSpec((1, D), lambda b, s: (0, 0)),
            mod_spec,
            mod_spec,
            pl.BlockSpec((D, nq), lambda b, s: (0, 0)),
            pl.BlockSpec(sub_keys.shape, lambda b, s: (0, 0, 0, 0)),
        ],
        out_specs=[tok_spec, slot_spec, slot_spec],
        out_shape=[
            jax.ShapeDtypeStruct((B, S, D), jnp.float32),
            jax.ShapeDtypeStruct((PEER_SLOTS, T), jnp.int32),
            jax.ShapeDtypeStruct((PEER_SLOTS, T), jnp.float32),
        ],
        scratch_shapes=[pltpu.VMEM((ROW_TILE, nq), jnp.bfloat16)],
        compiler_params=pltpu.CompilerParams(vmem_limit_bytes=VMEM_LIMIT),
        name="peer_retrieve",
    )(x, g_pre.reshape(1, D), sc2, sh2, w_q.astype(jnp.bfloat16), sub_keys.astype(jnp.bfloat16))


PEER_SLOTS = PEER_HEADS * PEER_TOPK
PEER_CHUNKS = D_MODEL // 256
PEER_TB = 256
TILE_STRIDE = PEER_SLOTS + 8


PACK_ROWS = 256


def _bf16_bits(x):
    return pltpu.bitcast(x.astype(jnp.bfloat16).astype(jnp.float32), jnp.uint32)


def _pack_table_body(x_ref, o_ref):
    half = x_ref.shape[1] // 2
    for c in range(PEER_CHUNKS):
        lo = _bf16_bits(x_ref[:, c * LANES:(c + 1) * LANES])
        hi = _bf16_bits(x_ref[:, half + c * LANES:half + (c + 1) * LANES])
        o_ref[pl.ds(c, PACK_ROWS, stride=PEER_CHUNKS), :] = lax.shift_right_logical(lo, jnp.uint32(16)) | hi


def _pack_table(tab):
    E, D = tab.shape
    return pl.pallas_call(
        _pack_table_body,
        grid=(E // PACK_ROWS,),
        in_specs=[pl.BlockSpec((PACK_ROWS, D), lambda i: (i, 0))],
        out_specs=pl.BlockSpec((PACK_ROWS * PEER_CHUNKS, LANES), lambda i: (i, 0)),
        out_shape=jax.ShapeDtypeStruct((E * PEER_CHUNKS, LANES), jnp.uint32),
        name="pack_table",
    )(tab)


def _gather_token(idx_ref, tab_ref, tile_ref, t):
    for k in range(PEER_SLOTS):
        row = pl.multiple_of(idx_ref[k, t], PEER_CHUNKS)
        tile_ref[pl.ds(k, PEER_CHUNKS, stride=TILE_STRIDE), :] = tab_ref[pl.ds(row, PEER_CHUNKS), :]


def _tile_chunk_f32(tile_ref, c):
    w = tile_ref[pl.ds(c * TILE_STRIDE, PEER_SLOTS), :]
    return pltpu.bitcast(w, jnp.bfloat16).astype(jnp.float32)


def _token_pairs(idx_ref, tab_ref, tile_a, tile_b, consume):
    _gather_token(idx_ref, tab_ref, tile_a, 0)

    def pair(j, carry):
        t0 = 2 * j
        _gather_token(idx_ref, tab_ref, tile_b, t0 + 1)
        consume(tile_a, t0)
        _gather_token(idx_ref, tab_ref, tile_a, jnp.minimum(t0 + 2, PEER_TB - 1))
        consume(tile_b, t0 + 1)
        return carry
    lax.fori_loop(0, PEER_TB // 2, pair, 0)


def _token_block_cols(t):
    return pl.ds(pl.multiple_of(lax.shift_right_logical(t, 7) * LANES, LANES), LANES)


def _peer_u_body(idx_ref, h_ref, gate_ref, tab_ref, coef_ref, tile_a, tile_b, sblk_ref):
    lane = lax.broadcasted_iota(jnp.int32, (2 * PEER_SLOTS, LANES), 1)
    odd = (lax.broadcasted_iota(jnp.int32, (8, LANES), 0) & 1) == 1

    def consume(tile_ref, t):
        acc = jnp.zeros((2 * PEER_SLOTS // 8, 8, LANES), jnp.float32)
        for c in range(PEER_CHUNKS):
            h_lo = jnp.broadcast_to(h_ref[t, pl.ds(c, 1), :], (8, LANES))
            h_hi = jnp.broadcast_to(h_ref[t, pl.ds(PEER_CHUNKS + c, 1), :], (8, LANES))
            pattern = jnp.where(odd, h_hi, h_lo)
            x = _tile_chunk_f32(tile_ref, c).reshape(2 * PEER_SLOTS // 8, 8, LANES)
            acc = acc + x * pattern[None]
        s = jnp.sum(acc.reshape(2 * PEER_SLOTS, LANES), axis=1, keepdims=True)
        tl = t & (LANES - 1)
        sblk_ref[...] = jnp.where(lane == tl, s, sblk_ref[...])

        @pl.when(tl == LANES - 1)
        def _():
            even_rows = sblk_ref[pl.ds(0, PEER_SLOTS, stride=2), :]
            odd_rows = sblk_ref[pl.ds(1, PEER_SLOTS, stride=2), :]
            a = even_rows + odd_rows
            cols = _token_block_cols(t)
            coef = gate_ref[:, cols] * (0.5 * a * (1.0 + lax.erf(a * (2.0 ** -0.5))))
            sblk_ref[pl.ds(0, PEER_SLOTS, stride=2), :] = coef
            sblk_ref[pl.ds(1, PEER_SLOTS, stride=2), :] = coef
            coef_ref[:, cols] = sblk_ref[...]

    _token_pairs(idx_ref, tab_ref, tile_a, tile_b, consume)


def _peer_u_pass(experts_t, ht, gates_t, tab):
    T = ht.shape[0]
    return pl.pallas_call(
        _peer_u_body,
        grid=(T // PEER_TB,),
        in_specs=[
            pl.BlockSpec((PEER_SLOTS, PEER_TB), lambda i: (0, i), memory_space=pltpu.SMEM),
            pl.BlockSpec((PEER_TB, 8, LANES), lambda i: (i, 0, 0)),
            pl.BlockSpec((PEER_SLOTS, PEER_TB), lambda i: (0, i)),
            pl.BlockSpec(memory_space=pltpu.VMEM),
        ],
        out_specs=pl.BlockSpec((2 * PEER_SLOTS, PEER_TB), lambda i: (0, i)),
        out_shape=jax.ShapeDtypeStruct((2 * PEER_SLOTS, T), jnp.float32),
        scratch_shapes=[
            pltpu.VMEM((PEER_CHUNKS * TILE_STRIDE, LANES), jnp.uint32),
            pltpu.VMEM((PEER_CHUNKS * TILE_STRIDE, LANES), jnp.uint32),
            pltpu.VMEM((2 * PEER_SLOTS, LANES), jnp.float32),
        ],
        compiler_params=pltpu.CompilerParams(vmem_limit_bytes=VMEM_LIMIT),
        name="peer_u_pass",
    )(experts_t, ht.reshape(T, 8, LANES), gates_t, tab)


def _peer_v_body(idx_ref, coef_ref, tab_ref, y_ref, tile_a, tile_b):
    lane = lax.broadcasted_iota(jnp.int32, (2 * PEER_SLOTS, LANES), 1)

    def consume(tile_ref, t):
        tl = t & (LANES - 1)
        coef_blk = coef_ref[:, _token_block_cols(t)]
        ccol = jnp.sum(jnp.where(lane == tl, coef_blk, 0.0), axis=1, keepdims=True)
        ccol = ccol.reshape(2 * PEER_SLOTS // 8, 8, 1)
        for c in range(PEER_CHUNKS):
            x = _tile_chunk_f32(tile_ref, c).reshape(2 * PEER_SLOTS // 8, 8, LANES)
            part = jnp.sum(x * ccol, axis=0)
            part = part + pltpu.roll(part, 4, axis=0)
            part = part + pltpu.roll(part, 2, axis=0)
            y_ref[t, pl.ds(c, 1), :] = part[0:1]
            y_ref[t, pl.ds(PEER_CHUNKS + c, 1), :] = part[1:2]

    _token_pairs(idx_ref, tab_ref, tile_a, tile_b, consume)


def _peer_v_pass(experts_t, coef_dup, tab):
    T = experts_t.shape[1]
    y = pl.pallas_call(
        _peer_v_body,
        grid=(T // PEER_TB,),
        in_specs=[
            pl.BlockSpec((PEER_SLOTS, PEER_TB), lambda i: (0, i), memory_space=pltpu.SMEM),
            pl.BlockSpec((2 * PEER_SLOTS, PEER_TB), lambda i: (0, i)),
            pl.BlockSpec(memory_space=pltpu.VMEM),
        ],
        out_specs=pl.BlockSpec((PEER_TB, 8, LANES), lambda i: (i, 0, 0)),
        out_shape=jax.ShapeDtypeStruct((T, 8, LANES), jnp.float32),
        scratch_shapes=[pltpu.VMEM((PEER_CHUNKS * TILE_STRIDE, LANES), jnp.uint32)] * 2,
        compiler_params=pltpu.CompilerParams(vmem_limit_bytes=VMEM_LIMIT),
        name="peer_v_pass",
    )(experts_t, coef_dup, tab)
    return y.reshape(T, D_MODEL)


def _post_residual_body(x_ref, y_ref, g_ref, gate_ref, o_ref):
    y = y_ref[0]
    yn = y * lax.rsqrt(jnp.mean(y * y, axis=-1, keepdims=True) + EPS) * g_ref[...]
    o_ref[0] = x_ref[0] + gate_ref[0] * yn


def _post_residual(x, y, g, gate):
    B, S, D = x.shape
    row_spec = pl.BlockSpec((1, ROW_TILE, D), lambda b, s: (b, s, 0))
    return pl.pallas_call(
        _post_residual_body,
        grid=(B, S // ROW_TILE),
        in_specs=[
            row_spec,
            row_spec,
            pl.BlockSpec((1, D), lambda b, s: (0, 0)),
            pl.BlockSpec((1, 1, D), lambda b, s: (b, 0, 0)),
        ],
        out_specs=row_spec,
        out_shape=jax.ShapeDtypeStruct((B, S, D), x.dtype),
        name="post_residual",
    )(x, y, g.reshape(1, D), gate)


def kernel(x, c, w_mod, b_mod, g_pre_mix, g_post_mix, w_in, w_alpha_f, b_alpha_f, w_alpha_b, b_alpha_b,
           g_gla_head, w_pool, pool_scale, w_out, g_pre_ffn, g_post_ffn, w_peer_q, peer_sub_keys,
           peer_u, peer_v):
    for l in range(w_mod.shape[0]):
        mod = _modulation(c, w_mod[l], b_mod[l])
        sh1, sc1, gt1, sh2, sc2, gt2 = [m[:, None, :] for m in jnp.split(mod, N_MOD, axis=-1)]
        z = _in_proj(x, g_pre_mix[l], sc1, sh1, w_in[l])
        o_gla = _gla(z, w_alpha_f[l], b_alpha_f[l], w_alpha_b[l], b_alpha_b[l], g_gla_head[l])
        x = _mix_out(x, o_gla, z, w_pool[l], pool_scale[l], w_out[l], g_post_mix[l], gt1)
        y = _peer_ffn(x, g_pre_ffn[l], sc2, sh2, w_peer_q[l], peer_sub_keys[l], peer_u[l], peer_v[l])
        x = _post_residual(x, y, g_post_ffn[l], gt2)
    return x
```

```python
import jax
import jax.numpy as jnp
from jax import lax
from jax.experimental import pallas as pl
from jax.experimental.pallas import tpu as pltpu

D_MODEL = 1024
GLA_WIDTH = 512
POOL_WIDTH = 512
GLA_HEADS = 4
GLA_DV = 128
GLA_DK = 64
GLA_KEY_WIDTH = 256
GLA_GATE_RANK = 16
GLA_TAU = 16.0
GLA_CHUNK = 64
POOL_WINDOWS = (2, 4, 8, 16)
POOL_GROUP_WIDTH = 128
PEER_HEADS = 8
PEER_NKEYS = 128
PEER_HALF = 128
PEER_TOPK = 16
N_MOD = 6
EPS = 1e-6

LANES = 128
VMEM_LIMIT = 48 * 1024 * 1024
ROW_TILE = 512

Z_Q, Z_K, Z_V, Z_R, Z_P, Z_A = 0, 256, 512, 1024, 1536, 2048
Z_COLS = 2176
POOL_HALO = 8
GLA_UNROLL = 4


def _mod_body(c_ref, w_ref, b_ref, o_ref):
    c = c_ref[...]
    cond = c * (1.0 / (1.0 + jnp.exp(-c)))
    o_ref[...] = jnp.dot(cond.astype(jnp.bfloat16), w_ref[...].astype(jnp.bfloat16),
                         preferred_element_type=jnp.float32) + b_ref[...]


def _modulation(c, w_mod, b_mod):
    B, D = c.shape
    n = w_mod.shape[1]
    return pl.pallas_call(
        _mod_body,
        grid=(n // D,),
        in_specs=[
            pl.BlockSpec((B, D), lambda j: (0, 0)),
            pl.BlockSpec((D, D), lambda j: (0, j)),
            pl.BlockSpec((1, D), lambda j: (0, j)),
        ],
        out_specs=pl.BlockSpec((B, D), lambda j: (0, j)),
        out_shape=jax.ShapeDtypeStruct((B, n), jnp.float32),
        name="modulation",
    )(c, w_mod, b_mod.reshape(1, n))


def _in_proj_body(x_ref, g_ref, sc_ref, sh_ref, w_ref, z_ref):
    x = x_ref[0]
    h = x * lax.rsqrt(jnp.mean(x * x, axis=-1, keepdims=True) + EPS) * g_ref[...]
    h = h * (1.0 + sc_ref[0]) + sh_ref[0]
    z_ref[0] = jnp.dot(h.astype(jnp.bfloat16), w_ref[...], preferred_element_type=jnp.float32)


def _in_proj(x, g, sc, sh, w_in):
    B, S, D = x.shape
    w = jnp.concatenate([w_in[:, :Z_P], w_in[:, Z_P + 32:], w_in[:, Z_P:Z_P + 32],
                         jnp.zeros((D, Z_COLS - Z_A - 32), w_in.dtype)], axis=1).astype(jnp.bfloat16)
    mod_spec = pl.BlockSpec((1, 1, D), lambda b, s: (b, 0, 0))
    return pl.pallas_call(
        _in_proj_body,
        grid=(B, S // ROW_TILE),
        in_specs=[
            pl.BlockSpec((1, ROW_TILE, D), lambda b, s: (b, s, 0)),
            pl.BlockSpec((1, D), lambda b, s: (0, 0)),
            mod_spec,
            mod_spec,
            pl.BlockSpec((D, Z_COLS), lambda b, s: (0, 0)),
        ],
        out_specs=pl.BlockSpec((1, ROW_TILE, Z_COLS), lambda b, s: (b, s, 0)),
        out_shape=jax.ShapeDtypeStruct((B, S, Z_COLS), jnp.float32),
        compiler_params=pltpu.CompilerParams(vmem_limit_bytes=VMEM_LIMIT),
        name="in_proj",
    )(x, g.reshape(1, D), sc, sh, w)


def _log_sigmoid(x):
    return jnp.minimum(x, 0.0) - jnp.log1p(jnp.exp(-jnp.abs(x)))


def _bf16_dot(a, b, dims):
    return lax.dot_general(a.astype(jnp.bfloat16), b.astype(jnp.bfloat16), (dims, ((), ())),
                           preferred_element_type=jnp.float32)


def _gla_body(q_ref, k_ref, v_ref, r_ref, a_ref, wf_ref, bf_ref, wb_ref, bb_ref, gh_ref, o_ref,
              of_ref, st_ref):
    C = GLA_CHUNK
    n_chunks = q_ref.shape[1] // C
    row = lax.broadcasted_iota(jnp.int32, (C, C), 0)
    col = lax.broadcasted_iota(jnp.int32, (C, C), 1)
    lane = lax.broadcasted_iota(jnp.int32, (1, LANES), 1)
    head_lanes = [lane < GLA_DK, lane >= GLA_DK]

    def chunk(rows, w_ref, b_ref, cum_mask, keep_mask, total_row, emit):
        a = a_ref[0, rows, :]
        log_a = _log_sigmoid(_bf16_dot(a, w_ref[0], ((1,), (0,))) + b_ref[0]) * (1.0 / GLA_TAU)
        b = jnp.dot(cum_mask, log_a, precision=lax.Precision.HIGHEST,
                    preferred_element_type=jnp.float32)
        b_tot = b[total_row:total_row + 1, :]
        q = q_ref[0, rows, :] * (GLA_DK ** -0.5)
        k = k_ref[0, rows, :]
        q_dec = q * jnp.exp(b)
        k_inv = k * jnp.exp(-b)
        k_end = k * jnp.exp(b_tot - b)
        gamma = jnp.exp(b_tot)
        for hd in range(2):
            qm = jnp.where(head_lanes[hd], q_dec, 0.0)
            v = v_ref[0, rows, hd * GLA_DV:(hd + 1) * GLA_DV]
            scores = jnp.where(keep_mask, _bf16_dot(qm, k_inv, ((1,), (1,))), 0.0)
            state = st_ref[hd]
            o = _bf16_dot(scores, v, ((1,), (0,))) + _bf16_dot(qm, state, ((1,), (1,)))
            st_ref[hd] = state * gamma + _bf16_dot(v, k_end, ((0,), (0,)))
            emit(hd, o)

    def chunk_rows(n):
        return pl.ds(pl.multiple_of(n * C, C), C)

    st_ref[...] = jnp.zeros_like(st_ref)
    prefix = (row >= col).astype(jnp.float32)

    def fwd(i, carry):
        for u in range(GLA_UNROLL):
            rows = chunk_rows(i * GLA_UNROLL + u)

            def emit(hd, o, rows=rows):
                of_ref[rows, hd * GLA_DV:(hd + 1) * GLA_DV] = o
            chunk(rows, wf_ref, bf_ref, prefix, row >= col, C - 1, emit)
        return carry
    lax.fori_loop(0, n_chunks // GLA_UNROLL, fwd, 0)

    st_ref[...] = jnp.zeros_like(st_ref)
    suffix = (row <= col).astype(jnp.float32)

    def bwd(i, carry):
        for u in range(GLA_UNROLL):
            rows = chunk_rows(n_chunks - 1 - (i * GLA_UNROLL + u))

            def emit(hd, o, rows=rows):
                cols = slice(hd * GLA_DV, (hd + 1) * GLA_DV)
                tot = of_ref[rows, cols] + o
                y = tot * lax.rsqrt(jnp.mean(tot * tot, axis=-1, keepdims=True) + EPS) * gh_ref[0, :, cols]
                r = r_ref[0, rows, cols]
                o_ref[0, rows, cols] = y * (r * (1.0 / (1.0 + jnp.exp(-r))))
            chunk(rows, wb_ref, bb_ref, suffix, row < col, 0, emit)
        return carry
    lax.fori_loop(0, n_chunks // GLA_UNROLL, bwd, 0)


def _gla(z, w_af, b_af, w_ab, b_ab, g_head):
    B, S, _ = z.shape
    pairs = GLA_HEADS // 2

    def pad_w(w, first_row):
        wp = jnp.zeros((pairs, LANES, LANES), jnp.float32)
        return wp.at[:, first_row:first_row + GLA_GATE_RANK, :].set(
            jnp.transpose(w.reshape(GLA_GATE_RANK, pairs, LANES), (1, 0, 2)))

    def lane_blk(off):
        return pl.BlockSpec((1, S, LANES), lambda b, j: (b, 0, off // LANES + j))

    def wide_blk(off):
        return pl.BlockSpec((1, S, 2 * LANES), lambda b, j: (b, 0, off // (2 * LANES) + j))

    w_spec = pl.BlockSpec((1, LANES, LANES), lambda b, j: (j, 0, 0))
    b_spec = pl.BlockSpec((1, 1, LANES), lambda b, j: (j, 0, 0))
    return pl.pallas_call(
        _gla_body,
        grid=(B, pairs),
        in_specs=[
            lane_blk(Z_Q), lane_blk(Z_K), wide_blk(Z_V), wide_blk(Z_R),
            pl.BlockSpec((1, S, LANES), lambda b, j: (b, 0, Z_A // LANES)),
            w_spec, b_spec, w_spec, b_spec,
            pl.BlockSpec((1, 1, 2 * LANES), lambda b, j: (j, 0, 0)),
        ],
        out_specs=pl.BlockSpec((1, S, 2 * LANES), lambda b, j: (b, 0, j)),
        out_shape=jax.ShapeDtypeStruct((B, S, GLA_WIDTH), jnp.float32),
        scratch_shapes=[
            pltpu.VMEM((S, 2 * LANES), jnp.float32),
            pltpu.VMEM((2, GLA_DV, LANES), jnp.float32),
        ],
        compiler_params=pltpu.CompilerParams(vmem_limit_bytes=VMEM_LIMIT),
        name="gla",
    )(z, z, z, z, z,
      pad_w(w_af, 0), b_af.reshape(pairs, 1, LANES),
      pad_w(w_ab, GLA_GATE_RANK), b_ab.reshape(pairs, 1, LANES),
      g_head.reshape(pairs, 1, 2 * LANES))


def _mix_out_body(x_ref, o_ref, p_ref, pp_ref, pn_ref, wp_ref, ps_ref, wo_ref, g_ref, gate_ref, out_ref, pe_ref):
    s = pl.program_id(1)
    n_s = pl.num_programs(1)
    seq_len = n_s * ROW_TILE
    pe_ref[pl.ds(0, POOL_HALO), :] = jnp.where(s > 0, pp_ref[0], 0.0)
    pe_ref[pl.ds(POOL_HALO, ROW_TILE), :] = p_ref[0]
    pe_ref[pl.ds(POOL_HALO + ROW_TILE, POOL_HALO), :] = jnp.where(s < n_s - 1, pn_ref[0], 0.0)
    pos = s * ROW_TILE + lax.broadcasted_iota(jnp.int32, (ROW_TILE, POOL_GROUP_WIDTH), 0)
    mix = jnp.dot(o_ref[0].astype(jnp.bfloat16), wo_ref[pl.ds(0, GLA_WIDTH), :],
                  preferred_element_type=jnp.float32)
    for gi, w in enumerate(POOL_WINDOWS):
        cols = pl.ds(gi * POOL_GROUP_WIDTH, POOL_GROUP_WIDTH)
        total = pe_ref[pl.ds(POOL_HALO - w // 2, ROW_TILE), cols]
        for d in range(-w // 2 + 1, w // 2):
            total = total + pe_ref[pl.ds(POOL_HALO + d, ROW_TILE), cols]
        count = (jnp.minimum(pos + w // 2, seq_len) - jnp.maximum(pos - w // 2, 0)).astype(jnp.float32)
        pooled = total / count - pe_ref[pl.ds(POOL_HALO, ROW_TILE), cols]
        y = jnp.dot(pooled.astype(jnp.bfloat16), wp_ref[gi], preferred_element_type=jnp.float32)
        y = y * ps_ref[:, gi * POOL_GROUP_WIDTH:(gi + 1) * POOL_GROUP_WIDTH]
        mix = mix + jnp.dot(y.astype(jnp.bfloat16),
                            wo_ref[pl.ds(GLA_WIDTH + gi * POOL_GROUP_WIDTH, POOL_GROUP_WIDTH), :],
                            preferred_element_type=jnp.float32)
    yn = mix * lax.rsqrt(jnp.mean(mix * mix, axis=-1, keepdims=True) + EPS) * g_ref[...]
    out_ref[0] = x_ref[0] + gate_ref[0] * yn


def _mix_out(x, o_gla, z, w_pool, pool_scale, w_out, g, gate):
    B, S, D = x.shape
    halo_per_tile = ROW_TILE // POOL_HALO
    last_halo = S // POOL_HALO - 1
    row_spec = pl.BlockSpec((1, ROW_TILE, D), lambda b, s: (b, s, 0))
    return pl.pallas_call(
        _mix_out_body,
        grid=(B, S // ROW_TILE),
        in_specs=[
            row_spec,
            pl.BlockSpec((1, ROW_TILE, GLA_WIDTH), lambda b, s: (b, s, 0)),
            pl.BlockSpec((1, ROW_TILE, POOL_WIDTH), lambda b, s: (b, s, Z_P // POOL_WIDTH)),
            pl.BlockSpec((1, POOL_HALO, POOL_WIDTH),
                         lambda b, s: (b, jnp.maximum(s * halo_per_tile - 1, 0), Z_P // POOL_WIDTH)),
            pl.BlockSpec((1, POOL_HALO, POOL_WIDTH),
                         lambda b, s: (b, jnp.minimum((s + 1) * halo_per_tile, last_halo), Z_P // POOL_WIDTH)),
            pl.BlockSpec(w_pool.shape, lambda b, s: (0, 0, 0)),
            pl.BlockSpec((1, POOL_WIDTH), lambda b, s: (0, 0)),
            pl.BlockSpec((D, D), lambda b, s: (0, 0)),
            pl.BlockSpec((1, D), lambda b, s: (0, 0)),
            pl.BlockSpec((1, 1, D), lambda b, s: (b, 0, 0)),
        ],
        out_specs=row_spec,
        out_shape=jax.ShapeDtypeStruct((B, S, D), jnp.float32),
        scratch_shapes=[pltpu.VMEM((ROW_TILE + 2 * POOL_HALO, POOL_WIDTH), jnp.float32)],
        compiler_params=pltpu.CompilerParams(vmem_limit_bytes=VMEM_LIMIT),
        name="mix_out",
    )(x, o_gla, z, z, z, w_pool.astype(jnp.bfloat16), pool_scale.reshape(1, POOL_WIDTH),
      w_out.astype(jnp.bfloat16), g.reshape(1, D), gate)


def _peer_ffn(x, g_pre, sc2, sh2, w_q, sub_keys, u_tab, v_tab):
    B, S, D = x.shape
    h2, experts_t, gates_t = _peer_retrieve(x, g_pre, sc2, sh2, w_q, sub_keys)
    coef_dup = _peer_u_pass(experts_t, h2.reshape(B * S, D // LANES, LANES), gates_t, _pack_table(u_tab))
    return _peer_v_pass(experts_t, coef_dup, _pack_table(v_tab))


NEG_INF = float("-inf")
RETR_TOKENS = 256


def _top16_rows(s, row_id):
    vals, ids = [], []
    for _ in range(PEER_TOPK):
        m = jnp.max(s, axis=0, keepdims=True)
        r = jnp.min(jnp.where(s == m, row_id, jnp.iinfo(jnp.int32).max), axis=0, keepdims=True)
        vals.append(m)
        ids.append(r)
        s = jnp.where(row_id == r, NEG_INF, s)
    return vals, ids


def _peer_retrieve_body(x_ref, g_ref, sc_ref, sh_ref, wq_ref, keys_ref, h_ref, idx_ref, gate_ref, q_ref):
    x = x_ref[0]
    h = x * lax.rsqrt(jnp.mean(x * x, axis=-1, keepdims=True) + EPS) * g_ref[...]
    h = h * (1.0 + sc_ref[0]) + sh_ref[0]
    for j in range(h.shape[1] // LANES):
        h_ref[pl.ds(j, ROW_TILE, stride=8), :] = h[:, j * LANES:(j + 1) * LANES]
    q_ref[...] = jnp.dot(h.astype(jnp.bfloat16), wq_ref[...],
                         preferred_element_type=jnp.float32).astype(jnp.bfloat16)
    key_iota = lax.broadcasted_iota(jnp.int32, (PEER_NKEYS, RETR_TOKENS), 0)
    sub8 = lax.broadcasted_iota(jnp.int32, (8, RETR_TOKENS), 0)
    cand_id = jnp.concatenate([sub8, sub8 + 8] + [sub8 + a * PEER_TOPK for a in range(1, 8)]
                              + [(sub8 + 8) * PEER_TOPK], axis=0)
    chunks = ROW_TILE // RETR_TOKENS

    def head_chunk(i, carry):
        hd = lax.shift_right_logical(i, chunks.bit_length() - 1)
        ch = i & (chunks - 1)
        tok = pl.ds(pl.multiple_of(ch * RETR_TOKENS, RETR_TOKENS), RETR_TOKENS)
        vals, rows = [], []
        for p in range(2):
            col = pl.multiple_of((hd * 2 + p) * PEER_HALF, PEER_HALF)
            qs = q_ref[tok, pl.ds(col, PEER_HALF)]
            s = lax.dot_general(keys_ref[hd, p], qs, (((1,), (1,)), ((), ())),
                                preferred_element_type=jnp.float32)
            v, r = _top16_rows(s, key_iota)
            vals.append(v)
            rows.append(r)
        v1_hi = jnp.concatenate(vals[0][8:], axis=0)
        v2_lo = jnp.concatenate(vals[1][:8], axis=0)
        v2_hi = jnp.concatenate(vals[1][8:], axis=0)
        cand = jnp.concatenate([vals[0][0] + v2_lo, vals[0][0] + v2_hi]
                               + [vals[0][a] + v2_lo for a in range(1, 8)]
                               + [v1_hi + vals[1][0]], axis=0)
        top_s, pos = _top16_rows(cand, cand_id)
        pos = jnp.concatenate(pos, axis=0)
        pa = pos >> 4
        pb = pos & (PEER_TOPK - 1)
        i1 = jnp.zeros_like(pos)
        i2 = jnp.zeros_like(pos)
        for a in range(PEER_TOPK):
            i1 = jnp.where(pa == a, rows[0][a], i1)
            i2 = jnp.where(pb == a, rows[1][a], i2)
        ts = jnp.concatenate(top_s, axis=0)
        e = jnp.exp(ts - top_s[0])
        gates = e / jnp.sum(e, axis=0, keepdims=True)
        out_rows = pl.ds(pl.multiple_of(hd * PEER_TOPK, PEER_TOPK), PEER_TOPK)
        idx_ref[out_rows, tok] = (i1 * PEER_NKEYS + i2) * PEER_CHUNKS
        gate_ref[out_rows, tok] = gates
        return carry

    lax.fori_loop(0, PEER_HEADS * chunks, head_chunk, 0)


def _peer_retrieve(x, g_pre, sc2, sh2, w_q, sub_keys):
    B, S, D = x.shape
    T = B * S
    nq = w_q.shape[1]
    spb = S // ROW_TILE
    tok_spec = pl.BlockSpec((1, ROW_TILE, D), lambda b, s: (b, s, 0))
    mod_spec = pl.BlockSpec((1, 1, D), lambda b, s: (b, 0, 0))
    slot_spec = pl.BlockSpec((PEER_SLOTS, ROW_TILE), lambda b, s: (0, b * spb + s))
    return pl.pallas_call(
        _peer_retrieve_body,
        grid=(B, spb),
        in_specs=[
            tok_spec,
            pl.BlockSpec((1, D), lambda b, s: (0, 0)),
            mod_spec,
            mod_spec,
            pl.BlockSpec((D, nq), lambda b, s: (0, 0)),
            pl.BlockSpec(sub_keys.shape, lambda b, s: (0, 0, 0, 0)),
        ],
        out_specs=[pl.BlockSpec((ROW_TILE * D // LANES, LANES), lambda b, s: (b * spb + s, 0)), slot_spec, slot_spec],
        out_shape=[
            jax.ShapeDtypeStruct((T * D // LANES, LANES), jnp.float32),
            jax.ShapeDtypeStruct((PEER_SLOTS, T), jnp.int32),
            jax.ShapeDtypeStruct((PEER_SLOTS, T), jnp.float32),
        ],
        scratch_shapes=[pltpu.VMEM((ROW_TILE, nq), jnp.bfloat16)],
        compiler_params=pltpu.CompilerParams(vmem_limit_bytes=VMEM_LIMIT),
        name="peer_retrieve",
    )(x, g_pre.reshape(1, D), sc2, sh2, w_q.astype(jnp.bfloat16), sub_keys.astype(jnp.bfloat16))


PEER_SLOTS = PEER_HEADS * PEER_TOPK
PEER_CHUNKS = D_MODEL // 256
PEER_TB = 256
TILE_STRIDE = PEER_SLOTS + 8


PACK_ROWS = 256


def _bf16_bits(x):
    return pltpu.bitcast(x.astype(jnp.bfloat16).astype(jnp.float32), jnp.uint32)


def _pack_table_body(x_ref, o_ref):
    half = x_ref.shape[1] // 2
    for c in range(PEER_CHUNKS):
        lo = _bf16_bits(x_ref[:, c * LANES:(c + 1) * LANES])
        hi = _bf16_bits(x_ref[:, half + c * LANES:half + (c + 1) * LANES])
        o_ref[pl.ds(c, PACK_ROWS, stride=PEER_CHUNKS), :] = lax.shift_right_logical(lo, jnp.uint32(16)) | hi


def _pack_table(tab):
    E, D = tab.shape
    return pl.pallas_call(
        _pack_table_body,
        grid=(E // PACK_ROWS,),
        in_specs=[pl.BlockSpec((PACK_ROWS, D), lambda i: (i, 0))],
        out_specs=pl.BlockSpec((PACK_ROWS * PEER_CHUNKS, LANES), lambda i: (i, 0)),
        out_shape=jax.ShapeDtypeStruct((E * PEER_CHUNKS, LANES), jnp.uint32),
        name="pack_table",
    )(tab)


def _gather_token(idx_ref, tab_ref, tile_ref, t):
    for k in range(PEER_SLOTS):
        row = pl.multiple_of(idx_ref[k, t], PEER_CHUNKS)
        tile_ref[pl.ds(k, PEER_CHUNKS, stride=TILE_STRIDE), :] = tab_ref[pl.ds(row, PEER_CHUNKS), :]


def _tile_chunk_f32(tile_ref, c):
    w = tile_ref[pl.ds(c * TILE_STRIDE, PEER_SLOTS), :]
    return pltpu.bitcast(w, jnp.bfloat16).astype(jnp.float32)


def _token_pairs(idx_ref, tab_ref, tile_a, tile_b, consume):
    _gather_token(idx_ref, tab_ref, tile_a, 0)

    def pair(j, carry):
        t0 = 2 * j
        _gather_token(idx_ref, tab_ref, tile_b, t0 + 1)
        consume(tile_a, t0)
        _gather_token(idx_ref, tab_ref, tile_a, jnp.minimum(t0 + 2, PEER_TB - 1))
        consume(tile_b, t0 + 1)
        return carry
    lax.fori_loop(0, PEER_TB // 2, pair, 0)


def _token_block_cols(t):
    return pl.ds(pl.multiple_of(lax.shift_right_logical(t, 7) * LANES, LANES), LANES)


def _peer_u_body(idx_ref, h_ref, gate_ref, tab_ref, coef_ref, tile_a, tile_b, sblk_ref):
    lane = lax.broadcasted_iota(jnp.int32, (2 * PEER_SLOTS, LANES), 1)
    odd = (lax.broadcasted_iota(jnp.int32, (8, LANES), 0) & 1) == 1

    def consume(tile_ref, t):
        acc = jnp.zeros((2 * PEER_SLOTS // 8, 8, LANES), jnp.float32)
        for c in range(PEER_CHUNKS):
            h_lo = jnp.broadcast_to(h_ref[t, pl.ds(c, 1), :], (8, LANES))
            h_hi = jnp.broadcast_to(h_ref[t, pl.ds(PEER_CHUNKS + c, 1), :], (8, LANES))
            pattern = jnp.where(odd, h_hi, h_lo)
            x = _tile_chunk_f32(tile_ref, c).reshape(2 * PEER_SLOTS // 8, 8, LANES)
            acc = acc + x * pattern[None]
        s = jnp.sum(acc.reshape(2 * PEER_SLOTS, LANES), axis=1, keepdims=True)
        tl = t & (LANES - 1)
        sblk_ref[...] = jnp.where(lane == tl, s, sblk_ref[...])

        @pl.when(tl == LANES - 1)
        def _():
            even_rows = sblk_ref[pl.ds(0, PEER_SLOTS, stride=2), :]
            odd_rows = sblk_ref[pl.ds(1, PEER_SLOTS, stride=2), :]
            a = even_rows + odd_rows
            cols = _token_block_cols(t)
            coef = gate_ref[:, cols] * (0.5 * a * (1.0 + lax.erf(a * (2.0 ** -0.5))))
            sblk_ref[pl.ds(0, PEER_SLOTS, stride=2), :] = coef
            sblk_ref[pl.ds(1, PEER_SLOTS, stride=2), :] = coef
            coef_ref[:, cols] = sblk_ref[...]

    _token_pairs(idx_ref, tab_ref, tile_a, tile_b, consume)


def _peer_u_pass(experts_t, ht, gates_t, tab):
    T = ht.shape[0]
    return pl.pallas_call(
        _peer_u_body,
        grid=(T // PEER_TB,),
        in_specs=[
            pl.BlockSpec((PEER_SLOTS, PEER_TB), lambda i: (0, i), memory_space=pltpu.SMEM),
            pl.BlockSpec((PEER_TB, 8, LANES), lambda i: (i, 0, 0)),
            pl.BlockSpec((PEER_SLOTS, PEER_TB), lambda i: (0, i)),
            pl.BlockSpec(memory_space=pltpu.VMEM),
        ],
        out_specs=pl.BlockSpec((2 * PEER_SLOTS, PEER_TB), lambda i: (0, i)),
        out_shape=jax.ShapeDtypeStruct((2 * PEER_SLOTS, T), jnp.float32),
        scratch_shapes=[
            pltpu.VMEM((PEER_CHUNKS * TILE_STRIDE, LANES), jnp.uint32),
            pltpu.VMEM((PEER_CHUNKS * TILE_STRIDE, LANES), jnp.uint32),
            pltpu.VMEM((2 * PEER_SLOTS, LANES), jnp.float32),
        ],
        compiler_params=pltpu.CompilerParams(vmem_limit_bytes=VMEM_LIMIT),
        name="peer_u_pass",
    )(experts_t, ht, gates_t, tab)


def _peer_v_body(idx_ref, coef_ref, tab_ref, y_ref, tile_a, tile_b):
    lane = lax.broadcasted_iota(jnp.int32, (2 * PEER_SLOTS, LANES), 1)

    def consume(tile_ref, t):
        tl = t & (LANES - 1)
        coef_blk = coef_ref[:, _token_block_cols(t)]
        ccol = jnp.sum(jnp.where(lane == tl, coef_blk, 0.0), axis=1, keepdims=True)
        ccol = ccol.reshape(2 * PEER_SLOTS // 8, 8, 1)
        for c in range(PEER_CHUNKS):
            x = _tile_chunk_f32(tile_ref, c).reshape(2 * PEER_SLOTS // 8, 8, LANES)
            part = jnp.sum(x * ccol, axis=0)
            part = part + pltpu.roll(part, 4, axis=0)
            part = part + pltpu.roll(part, 2, axis=0)
            y_ref[t, pl.ds(c, 1), :] = part[0:1]
            y_ref[t, pl.ds(PEER_CHUNKS + c, 1), :] = part[1:2]

    _token_pairs(idx_ref, tab_ref, tile_a, tile_b, consume)


def _peer_v_pass(experts_t, coef_dup, tab):
    T = experts_t.shape[1]
    y = pl.pallas_call(
        _peer_v_body,
        grid=(T // PEER_TB,),
        in_specs=[
            pl.BlockSpec((PEER_SLOTS, PEER_TB), lambda i: (0, i), memory_space=pltpu.SMEM),
            pl.BlockSpec((2 * PEER_SLOTS, PEER_TB), lambda i: (0, i)),
            pl.BlockSpec(memory_space=pltpu.VMEM),
        ],
        out_specs=pl.BlockSpec((PEER_TB, 8, LANES), lambda i: (i, 0, 0)),
        out_shape=jax.ShapeDtypeStruct((T, 8, LANES), jnp.float32),
        scratch_shapes=[pltpu.VMEM((PEER_CHUNKS * TILE_STRIDE, LANES), jnp.uint32)] * 2,
        compiler_params=pltpu.CompilerParams(vmem_limit_bytes=VMEM_LIMIT),
        name="peer_v_pass",
    )(experts_t, coef_dup, tab)
    return y


def _post_residual_body(x_ref, y_ref, g_ref, gate_ref, o_ref):
    pieces = [y_ref[pl.ds(j, ROW_TILE, stride=8), :] for j in range(x_ref.shape[2] // LANES)]
    sq = pieces[0] * pieces[0]
    for p in pieces[1:]:
        sq = sq + p * p
    inv = lax.rsqrt(jnp.sum(sq, axis=-1, keepdims=True) * (1.0 / x_ref.shape[2]) + EPS)
    for j, p in enumerate(pieces):
        cols = slice(j * LANES, (j + 1) * LANES)
        o_ref[0, :, cols] = x_ref[0, :, cols] + gate_ref[0, :, cols] * (p * inv * g_ref[:, cols])


def _post_residual(x, y, g, gate):
    B, S, D = x.shape
    pieces = D // LANES
    spb = S // ROW_TILE
    row_spec = pl.BlockSpec((1, ROW_TILE, D), lambda b, s: (b, s, 0))
    return pl.pallas_call(
        _post_residual_body,
        grid=(B, spb),
        in_specs=[
            row_spec,
            pl.BlockSpec((ROW_TILE * pieces, LANES), lambda b, s: (b * spb + s, 0)),
            pl.BlockSpec((1, D), lambda b, s: (0, 0)),
            pl.BlockSpec((1, 1, D), lambda b, s: (b, 0, 0)),
        ],
        out_specs=row_spec,
        out_shape=jax.ShapeDtypeStruct((B, S, D), x.dtype),
        name="post_residual",
    )(x, y.reshape(B * S * pieces, LANES), g.reshape(1, D), gate)


def kernel(x, c, w_mod, b_mod, g_pre_mix, g_post_mix, w_in, w_alpha_f, b_alpha_f, w_alpha_b, b_alpha_b,
           g_gla_head, w_pool, pool_scale, w_out, g_pre_ffn, g_post_ffn, w_peer_q, peer_sub_keys,
           peer_u, peer_v):
    for l in range(w_mod.shape[0]):
        mod = _modulation(c, w_mod[l], b_mod[l])
        sh1, sc1, gt1, sh2, sc2, gt2 = [m[:, None, :] for m in jnp.split(mod, N_MOD, axis=-1)]
        z = _in_proj(x, g_pre_mix[l], sc1, sh1, w_in[l])
        o_gla = _gla(z, w_alpha_f[l], b_alpha_f[l], w_alpha_b[l], b_alpha_b[l], g_gla_head[l])
        x = _mix_out(x, o_gla, z, w_pool[l], pool_scale[l], w_out[l], g_post_mix[l], gt1)
        y = _peer_ffn(x, g_pre_ffn[l], sc2, sh2, w_peer_q[l], peer_sub_keys[l], peer_u[l], peer_v[l])
        x = _post_residual(x, y, g_post_ffn[l], gt2)
    return x
```

```python
import jax
import jax.numpy as jnp
from jax import lax
from jax.experimental import pallas as pl
from jax.experimental.pallas import tpu as pltpu

D_MODEL = 1024
GLA_WIDTH = 512
POOL_WIDTH = 512
GLA_HEADS = 4
GLA_DV = 128
GLA_DK = 64
GLA_KEY_WIDTH = 256
GLA_GATE_RANK = 16
GLA_TAU = 16.0
GLA_CHUNK = 64
POOL_WINDOWS = (2, 4, 8, 16)
POOL_GROUP_WIDTH = 128
PEER_HEADS = 8
PEER_NKEYS = 128
PEER_HALF = 128
PEER_TOPK = 16
N_MOD = 6
EPS = 1e-6

LANES = 128
VMEM_LIMIT = 48 * 1024 * 1024
ROW_TILE = 512

Z_Q, Z_K, Z_V, Z_R, Z_P, Z_A = 0, 256, 512, 1024, 1536, 2048
Z_COLS = 2176
POOL_HALO = 8
GLA_UNROLL = 8


def _mod_body(c_ref, w_ref, b_ref, o_ref):
    c = c_ref[...]
    cond = c * (1.0 / (1.0 + jnp.exp(-c)))
    o_ref[...] = jnp.dot(cond.astype(jnp.bfloat16), w_ref[...].astype(jnp.bfloat16),
                         preferred_element_type=jnp.float32) + b_ref[...]


def _modulation(c, w_mod, b_mod):
    B, D = c.shape
    n = w_mod.shape[1]
    return pl.pallas_call(
        _mod_body,
        grid=(n // D,),
        in_specs=[
            pl.BlockSpec((B, D), lambda j: (0, 0)),
            pl.BlockSpec((D, D), lambda j: (0, j)),
            pl.BlockSpec((1, D), lambda j: (0, j)),
        ],
        out_specs=pl.BlockSpec((B, D), lambda j: (0, j)),
        out_shape=jax.ShapeDtypeStruct((B, n), jnp.float32),
        name="modulation",
    )(c, w_mod, b_mod.reshape(1, n))


def _in_proj_body(x_ref, g_ref, sc_ref, sh_ref, w_ref, z_ref):
    x = x_ref[0]
    h = x * lax.rsqrt(jnp.mean(x * x, axis=-1, keepdims=True) + EPS) * g_ref[...]
    h = h * (1.0 + sc_ref[0]) + sh_ref[0]
    z_ref[0] = jnp.dot(h.astype(jnp.bfloat16), w_ref[...], preferred_element_type=jnp.float32)


def _in_proj(x, g, sc, sh, w_in):
    B, S, D = x.shape
    w = jnp.concatenate([w_in[:, :Z_P], w_in[:, Z_P + 32:], w_in[:, Z_P:Z_P + 32],
                         jnp.zeros((D, Z_COLS - Z_A - 32), w_in.dtype)], axis=1).astype(jnp.bfloat16)
    mod_spec = pl.BlockSpec((1, 1, D), lambda b, s: (b, 0, 0))
    return pl.pallas_call(
        _in_proj_body,
        grid=(B, S // ROW_TILE),
        in_specs=[
            pl.BlockSpec((1, ROW_TILE, D), lambda b, s: (b, s, 0)),
            pl.BlockSpec((1, D), lambda b, s: (0, 0)),
            mod_spec,
            mod_spec,
            pl.BlockSpec((D, Z_COLS), lambda b, s: (0, 0)),
        ],
        out_specs=pl.BlockSpec((1, ROW_TILE, Z_COLS), lambda b, s: (b, s, 0)),
        out_shape=jax.ShapeDtypeStruct((B, S, Z_COLS), jnp.float32),
        compiler_params=pltpu.CompilerParams(vmem_limit_bytes=VMEM_LIMIT),
        name="in_proj",
    )(x, g.reshape(1, D), sc, sh, w)


def _log_sigmoid(x):
    return jnp.minimum(x, 0.0) - jnp.log1p(jnp.exp(-jnp.abs(x)))


def _bf16_dot(a, b, dims):
    return lax.dot_general(a.astype(jnp.bfloat16), b.astype(jnp.bfloat16), (dims, ((), ())),
                           preferred_element_type=jnp.float32)


def _gla_body(q_ref, k_ref, v_ref, r_ref, a_ref, wf_ref, bf_ref, wb_ref, bb_ref, gh_ref, o_ref,
              of_ref, st_ref):
    C = GLA_CHUNK
    n_chunks = q_ref.shape[1] // C
    row = lax.broadcasted_iota(jnp.int32, (C, C), 0)
    col = lax.broadcasted_iota(jnp.int32, (C, C), 1)
    lane = lax.broadcasted_iota(jnp.int32, (1, LANES), 1)
    head_lanes = [lane < GLA_DK, lane >= GLA_DK]

    def chunk_group(chunk_ids, w_ref, b_ref, cum_mask, keep_mask, total_row, emit):
        rows = [pl.ds(pl.multiple_of(n * C, C), C) for n in chunk_ids]
        log_a = [_log_sigmoid(_bf16_dot(a_ref[0, r, :], w_ref[0], ((1,), (0,))) + b_ref[0]) * (1.0 / GLA_TAU)
                 for r in rows]
        b = [jnp.dot(cum_mask, la, precision=lax.Precision.HIGHEST, preferred_element_type=jnp.float32)
             for la in log_a]
        qm, k_inv, k_end, gamma = [], [], [], []
        for r, bc in zip(rows, b):
            b_tot = bc[total_row:total_row + 1, :]
            q_dec = q_ref[0, r, :] * (GLA_DK ** -0.5) * jnp.exp(bc)
            k = k_ref[0, r, :]
            qm.append([jnp.where(head_lanes[hd], q_dec, 0.0) for hd in range(2)])
            k_inv.append(k * jnp.exp(-bc))
            k_end.append(k * jnp.exp(b_tot - bc))
            gamma.append(jnp.exp(b_tot))
        v = [[v_ref[0, r, hd * GLA_DV:(hd + 1) * GLA_DV] for hd in range(2)] for r in rows]
        scores = [[jnp.where(keep_mask, _bf16_dot(qm[u][hd], k_inv[u], ((1,), (1,))), 0.0) for hd in range(2)]
                  for u in range(len(rows))]
        o_intra = [[_bf16_dot(scores[u][hd], v[u][hd], ((1,), (0,))) for hd in range(2)] for u in range(len(rows))]
        kv = [[_bf16_dot(v[u][hd], k_end[u], ((0,), (0,))) for hd in range(2)] for u in range(len(rows))]
        for u, r in enumerate(rows):
            for hd in range(2):
                state = st_ref[hd]
                emit(r, hd, o_intra[u][hd] + _bf16_dot(qm[u][hd], state, ((1,), (1,))))
                st_ref[hd] = state * gamma[u] + kv[u][hd]

    st_ref[...] = jnp.zeros_like(st_ref)
    prefix = (row >= col).astype(jnp.float32)

    def emit_fwd(rows, hd, o):
        of_ref[rows, hd * GLA_DV:(hd + 1) * GLA_DV] = o

    def fwd(i, carry):
        chunk_group([i * GLA_UNROLL + u for u in range(GLA_UNROLL)], wf_ref, bf_ref, prefix, row >= col, C - 1,
                    emit_fwd)
        return carry
    lax.fori_loop(0, n_chunks // GLA_UNROLL, fwd, 0)

    st_ref[...] = jnp.zeros_like(st_ref)
    suffix = (row <= col).astype(jnp.float32)

    def emit_bwd(rows, hd, o):
        cols = slice(hd * GLA_DV, (hd + 1) * GLA_DV)
        tot = of_ref[rows, cols] + o
        y = tot * lax.rsqrt(jnp.mean(tot * tot, axis=-1, keepdims=True) + EPS) * gh_ref[0, :, cols]
        r = r_ref[0, rows, cols]
        o_ref[0, rows, cols] = y * (r * (1.0 / (1.0 + jnp.exp(-r))))

    def bwd(i, carry):
        chunk_group([n_chunks - 1 - (i * GLA_UNROLL + u) for u in range(GLA_UNROLL)], wb_ref, bb_ref, suffix,
                    row < col, 0, emit_bwd)
        return carry
    lax.fori_loop(0, n_chunks // GLA_UNROLL, bwd, 0)


def _gla(z, w_af, b_af, w_ab, b_ab, g_head):
    B, S, _ = z.shape
    pairs = GLA_HEADS // 2

    def pad_w(w, first_row):
        wp = jnp.zeros((pairs, LANES, LANES), jnp.float32)
        return wp.at[:, first_row:first_row + GLA_GATE_RANK, :].set(
            jnp.transpose(w.reshape(GLA_GATE_RANK, pairs, LANES), (1, 0, 2)))

    def lane_blk(off):
        return pl.BlockSpec((1, S, LANES), lambda b, j: (b, 0, off // LANES + j))

    def wide_blk(off):
        return pl.BlockSpec((1, S, 2 * LANES), lambda b, j: (b, 0, off // (2 * LANES) + j))

    w_spec = pl.BlockSpec((1, LANES, LANES), lambda b, j: (j, 0, 0))
    b_spec = pl.BlockSpec((1, 1, LANES), lambda b, j: (j, 0, 0))
    return pl.pallas_call(
        _gla_body,
        grid=(B, pairs),
        in_specs=[
            lane_blk(Z_Q), lane_blk(Z_K), wide_blk(Z_V), wide_blk(Z_R),
            pl.BlockSpec((1, S, LANES), lambda b, j: (b, 0, Z_A // LANES)),
            w_spec, b_spec, w_spec, b_spec,
            pl.BlockSpec((1, 1, 2 * LANES), lambda b, j: (j, 0, 0)),
        ],
        out_specs=pl.BlockSpec((1, S, 2 * LANES), lambda b, j: (b, 0, j)),
        out_shape=jax.ShapeDtypeStruct((B, S, GLA_WIDTH), jnp.float32),
        scratch_shapes=[
            pltpu.VMEM((S, 2 * LANES), jnp.float32),
            pltpu.VMEM((2, GLA_DV, LANES), jnp.float32),
        ],
        compiler_params=pltpu.CompilerParams(vmem_limit_bytes=VMEM_LIMIT),
        name="gla",
    )(z, z, z, z, z,
      pad_w(w_af, 0), b_af.reshape(pairs, 1, LANES),
      pad_w(w_ab, GLA_GATE_RANK), b_ab.reshape(pairs, 1, LANES),
      g_head.reshape(pairs, 1, 2 * LANES))


def _mix_out_body(x_ref, o_ref, p_ref, pp_ref, pn_ref, wp_ref, ps_ref, wo_ref, g_ref, gate_ref, out_ref, pe_ref):
    s = pl.program_id(1)
    n_s = pl.num_programs(1)
    seq_len = n_s * ROW_TILE
    pe_ref[pl.ds(0, POOL_HALO), :] = jnp.where(s > 0, pp_ref[0], 0.0)
    pe_ref[pl.ds(POOL_HALO, ROW_TILE), :] = p_ref[0]
    pe_ref[pl.ds(POOL_HALO + ROW_TILE, POOL_HALO), :] = jnp.where(s < n_s - 1, pn_ref[0], 0.0)
    pos = s * ROW_TILE + lax.broadcasted_iota(jnp.int32, (ROW_TILE, POOL_GROUP_WIDTH), 0)
    mix = jnp.dot(o_ref[0].astype(jnp.bfloat16), wo_ref[pl.ds(0, GLA_WIDTH), :],
                  preferred_element_type=jnp.float32)
    for gi, w in enumerate(POOL_WINDOWS):
        cols = pl.ds(gi * POOL_GROUP_WIDTH, POOL_GROUP_WIDTH)
        total = pe_ref[pl.ds(POOL_HALO - w // 2, ROW_TILE), cols]
        for d in range(-w // 2 + 1, w // 2):
            total = total + pe_ref[pl.ds(POOL_HALO + d, ROW_TILE), cols]
        count = (jnp.minimum(pos + w // 2, seq_len) - jnp.maximum(pos - w // 2, 0)).astype(jnp.float32)
        pooled = total / count - pe_ref[pl.ds(POOL_HALO, ROW_TILE), cols]
        y = jnp.dot(pooled.astype(jnp.bfloat16), wp_ref[gi], preferred_element_type=jnp.float32)
        y = y * ps_ref[:, gi * POOL_GROUP_WIDTH:(gi + 1) * POOL_GROUP_WIDTH]
        mix = mix + jnp.dot(y.astype(jnp.bfloat16),
                            wo_ref[pl.ds(GLA_WIDTH + gi * POOL_GROUP_WIDTH, POOL_GROUP_WIDTH), :],
                            preferred_element_type=jnp.float32)
    yn = mix * lax.rsqrt(jnp.mean(mix * mix, axis=-1, keepdims=True) + EPS) * g_ref[...]
    out_ref[0] = x_ref[0] + gate_ref[0] * yn


def _mix_out(x, o_gla, z, w_pool, pool_scale, w_out, g, gate):
    B, S, D = x.shape
    halo_per_tile = ROW_TILE // POOL_HALO
    last_halo = S // POOL_HALO - 1
    row_spec = pl.BlockSpec((1, ROW_TILE, D), lambda b, s: (b, s, 0))
    return pl.pallas_call(
        _mix_out_body,
        grid=(B, S // ROW_TILE),
        in_specs=[
            row_spec,
            pl.BlockSpec((1, ROW_TILE, GLA_WIDTH), lambda b, s: (b, s, 0)),
            pl.BlockSpec((1, ROW_TILE, POOL_WIDTH), lambda b, s: (b, s, Z_P // POOL_WIDTH)),
            pl.BlockSpec((1, POOL_HALO, POOL_WIDTH),
                         lambda b, s: (b, jnp.maximum(s * halo_per_tile - 1, 0), Z_P // POOL_WIDTH)),
            pl.BlockSpec((1, POOL_HALO, POOL_WIDTH),
                         lambda b, s: (b, jnp.minimum((s + 1) * halo_per_tile, last_halo), Z_P // POOL_WIDTH)),
            pl.BlockSpec(w_pool.shape, lambda b, s: (0, 0, 0)),
            pl.BlockSpec((1, POOL_WIDTH), lambda b, s: (0, 0)),
            pl.BlockSpec((D, D), lambda b, s: (0, 0)),
            pl.BlockSpec((1, D), lambda b, s: (0, 0)),
            pl.BlockSpec((1, 1, D), lambda b, s: (b, 0, 0)),
        ],
        out_specs=row_spec,
        out_shape=jax.ShapeDtypeStruct((B, S, D), jnp.float32),
        scratch_shapes=[pltpu.VMEM((ROW_TILE + 2 * POOL_HALO, POOL_WIDTH), jnp.float32)],
        compiler_params=pltpu.CompilerParams(vmem_limit_bytes=VMEM_LIMIT),
        name="mix_out",
    )(x, o_gla, z, z, z, w_pool.astype(jnp.bfloat16), pool_scale.reshape(1, POOL_WIDTH),
      w_out.astype(jnp.bfloat16), g.reshape(1, D), gate)


def _peer_ffn(x, g_pre, sc2, sh2, w_q, sub_keys, u_tab, v_tab):
    B, S, D = x.shape
    h2, experts_t, gates_t = _peer_retrieve(x, g_pre, sc2, sh2, w_q, sub_keys)
    coef_dup = _peer_u_pass(experts_t, h2.reshape(B * S, D // LANES, LANES), gates_t, _pack_table(u_tab))
    return _peer_v_pass(experts_t, coef_dup, _pack_table(v_tab))


NEG_INF = float("-inf")
RETR_TOKENS = 256


def _top16_rows(s, row_id):
    vals, ids = [], []
    for _ in range(PEER_TOPK):
        m = jnp.max(s, axis=0, keepdims=True)
        r = jnp.min(jnp.where(s == m, row_id, jnp.iinfo(jnp.int32).max), axis=0, keepdims=True)
        vals.append(m)
        ids.append(r)
        s = jnp.where(row_id == r, NEG_INF, s)
    return vals, ids


def _peer_retrieve_body(x_ref, g_ref, sc_ref, sh_ref, wq_ref, keys_ref, h_ref, idx_ref, gate_ref, q_ref):
    x = x_ref[0]
    h = x * lax.rsqrt(jnp.mean(x * x, axis=-1, keepdims=True) + EPS) * g_ref[...]
    h = h * (1.0 + sc_ref[0]) + sh_ref[0]
    for j in range(h.shape[1] // LANES):
        h_ref[pl.ds(j, ROW_TILE, stride=8), :] = h[:, j * LANES:(j + 1) * LANES]
    q_ref[...] = jnp.dot(h.astype(jnp.bfloat16), wq_ref[...],
                         preferred_element_type=jnp.float32).astype(jnp.bfloat16)
    key_iota = lax.broadcasted_iota(jnp.int32, (PEER_NKEYS, RETR_TOKENS), 0)
    sub8 = lax.broadcasted_iota(jnp.int32, (8, RETR_TOKENS), 0)
    cand_id = jnp.concatenate([sub8, sub8 + 8] + [sub8 + a * PEER_TOPK for a in range(1, 8)]
                              + [(sub8 + 8) * PEER_TOPK], axis=0)
    chunks = ROW_TILE // RETR_TOKENS

    def head_chunk(i, carry):
        hd = lax.shift_right_logical(i, chunks.bit_length() - 1)
        ch = i & (chunks - 1)
        tok = pl.ds(pl.multiple_of(ch * RETR_TOKENS, RETR_TOKENS), RETR_TOKENS)
        vals, rows = [], []
        for p in range(2):
            col = pl.multiple_of((hd * 2 + p) * PEER_HALF, PEER_HALF)
            qs = q_ref[tok, pl.ds(col, PEER_HALF)]
            s = lax.dot_general(keys_ref[hd, p], qs, (((1,), (1,)), ((), ())),
                                preferred_element_type=jnp.float32)
            v, r = _top16_rows(s, key_iota)
            vals.append(v)
            rows.append(r)
        v1_hi = jnp.concatenate(vals[0][8:], axis=0)
        v2_lo = jnp.concatenate(vals[1][:8], axis=0)
        v2_hi = jnp.concatenate(vals[1][8:], axis=0)
        cand = jnp.concatenate([vals[0][0] + v2_lo, vals[0][0] + v2_hi]
                               + [vals[0][a] + v2_lo for a in range(1, 8)]
                               + [v1_hi + vals[1][0]], axis=0)
        top_s, pos = _top16_rows(cand, cand_id)
        pos = jnp.concatenate(pos, axis=0)
        pa = pos >> 4
        pb = pos & (PEER_TOPK - 1)
        i1 = jnp.zeros_like(pos)
        i2 = jnp.zeros_like(pos)
        for a in range(PEER_TOPK):
            i1 = jnp.where(pa == a, rows[0][a], i1)
            i2 = jnp.where(pb == a, rows[1][a], i2)
        ts = jnp.concatenate(top_s, axis=0)
        e = jnp.exp(ts - top_s[0])
        gates = e / jnp.sum(e, axis=0, keepdims=True)
        out_rows = pl.ds(pl.multiple_of(hd * PEER_TOPK, PEER_TOPK), PEER_TOPK)
        idx_ref[out_rows, tok] = (i1 * PEER_NKEYS + i2) * PEER_CHUNKS
        gate_ref[out_rows, tok] = gates
        return carry

    lax.fori_loop(0, PEER_HEADS * chunks, head_chunk, 0)


def _peer_retrieve(x, g_pre, sc2, sh2, w_q, sub_keys):
    B, S, D = x.shape
    T = B * S
    nq = w_q.shape[1]
    spb = S // ROW_TILE
    tok_spec = pl.BlockSpec((1, ROW_TILE, D), lambda b, s: (b, s, 0))
    mod_spec = pl.BlockSpec((1, 1, D), lambda b, s: (b, 0, 0))
    slot_spec = pl.BlockSpec((PEER_SLOTS, ROW_TILE), lambda b, s: (0, b * spb + s))
    return pl.pallas_call(
        _peer_retrieve_body,
        grid=(B, spb),
        in_specs=[
            tok_spec,
            pl.BlockSpec((1, D), lambda b, s: (0, 0)),
            mod_spec,
            mod_spec,
            pl.BlockSpec((D, nq), lambda b, s: (0, 0)),
            pl.BlockSpec(sub_keys.shape, lambda b, s: (0, 0, 0, 0)),
        ],
        out_specs=[pl.BlockSpec((ROW_TILE * D // LANES, LANES), lambda b, s: (b * spb + s, 0)), slot_spec, slot_spec],
        out_shape=[
            jax.ShapeDtypeStruct((T * D // LANES, LANES), jnp.float32),
            jax.ShapeDtypeStruct((PEER_SLOTS, T), jnp.int32),
            jax.ShapeDtypeStruct((PEER_SLOTS, T), jnp.float32),
        ],
        scratch_shapes=[pltpu.VMEM((ROW_TILE, nq), jnp.bfloat16)],
        compiler_params=pltpu.CompilerParams(vmem_limit_bytes=VMEM_LIMIT),
        name="peer_retrieve",
    )(x, g_pre.reshape(1, D), sc2, sh2, w_q.astype(jnp.bfloat16), sub_keys.astype(jnp.bfloat16))


PEER_SLOTS = PEER_HEADS * PEER_TOPK
PEER_CHUNKS = D_MODEL // 256
PEER_TB = 256
TILE_STRIDE = PEER_SLOTS + 8


PACK_ROWS = 256


def _bf16_bits(x):
    return pltpu.bitcast(x.astype(jnp.bfloat16).astype(jnp.float32), jnp.uint32)


def _pack_table_body(x_ref, o_ref):
    half = x_ref.shape[1] // 2
    for c in range(PEER_CHUNKS):
        lo = _bf16_bits(x_ref[:, c * LANES:(c + 1) * LANES])
        hi = _bf16_bits(x_ref[:, half + c * LANES:half + (c + 1) * LANES])
        o_ref[pl.ds(c, PACK_ROWS, stride=PEER_CHUNKS), :] = lax.shift_right_logical(lo, jnp.uint32(16)) | hi


def _pack_table(tab):
    E, D = tab.shape
    return pl.pallas_call(
        _pack_table_body,
        grid=(E // PACK_ROWS,),
        in_specs=[pl.BlockSpec((PACK_ROWS, D), lambda i: (i, 0))],
        out_specs=pl.BlockSpec((PACK_ROWS * PEER_CHUNKS, LANES), lambda i: (i, 0)),
        out_shape=jax.ShapeDtypeStruct((E * PEER_CHUNKS, LANES), jnp.uint32),
        name="pack_table",
    )(tab)


def _gather_token(idx_ref, tab_ref, tile_ref, t):
    for k in range(PEER_SLOTS):
        row = pl.multiple_of(idx_ref[k, t], PEER_CHUNKS)
        tile_ref[pl.ds(k, PEER_CHUNKS, stride=TILE_STRIDE), :] = tab_ref[pl.ds(row, PEER_CHUNKS), :]


def _tile_chunk_f32(tile_ref, c):
    w = tile_ref[pl.ds(c * TILE_STRIDE, PEER_SLOTS), :]
    return pltpu.bitcast(w, jnp.bfloat16).astype(jnp.float32)


def _token_pairs(idx_ref, tab_ref, tile_a, tile_b, consume):
    _gather_token(idx_ref, tab_ref, tile_a, 0)

    def pair(j, carry):
        t0 = 2 * j
        _gather_token(idx_ref, tab_ref, tile_b, t0 + 1)
        consume(tile_a, t0)
        _gather_token(idx_ref, tab_ref, tile_a, jnp.minimum(t0 + 2, PEER_TB - 1))
        consume(tile_b, t0 + 1)
        return carry
    lax.fori_loop(0, PEER_TB // 2, pair, 0)


def _token_block_cols(t):
    return pl.ds(pl.multiple_of(lax.shift_right_logical(t, 7) * LANES, LANES), LANES)


def _peer_u_body(idx_ref, h_ref, gate_ref, tab_ref, coef_ref, tile_a, tile_b, sblk_ref):
    lane = lax.broadcasted_iota(jnp.int32, (2 * PEER_SLOTS, LANES), 1)
    odd = (lax.broadcasted_iota(jnp.int32, (8, LANES), 0) & 1) == 1

    def consume(tile_ref, t):
        acc = jnp.zeros((2 * PEER_SLOTS // 8, 8, LANES), jnp.float32)
        for c in range(PEER_CHUNKS):
            h_lo = jnp.broadcast_to(h_ref[t, pl.ds(c, 1), :], (8, LANES))
            h_hi = jnp.broadcast_to(h_ref[t, pl.ds(PEER_CHUNKS + c, 1), :], (8, LANES))
            pattern = jnp.where(odd, h_hi, h_lo)
            x = _tile_chunk_f32(tile_ref, c).reshape(2 * PEER_SLOTS // 8, 8, LANES)
            acc = acc + x * pattern[None]
        s = jnp.sum(acc.reshape(2 * PEER_SLOTS, LANES), axis=1, keepdims=True)
        tl = t & (LANES - 1)
        sblk_ref[...] = jnp.where(lane == tl, s, sblk_ref[...])

        @pl.when(tl == LANES - 1)
        def _():
            even_rows = sblk_ref[pl.ds(0, PEER_SLOTS, stride=2), :]
            odd_rows = sblk_ref[pl.ds(1, PEER_SLOTS, stride=2), :]
            a = even_rows + odd_rows
            cols = _token_block_cols(t)
            coef = gate_ref[:, cols] * (0.5 * a * (1.0 + lax.erf(a * (2.0 ** -0.5))))
            sblk_ref[pl.ds(0, PEER_SLOTS, stride=2), :] = coef
            sblk_ref[pl.ds(1, PEER_SLOTS, stride=2), :] = coef
            coef_ref[:, cols] = sblk_ref[...]

    _token_pairs(idx_ref, tab_ref, tile_a, tile_b, consume)


def _peer_u_pass(experts_t, ht, gates_t, tab):
    T = ht.shape[0]
    return pl.pallas_call(
        _peer_u_body,
        grid=(T // PEER_TB,),
        in_specs=[
            pl.BlockSpec((PEER_SLOTS, PEER_TB), lambda i: (0, i), memory_space=pltpu.SMEM),
            pl.BlockSpec((PEER_TB, 8, LANES), lambda i: (i, 0, 0)),
            pl.BlockSpec((PEER_SLOTS, PEER_TB), lambda i: (0, i)),
            pl.BlockSpec(memory_space=pltpu.VMEM),
        ],
        out_specs=pl.BlockSpec((2 * PEER_SLOTS, PEER_TB), lambda i: (0, i)),
        out_shape=jax.ShapeDtypeStruct((2 * PEER_SLOTS, T), jnp.float32),
        scratch_shapes=[
            pltpu.VMEM((PEER_CHUNKS * TILE_STRIDE, LANES), jnp.uint32),
            pltpu.VMEM((PEER_CHUNKS * TILE_STRIDE, LANES), jnp.uint32),
            pltpu.VMEM((2 * PEER_SLOTS, LANES), jnp.float32),
        ],
        compiler_params=pltpu.CompilerParams(vmem_limit_bytes=VMEM_LIMIT),
        name="peer_u_pass",
    )(experts_t, ht, gates_t, tab)


def _peer_v_body(idx_ref, coef_ref, tab_ref, y_ref, tile_a, tile_b):
    lane = lax.broadcasted_iota(jnp.int32, (2 * PEER_SLOTS, LANES), 1)

    def consume(tile_ref, t):
        tl = t & (LANES - 1)
        coef_blk = coef_ref[:, _token_block_cols(t)]
        ccol = jnp.sum(jnp.where(lane == tl, coef_blk, 0.0), axis=1, keepdims=True)
        ccol = ccol.reshape(2 * PEER_SLOTS // 8, 8, 1)
        for c in range(PEER_CHUNKS):
            x = _tile_chunk_f32(tile_ref, c).reshape(2 * PEER_SLOTS // 8, 8, LANES)
            part = jnp.sum(x * ccol, axis=0)
            part = part + pltpu.roll(part, 4, axis=0)
            part = part + pltpu.roll(part, 2, axis=0)
            y_ref[t, pl.ds(c, 1), :] = part[0:1]
            y_ref[t, pl.ds(PEER_CHUNKS + c, 1), :] = part[1:2]

    _token_pairs(idx_ref, tab_ref, tile_a, tile_b, consume)


def _peer_v_pass(experts_t, coef_dup, tab):
    T = experts_t.shape[1]
    y = pl.pallas_call(
        _peer_v_body,
        grid=(T // PEER_TB,),
        in_specs=[
            pl.BlockSpec((PEER_SLOTS, PEER_TB), lambda i: (0, i), memory_space=pltpu.SMEM),
            pl.BlockSpec((2 * PEER_SLOTS, PEER_TB), lambda i: (0, i)),
            pl.BlockSpec(memory_space=pltpu.VMEM),
        ],
        out_specs=pl.BlockSpec((PEER_TB, 8, LANES), lambda i: (i, 0, 0)),
        out_shape=jax.ShapeDtypeStruct((T, 8, LANES), jnp.float32),
        scratch_shapes=[pltpu.VMEM((PEER_CHUNKS * TILE_STRIDE, LANES), jnp.uint32)] * 2,
        compiler_params=pltpu.CompilerParams(vmem_limit_bytes=VMEM_LIMIT),
        name="peer_v_pass",
    )(experts_t, coef_dup, tab)
    return y


def _post_residual_body(x_ref, y_ref, g_ref, gate_ref, o_ref):
    pieces = [y_ref[pl.ds(j, ROW_TILE, stride=8), :] for j in range(x_ref.shape[2] // LANES)]
    sq = pieces[0] * pieces[0]
    for p in pieces[1:]:
        sq = sq + p * p
    inv = lax.rsqrt(jnp.sum(sq, axis=-1, keepdims=True) * (1.0 / x_ref.shape[2]) + EPS)
    for j, p in enumerate(pieces):
        cols = slice(j * LANES, (j + 1) * LANES)
        o_ref[0, :, cols] = x_ref[0, :, cols] + gate_ref[0, :, cols] * (p * inv * g_ref[:, cols])


def _post_residual(x, y, g, gate):
    B, S, D = x.shape
    pieces = D // LANES
    spb = S // ROW_TILE
    row_spec = pl.BlockSpec((1, ROW_TILE, D), lambda b, s: (b, s, 0))
    return pl.pallas_call(
        _post_residual_body,
        grid=(B, spb),
        in_specs=[
            row_spec,
            pl.BlockSpec((ROW_TILE * pieces, LANES), lambda b, s: (b * spb + s, 0)),
            pl.BlockSpec((1, D), lambda b, s: (0, 0)),
            pl.BlockSpec((1, 1, D), lambda b, s: (b, 0, 0)),
        ],
        out_specs=row_spec,
        out_shape=jax.ShapeDtypeStruct((B, S, D), x.dtype),
        name="post_residual",
    )(x, y.reshape(B * S * pieces, LANES), g.reshape(1, D), gate)


def kernel(x, c, w_mod, b_mod, g_pre_mix, g_post_mix, w_in, w_alpha_f, b_alpha_f, w_alpha_b, b_alpha_b,
           g_gla_head, w_pool, pool_scale, w_out, g_pre_ffn, g_post_ffn, w_peer_q, peer_sub_keys,
           peer_u, peer_v):
    for l in range(w_mod.shape[0]):
        mod = _modulation(c, w_mod[l], b_mod[l])
        sh1, sc1, gt1, sh2, sc2, gt2 = [m[:, None, :] for m in jnp.split(mod, N_MOD, axis=-1)]
        z = _in_proj(x, g_pre_mix[l], sc1, sh1, w_in[l])
        o_gla = _gla(z, w_alpha_f[l], b_alpha_f[l], w_alpha_b[l], b_alpha_b[l], g_gla_head[l])
        x = _mix_out(x, o_gla, z, w_pool[l], pool_scale[l], w_out[l], g_post_mix[l], gt1)
        y = _peer_ffn(x, g_pre_ffn[l], sc2, sh2, w_peer_q[l], peer_sub_keys[l], peer_u[l], peer_v[l])
        x = _post_residual(x, y, g_post_ffn[l], gt2)
    return x
```

```python
import jax
import jax.numpy as jnp
from jax import lax
from jax.experimental import pallas as pl
from jax.experimental.pallas import tpu as pltpu

D_MODEL = 1024
GLA_WIDTH = 512
POOL_WIDTH = 512
GLA_HEADS = 4
GLA_DV = 128
GLA_DK = 64
GLA_KEY_WIDTH = 256
GLA_GATE_RANK = 16
GLA_TAU = 16.0
GLA_CHUNK = 64
POOL_WINDOWS = (2, 4, 8, 16)
POOL_GROUP_WIDTH = 128
PEER_HEADS = 8
PEER_NKEYS = 128
PEER_HALF = 128
PEER_TOPK = 16
N_MOD = 6
EPS = 1e-6

LANES = 128
VMEM_LIMIT = 48 * 1024 * 1024
ROW_TILE = 512

Z_Q, Z_K, Z_V, Z_R, Z_P, Z_A = 0, 256, 512, 1024, 1536, 2048
Z_COLS = 2176
POOL_HALO = 8
GLA_UNROLL = 8


def _mod_body(c_ref, w_ref, b_ref, o_ref):
    c = c_ref[...]
    cond = c * (1.0 / (1.0 + jnp.exp(-c)))
    o_ref[...] = jnp.dot(cond.astype(jnp.bfloat16), w_ref[...].astype(jnp.bfloat16),
                         preferred_element_type=jnp.float32) + b_ref[...]


def _modulation(c, w_mod, b_mod):
    B, D = c.shape
    n = w_mod.shape[1]
    return pl.pallas_call(
        _mod_body,
        grid=(n // D,),
        in_specs=[
            pl.BlockSpec((B, D), lambda j: (0, 0)),
            pl.BlockSpec((D, D), lambda j: (0, j)),
            pl.BlockSpec((1, D), lambda j: (0, j)),
        ],
        out_specs=pl.BlockSpec((B, D), lambda j: (0, j)),
        out_shape=jax.ShapeDtypeStruct((B, n), jnp.float32),
        name="modulation",
    )(c, w_mod, b_mod.reshape(1, n))


def _in_proj_body(x_ref, g_ref, sc_ref, sh_ref, w_ref, z_ref):
    x = x_ref[0]
    h = x * lax.rsqrt(jnp.mean(x * x, axis=-1, keepdims=True) + EPS) * g_ref[...]
    h = h * (1.0 + sc_ref[0]) + sh_ref[0]
    z_ref[0] = jnp.dot(h.astype(jnp.bfloat16), w_ref[...], preferred_element_type=jnp.float32)


def _in_proj(x, g, sc, sh, w_in):
    B, S, D = x.shape
    w = jnp.concatenate([w_in[:, :Z_P], w_in[:, Z_P + 32:], w_in[:, Z_P:Z_P + 32],
                         jnp.zeros((D, Z_COLS - Z_A - 32), w_in.dtype)], axis=1).astype(jnp.bfloat16)
    mod_spec = pl.BlockSpec((1, 1, D), lambda b, s: (b, 0, 0))
    return pl.pallas_call(
        _in_proj_body,
        grid=(B, S // ROW_TILE),
        in_specs=[
            pl.BlockSpec((1, ROW_TILE, D), lambda b, s: (b, s, 0)),
            pl.BlockSpec((1, D), lambda b, s: (0, 0)),
            mod_spec,
            mod_spec,
            pl.BlockSpec((D, Z_COLS), lambda b, s: (0, 0)),
        ],
        out_specs=pl.BlockSpec((1, ROW_TILE, Z_COLS), lambda b, s: (b, s, 0)),
        out_shape=jax.ShapeDtypeStruct((B, S, Z_COLS), jnp.float32),
        compiler_params=pltpu.CompilerParams(vmem_limit_bytes=VMEM_LIMIT),
        name="in_proj",
    )(x, g.reshape(1, D), sc, sh, w)


def _log_sigmoid(x):
    return jnp.minimum(x, 0.0) - jnp.log1p(jnp.exp(-jnp.abs(x)))


def _bf16_dot(a, b, dims):
    return lax.dot_general(a.astype(jnp.bfloat16), b.astype(jnp.bfloat16), (dims, ((), ())),
                           preferred_element_type=jnp.float32)


def _gla_body(q_ref, k_ref, v_ref, r_ref, a_ref, wf_ref, bf_ref, wb_ref, bb_ref, gh_ref, o_ref,
              of_ref, st_ref):
    C = GLA_CHUNK
    n_chunks = q_ref.shape[1] // C
    row = lax.broadcasted_iota(jnp.int32, (C, C), 0)
    col = lax.broadcasted_iota(jnp.int32, (C, C), 1)
    lane = lax.broadcasted_iota(jnp.int32, (1, LANES), 1)
    head_lanes = [lane < GLA_DK, lane >= GLA_DK]

    def chunk_group(chunk_ids, w_ref, b_ref, cum_mask, keep_mask, total_row, emit):
        rows = [pl.ds(pl.multiple_of(n * C, C), C) for n in chunk_ids]
        log_a = [_log_sigmoid(_bf16_dot(a_ref[0, r, :], w_ref[0], ((1,), (0,))) + b_ref[0]) * (1.0 / GLA_TAU)
                 for r in rows]
        b = [jnp.dot(cum_mask, la, precision=lax.Precision.HIGHEST, preferred_element_type=jnp.float32)
             for la in log_a]
        qm, k_inv, k_end, gamma = [], [], [], []
        for r, bc in zip(rows, b):
            b_tot = bc[total_row:total_row + 1, :]
            q_dec = q_ref[0, r, :] * (GLA_DK ** -0.5) * jnp.exp(bc)
            k = k_ref[0, r, :]
            qm.append([jnp.where(head_lanes[hd], q_dec, 0.0) for hd in range(2)])
            k_inv.append(k * jnp.exp(-bc))
            k_end.append(k * jnp.exp(b_tot - bc))
            gamma.append(jnp.exp(b_tot))
        v = [[v_ref[0, r, hd * GLA_DV:(hd + 1) * GLA_DV] for hd in range(2)] for r in rows]
        scores = [[jnp.where(keep_mask, _bf16_dot(qm[u][hd], k_inv[u], ((1,), (1,))), 0.0) for hd in range(2)]
                  for u in range(len(rows))]
        o_intra = [[_bf16_dot(scores[u][hd], v[u][hd], ((1,), (0,))) for hd in range(2)] for u in range(len(rows))]
        kv = [[_bf16_dot(v[u][hd], k_end[u], ((0,), (0,))) for hd in range(2)] for u in range(len(rows))]
        for u, r in enumerate(rows):
            for hd in range(2):
                state = st_ref[hd]
                emit(r, hd, o_intra[u][hd] + _bf16_dot(qm[u][hd], state, ((1,), (1,))))
                st_ref[hd] = state * gamma[u] + kv[u][hd]

    st_ref[...] = jnp.zeros_like(st_ref)
    prefix = (row >= col).astype(jnp.float32)

    def emit_fwd(rows, hd, o):
        of_ref[rows, hd * GLA_DV:(hd + 1) * GLA_DV] = o

    def fwd(i, carry):
        chunk_group([i * GLA_UNROLL + u for u in range(GLA_UNROLL)], wf_ref, bf_ref, prefix, row >= col, C - 1,
                    emit_fwd)
        return carry
    lax.fori_loop(0, n_chunks // GLA_UNROLL, fwd, 0)

    st_ref[...] = jnp.zeros_like(st_ref)
    suffix = (row <= col).astype(jnp.float32)

    def emit_bwd(rows, hd, o):
        cols = slice(hd * GLA_DV, (hd + 1) * GLA_DV)
        tot = of_ref[rows, cols] + o
        y = tot * lax.rsqrt(jnp.mean(tot * tot, axis=-1, keepdims=True) + EPS) * gh_ref[0, :, cols]
        r = r_ref[0, rows, cols]
        o_ref[0, rows, cols] = y * (r * (1.0 / (1.0 + jnp.exp(-r))))

    def bwd(i, carry):
        chunk_group([n_chunks - 1 - (i * GLA_UNROLL + u) for u in range(GLA_UNROLL)], wb_ref, bb_ref, suffix,
                    row < col, 0, emit_bwd)
        return carry
    lax.fori_loop(0, n_chunks // GLA_UNROLL, bwd, 0)


def _gla(z, w_af, b_af, w_ab, b_ab, g_head):
    B, S, _ = z.shape
    pairs = GLA_HEADS // 2

    def pad_w(w, first_row):
        wp = jnp.zeros((pairs, LANES, LANES), jnp.float32)
        return wp.at[:, first_row:first_row + GLA_GATE_RANK, :].set(
            jnp.transpose(w.reshape(GLA_GATE_RANK, pairs, LANES), (1, 0, 2)))

    def lane_blk(off):
        return pl.BlockSpec((1, S, LANES), lambda b, j: (b, 0, off // LANES + j))

    def wide_blk(off):
        return pl.BlockSpec((1, S, 2 * LANES), lambda b, j: (b, 0, off // (2 * LANES) + j))

    w_spec = pl.BlockSpec((1, LANES, LANES), lambda b, j: (j, 0, 0))
    b_spec = pl.BlockSpec((1, 1, LANES), lambda b, j: (j, 0, 0))
    return pl.pallas_call(
        _gla_body,
        grid=(B, pairs),
        in_specs=[
            lane_blk(Z_Q), lane_blk(Z_K), wide_blk(Z_V), wide_blk(Z_R),
            pl.BlockSpec((1, S, LANES), lambda b, j: (b, 0, Z_A // LANES)),
            w_spec, b_spec, w_spec, b_spec,
            pl.BlockSpec((1, 1, 2 * LANES), lambda b, j: (j, 0, 0)),
        ],
        out_specs=pl.BlockSpec((1, S, 2 * LANES), lambda b, j: (b, 0, j)),
        out_shape=jax.ShapeDtypeStruct((B, S, GLA_WIDTH), jnp.float32),
        scratch_shapes=[
            pltpu.VMEM((S, 2 * LANES), jnp.float32),
            pltpu.VMEM((2, GLA_DV, LANES), jnp.float32),
        ],
        compiler_params=pltpu.CompilerParams(vmem_limit_bytes=VMEM_LIMIT),
        name="gla",
    )(z, z, z, z, z,
      pad_w(w_af, 0), b_af.reshape(pairs, 1, LANES),
      pad_w(w_ab, GLA_GATE_RANK), b_ab.reshape(pairs, 1, LANES),
      g_head.reshape(pairs, 1, 2 * LANES))


def _mix_out_body(x_ref, o_ref, p_ref, pp_ref, pn_ref, wp_ref, ps_ref, wo_ref, g_ref, gate_ref, out_ref, pe_ref):
    s = pl.program_id(1)
    n_s = pl.num_programs(1)
    seq_len = n_s * ROW_TILE
    pe_ref[pl.ds(0, POOL_HALO), :] = jnp.where(s > 0, pp_ref[0], 0.0)
    pe_ref[pl.ds(POOL_HALO, ROW_TILE), :] = p_ref[0]
    pe_ref[pl.ds(POOL_HALO + ROW_TILE, POOL_HALO), :] = jnp.where(s < n_s - 1, pn_ref[0], 0.0)
    pos = s * ROW_TILE + lax.broadcasted_iota(jnp.int32, (ROW_TILE, POOL_GROUP_WIDTH), 0)
    mix = jnp.dot(o_ref[0].astype(jnp.bfloat16), wo_ref[pl.ds(0, GLA_WIDTH), :],
                  preferred_element_type=jnp.float32)
    for gi, w in enumerate(POOL_WINDOWS):
        cols = pl.ds(gi * POOL_GROUP_WIDTH, POOL_GROUP_WIDTH)
        total = pe_ref[pl.ds(POOL_HALO - w // 2, ROW_TILE), cols]
        for d in range(-w // 2 + 1, w // 2):
            total = total + pe_ref[pl.ds(POOL_HALO + d, ROW_TILE), cols]
        count = (jnp.minimum(pos + w // 2, seq_len) - jnp.maximum(pos - w // 2, 0)).astype(jnp.float32)
        pooled = total / count - pe_ref[pl.ds(POOL_HALO, ROW_TILE), cols]
        y = jnp.dot(pooled.astype(jnp.bfloat16), wp_ref[gi], preferred_element_type=jnp.float32)
        y = y * ps_ref[:, gi * POOL_GROUP_WIDTH:(gi + 1) * POOL_GROUP_WIDTH]
        mix = mix + jnp.dot(y.astype(jnp.bfloat16),
                            wo_ref[pl.ds(GLA_WIDTH + gi * POOL_GROUP_WIDTH, POOL_GROUP_WIDTH), :],
                            preferred_element_type=jnp.float32)
    yn = mix * lax.rsqrt(jnp.mean(mix * mix, axis=-1, keepdims=True) + EPS) * g_ref[...]
    out_ref[0] = x_ref[0] + gate_ref[0] * yn


def _mix_out(x, o_gla, z, w_pool, pool_scale, w_out, g, gate):
    B, S, D = x.shape
    halo_per_tile = ROW_TILE // POOL_HALO
    last_halo = S // POOL_HALO - 1
    row_spec = pl.BlockSpec((1, ROW_TILE, D), lambda b, s: (b, s, 0))
    return pl.pallas_call(
        _mix_out_body,
        grid=(B, S // ROW_TILE),
        in_specs=[
            row_spec,
            pl.BlockSpec((1, ROW_TILE, GLA_WIDTH), lambda b, s: (b, s, 0)),
            pl.BlockSpec((1, ROW_TILE, POOL_WIDTH), lambda b, s: (b, s, Z_P // POOL_WIDTH)),
            pl.BlockSpec((1, POOL_HALO, POOL_WIDTH),
                         lambda b, s: (b, jnp.maximum(s * halo_per_tile - 1, 0), Z_P // POOL_WIDTH)),
            pl.BlockSpec((1, POOL_HALO, POOL_WIDTH),
                         lambda b, s: (b, jnp.minimum((s + 1) * halo_per_tile, last_halo), Z_P // POOL_WIDTH)),
            pl.BlockSpec(w_pool.shape, lambda b, s: (0, 0, 0)),
            pl.BlockSpec((1, POOL_WIDTH), lambda b, s: (0, 0)),
            pl.BlockSpec((D, D), lambda b, s: (0, 0)),
            pl.BlockSpec((1, D), lambda b, s: (0, 0)),
            pl.BlockSpec((1, 1, D), lambda b, s: (b, 0, 0)),
        ],
        out_specs=row_spec,
        out_shape=jax.ShapeDtypeStruct((B, S, D), jnp.float32),
        scratch_shapes=[pltpu.VMEM((ROW_TILE + 2 * POOL_HALO, POOL_WIDTH), jnp.float32)],
        compiler_params=pltpu.CompilerParams(vmem_limit_bytes=VMEM_LIMIT),
        name="mix_out",
    )(x, o_gla, z, z, z, w_pool.astype(jnp.bfloat16), pool_scale.reshape(1, POOL_WIDTH),
      w_out.astype(jnp.bfloat16), g.reshape(1, D), gate)


def _peer_ffn(x, g_pre, sc2, sh2, w_q, sub_keys, u_tab, v_tab):
    B, S, D = x.shape
    h2, offsets_t, gates_t = _peer_retrieve(x, g_pre, sc2, sh2, w_q, sub_keys)
    row_offsets = offsets_t.T
    coef_dup = _peer_u_pass(row_offsets, h2.reshape(B * S, D // LANES, LANES), gates_t, _pack_table(u_tab))
    return _peer_v_pass(row_offsets, coef_dup, _pack_table(v_tab))


NEG_INF = float("-inf")
RETR_TOKENS = 256


def _top16_rows(s, row_id):
    vals, ids = [], []
    for _ in range(PEER_TOPK):
        m = jnp.max(s, axis=0, keepdims=True)
        r = jnp.min(jnp.where(s == m, row_id, jnp.iinfo(jnp.int32).max), axis=0, keepdims=True)
        vals.append(m)
        ids.append(r)
        s = jnp.where(row_id == r, NEG_INF, s)
    return vals, ids


def _peer_retrieve_body(x_ref, g_ref, sc_ref, sh_ref, wq_ref, keys_ref, h_ref, idx_ref, gate_ref, q_ref):
    x = x_ref[0]
    h = x * lax.rsqrt(jnp.mean(x * x, axis=-1, keepdims=True) + EPS) * g_ref[...]
    h = h * (1.0 + sc_ref[0]) + sh_ref[0]
    for j in range(h.shape[1] // LANES):
        h_ref[pl.ds(j, ROW_TILE, stride=8), :] = h[:, j * LANES:(j + 1) * LANES]
    q_ref[...] = jnp.dot(h.astype(jnp.bfloat16), wq_ref[...],
                         preferred_element_type=jnp.float32).astype(jnp.bfloat16)
    key_iota = lax.broadcasted_iota(jnp.int32, (PEER_NKEYS, RETR_TOKENS), 0)
    sub8 = lax.broadcasted_iota(jnp.int32, (8, RETR_TOKENS), 0)
    cand_id = jnp.concatenate([sub8, sub8 + 8] + [sub8 + a * PEER_TOPK for a in range(1, 8)]
                              + [(sub8 + 8) * PEER_TOPK], axis=0)
    chunks = ROW_TILE // RETR_TOKENS

    def head_chunk(i, carry):
        hd = lax.shift_right_logical(i, chunks.bit_length() - 1)
        ch = i & (chunks - 1)
        tok = pl.ds(pl.multiple_of(ch * RETR_TOKENS, RETR_TOKENS), RETR_TOKENS)
        vals, rows = [], []
        for p in range(2):
            col = pl.multiple_of((hd * 2 + p) * PEER_HALF, PEER_HALF)
            qs = q_ref[tok, pl.ds(col, PEER_HALF)]
            s = lax.dot_general(keys_ref[hd, p], qs, (((1,), (1,)), ((), ())),
                                preferred_element_type=jnp.float32)
            v, r = _top16_rows(s, key_iota)
            vals.append(v)
            rows.append(r)
        v1_hi = jnp.concatenate(vals[0][8:], axis=0)
        v2_lo = jnp.concatenate(vals[1][:8], axis=0)
        v2_hi = jnp.concatenate(vals[1][8:], axis=0)
        cand = jnp.concatenate([vals[0][0] + v2_lo, vals[0][0] + v2_hi]
                               + [vals[0][a] + v2_lo for a in range(1, 8)]
                               + [v1_hi + vals[1][0]], axis=0)
        top_s, pos = _top16_rows(cand, cand_id)
        pos = jnp.concatenate(pos, axis=0)
        pa = pos >> 4
        pb = pos & (PEER_TOPK - 1)
        i1 = jnp.zeros_like(pos)
        i2 = jnp.zeros_like(pos)
        for a in range(PEER_TOPK):
            i1 = jnp.where(pa == a, rows[0][a], i1)
            i2 = jnp.where(pb == a, rows[1][a], i2)
        ts = jnp.concatenate(top_s, axis=0)
        e = jnp.exp(ts - top_s[0])
        gates = e / jnp.sum(e, axis=0, keepdims=True)
        out_rows = pl.ds(pl.multiple_of(hd * PEER_TOPK, PEER_TOPK), PEER_TOPK)
        idx_ref[out_rows, tok] = (i1 * PEER_NKEYS + i2) * PEER_CHUNKS
        gate_ref[out_rows, tok] = gates
        return carry

    lax.fori_loop(0, PEER_HEADS * chunks, head_chunk, 0)


def _peer_retrieve(x, g_pre, sc2, sh2, w_q, sub_keys):
    B, S, D = x.shape
    T = B * S
    nq = w_q.shape[1]
    spb = S // ROW_TILE
    tok_spec = pl.BlockSpec((1, ROW_TILE, D), lambda b, s: (b, s, 0))
    mod_spec = pl.BlockSpec((1, 1, D), lambda b, s: (b, 0, 0))
    slot_spec = pl.BlockSpec((PEER_SLOTS, ROW_TILE), lambda b, s: (0, b * spb + s))
    return pl.pallas_call(
        _peer_retrieve_body,
        grid=(B, spb),
        in_specs=[
            tok_spec,
            pl.BlockSpec((1, D), lambda b, s: (0, 0)),
            mod_spec,
            mod_spec,
            pl.BlockSpec((D, nq), lambda b, s: (0, 0)),
            pl.BlockSpec(sub_keys.shape, lambda b, s: (0, 0, 0, 0)),
        ],
        out_specs=[pl.BlockSpec((ROW_TILE * D // LANES, LANES), lambda b, s: (b * spb + s, 0)), slot_spec, slot_spec],
        out_shape=[
            jax.ShapeDtypeStruct((T * D // LANES, LANES), jnp.float32),
            jax.ShapeDtypeStruct((PEER_SLOTS, T), jnp.int32),
            jax.ShapeDtypeStruct((PEER_SLOTS, T), jnp.float32),
        ],
        scratch_shapes=[pltpu.VMEM((ROW_TILE, nq), jnp.bfloat16)],
        compiler_params=pltpu.CompilerParams(vmem_limit_bytes=VMEM_LIMIT),
        name="peer_retrieve",
    )(x, g_pre.reshape(1, D), sc2, sh2, w_q.astype(jnp.bfloat16), sub_keys.astype(jnp.bfloat16))


PEER_SLOTS = PEER_HEADS * PEER_TOPK
PEER_CHUNKS = D_MODEL // 256
PEER_TB = 256
TILE_STRIDE = PEER_SLOTS + 8


PACK_ROWS = 256


def _bf16_bits(x):
    return pltpu.bitcast(x.astype(jnp.bfloat16).astype(jnp.float32), jnp.uint32)


def _pack_table_body(x_ref, o_ref):
    half = x_ref.shape[1] // 2
    for c in range(PEER_CHUNKS):
        lo = _bf16_bits(x_ref[:, c * LANES:(c + 1) * LANES])
        hi = _bf16_bits(x_ref[:, half + c * LANES:half + (c + 1) * LANES])
        o_ref[pl.ds(c, PACK_ROWS, stride=PEER_CHUNKS), :] = lax.shift_right_logical(lo, jnp.uint32(16)) | hi


def _pack_table(tab):
    E, D = tab.shape
    return pl.pallas_call(
        _pack_table_body,
        grid=(E // PACK_ROWS,),
        in_specs=[pl.BlockSpec((PACK_ROWS, D), lambda i: (i, 0))],
        out_specs=pl.BlockSpec((PACK_ROWS * PEER_CHUNKS, LANES), lambda i: (i, 0)),
        out_shape=jax.ShapeDtypeStruct((E * PEER_CHUNKS, LANES), jnp.uint32),
        name="pack_table",
    )(tab)


IDX_SUB = 8
N_SUB = PEER_TB // IDX_SUB


def _gather_token(stage_ref, half, i, tab_ref, tile_ref):
    for k in range(PEER_SLOTS):
        row = pl.multiple_of(stage_ref[half, i, k], PEER_CHUNKS)
        tile_ref[pl.ds(k, PEER_CHUNKS, stride=TILE_STRIDE), :] = tab_ref[pl.ds(row, PEER_CHUNKS), :]


def _tile_chunk_f32(tile_ref, c):
    w = tile_ref[pl.ds(c * TILE_STRIDE, PEER_SLOTS), :]
    return pltpu.bitcast(w, jnp.bfloat16).astype(jnp.float32)


def _token_stream(idx_ref, tab_ref, stage_ref, sems, tiles, consume):
    def stage_copy(sub, half):
        first = pl.multiple_of(jnp.minimum(sub, N_SUB - 1) * IDX_SUB, IDX_SUB)
        return pltpu.make_async_copy(idx_ref.at[pl.ds(first, IDX_SUB), :], stage_ref.at[half], sems.at[half])

    stage_copy(0, 0).start()
    stage_copy(1, 1).start()
    stage_copy(0, 0).wait()
    _gather_token(stage_ref, 0, 0, tab_ref, tiles[0])

    def two_subs(j, carry):
        for half in range(2):
            sub = 2 * j + half
            for i in range(IDX_SUB):
                if i + 1 < IDX_SUB:
                    _gather_token(stage_ref, half, i + 1, tab_ref, tiles[(i + 1) % 2])
                else:
                    stage_copy(sub + 1, 1 - half).wait()
                    stage_copy(sub + 2, half).start()
                    _gather_token(stage_ref, 1 - half, 0, tab_ref, tiles[(i + 1) % 2])
                consume(tiles[i % 2], sub * IDX_SUB + i, half == 1 and i == IDX_SUB - 1)
        return carry
    lax.fori_loop(0, N_SUB // 2, two_subs, 0)
    stage_copy(N_SUB + 1, 1).wait()


def _token_block_cols(t):
    return pl.ds(pl.multiple_of(lax.shift_right_logical(t, 7) * LANES, LANES), LANES)


def _stage_scratch():
    return [
        pltpu.SMEM((2, IDX_SUB, PEER_SLOTS), jnp.int32),
        pltpu.SemaphoreType.DMA((2,)),
        pltpu.VMEM((PEER_CHUNKS * TILE_STRIDE, LANES), jnp.uint32),
        pltpu.VMEM((PEER_CHUNKS * TILE_STRIDE, LANES), jnp.uint32),
    ]


def _peer_u_body(idx_ref, h_ref, gate_ref, tab_ref, coef_ref, stage_ref, sems, tile_a, tile_b, sblk_ref):
    lane = lax.broadcasted_iota(jnp.int32, (2 * PEER_SLOTS, LANES), 1)
    odd = (lax.broadcasted_iota(jnp.int32, (8, LANES), 0) & 1) == 1

    def consume(tile_ref, t, may_end_block):
        acc = jnp.zeros((2 * PEER_SLOTS // 8, 8, LANES), jnp.float32)
        for c in range(PEER_CHUNKS):
            h_lo = jnp.broadcast_to(h_ref[t, pl.ds(c, 1), :], (8, LANES))
            h_hi = jnp.broadcast_to(h_ref[t, pl.ds(PEER_CHUNKS + c, 1), :], (8, LANES))
            pattern = jnp.where(odd, h_hi, h_lo)
            x = _tile_chunk_f32(tile_ref, c).reshape(2 * PEER_SLOTS // 8, 8, LANES)
            acc = acc + x * pattern[None]
        s = jnp.sum(acc.reshape(2 * PEER_SLOTS, LANES), axis=1, keepdims=True)
        tl = t & (LANES - 1)
        sblk_ref[...] = jnp.where(lane == tl, s, sblk_ref[...])
        if not may_end_block:
            return

        @pl.when(tl == LANES - 1)
        def _():
            even_rows = sblk_ref[pl.ds(0, PEER_SLOTS, stride=2), :]
            odd_rows = sblk_ref[pl.ds(1, PEER_SLOTS, stride=2), :]
            a = even_rows + odd_rows
            cols = _token_block_cols(t)
            coef = gate_ref[:, cols] * (0.5 * a * (1.0 + lax.erf(a * (2.0 ** -0.5))))
            sblk_ref[pl.ds(0, PEER_SLOTS, stride=2), :] = coef
            sblk_ref[pl.ds(1, PEER_SLOTS, stride=2), :] = coef
            coef_ref[:, cols] = sblk_ref[...]

    _token_stream(idx_ref, tab_ref, stage_ref, sems, (tile_a, tile_b), consume)


def _peer_u_pass(row_offsets, ht, gates_t, tab):
    T = ht.shape[0]
    return pl.pallas_call(
        _peer_u_body,
        grid=(T // PEER_TB,),
        in_specs=[
            pl.BlockSpec((PEER_TB, PEER_SLOTS), lambda i: (i, 0)),
            pl.BlockSpec((PEER_TB, 8, LANES), lambda i: (i, 0, 0)),
            pl.BlockSpec((PEER_SLOTS, PEER_TB), lambda i: (0, i)),
            pl.BlockSpec(memory_space=pltpu.VMEM),
        ],
        out_specs=pl.BlockSpec((2 * PEER_SLOTS, PEER_TB), lambda i: (0, i)),
        out_shape=jax.ShapeDtypeStruct((2 * PEER_SLOTS, T), jnp.float32),
        scratch_shapes=_stage_scratch() + [pltpu.VMEM((2 * PEER_SLOTS, LANES), jnp.float32)],
        compiler_params=pltpu.CompilerParams(vmem_limit_bytes=VMEM_LIMIT),
        name="peer_u_pass",
    )(row_offsets, ht, gates_t, tab)


def _peer_v_body(idx_ref, coef_ref, tab_ref, y_ref, stage_ref, sems, tile_a, tile_b):
    lane = lax.broadcasted_iota(jnp.int32, (2 * PEER_SLOTS, LANES), 1)

    def consume(tile_ref, t, may_end_block):
        del may_end_block
        tl = t & (LANES - 1)
        coef_blk = coef_ref[:, _token_block_cols(t)]
        ccol = jnp.sum(jnp.where(lane == tl, coef_blk, 0.0), axis=1, keepdims=True)
        ccol = ccol.reshape(2 * PEER_SLOTS // 8, 8, 1)
        for c in range(PEER_CHUNKS):
            x = _tile_chunk_f32(tile_ref, c).reshape(2 * PEER_SLOTS // 8, 8, LANES)
            part = jnp.sum(x * ccol, axis=0)
            part = part + pltpu.roll(part, 4, axis=0)
            part = part + pltpu.roll(part, 2, axis=0)
            y_ref[t, pl.ds(c, 1), :] = part[0:1]
            y_ref[t, pl.ds(PEER_CHUNKS + c, 1), :] = part[1:2]

    _token_stream(idx_ref, tab_ref, stage_ref, sems, (tile_a, tile_b), consume)


def _peer_v_pass(row_offsets, coef_dup, tab):
    T = row_offsets.shape[0]
    return pl.pallas_call(
        _peer_v_body,
        grid=(T // PEER_TB,),
        in_specs=[
            pl.BlockSpec((PEER_TB, PEER_SLOTS), lambda i: (i, 0)),
            pl.BlockSpec((2 * PEER_SLOTS, PEER_TB), lambda i: (0, i)),
            pl.BlockSpec(memory_space=pltpu.VMEM),
        ],
        out_specs=pl.BlockSpec((PEER_TB, 8, LANES), lambda i: (i, 0, 0)),
        out_shape=jax.ShapeDtypeStruct((T, 8, LANES), jnp.float32),
        scratch_shapes=_stage_scratch(),
        compiler_params=pltpu.CompilerParams(vmem_limit_bytes=VMEM_LIMIT),
        name="peer_v_pass",
    )(row_offsets, coef_dup, tab)


def _post_residual_body(x_ref, y_ref, g_ref, gate_ref, o_ref):
    pieces = [y_ref[pl.ds(j, ROW_TILE, stride=8), :] for j in range(x_ref.shape[2] // LANES)]
    sq = pieces[0] * pieces[0]
    for p in pieces[1:]:
        sq = sq + p * p
    inv = lax.rsqrt(jnp.sum(sq, axis=-1, keepdims=True) * (1.0 / x_ref.shape[2]) + EPS)
    for j, p in enumerate(pieces):
        cols = slice(j * LANES, (j + 1) * LANES)
        o_ref[0, :, cols] = x_ref[0, :, cols] + gate_ref[0, :, cols] * (p * inv * g_ref[:, cols])


def _post_residual(x, y, g, gate):
    B, S, D = x.shape
    pieces = D // LANES
    spb = S // ROW_TILE
    row_spec = pl.BlockSpec((1, ROW_TILE, D), lambda b, s: (b, s, 0))
    return pl.pallas_call(
        _post_residual_body,
        grid=(B, spb),
        in_specs=[
            row_spec,
            pl.BlockSpec((ROW_TILE * pieces, LANES), lambda b, s: (b * spb + s, 0)),
            pl.BlockSpec((1, D), lambda b, s: (0, 0)),
            pl.BlockSpec((1, 1, D), lambda b, s: (b, 0, 0)),
        ],
        out_specs=row_spec,
        out_shape=jax.ShapeDtypeStruct((B, S, D), x.dtype),
        name="post_residual",
    )(x, y.reshape(B * S * pieces, LANES), g.reshape(1, D), gate)


def kernel(x, c, w_mod, b_mod, g_pre_mix, g_post_mix, w_in, w_alpha_f, b_alpha_f, w_alpha_b, b_alpha_b,
           g_gla_head, w_pool, pool_scale, w_out, g_pre_ffn, g_post_ffn, w_peer_q, peer_sub_keys,
           peer_u, peer_v):
    for l in range(w_mod.shape[0]):
        mod = _modulation(c, w_mod[l], b_mod[l])
        sh1, sc1, gt1, sh2, sc2, gt2 = [m[:, None, :] for m in jnp.split(mod, N_MOD, axis=-1)]
        z = _in_proj(x, g_pre_mix[l], sc1, sh1, w_in[l])
        o_gla = _gla(z, w_alpha_f[l], b_alpha_f[l], w_alpha_b[l], b_alpha_b[l], g_gla_head[l])
        x = _mix_out(x, o_gla, z, w_pool[l], pool_scale[l], w_out[l], g_post_mix[l], gt1)
        y = _peer_ffn(x, g_pre_ffn[l], sc2, sh2, w_peer_q[l], peer_sub_keys[l], peer_u[l], peer_v[l])
        x = _post_residual(x, y, g_post_ffn[l], gt2)
    return x
```

```python
import jax
import jax.numpy as jnp
from jax import lax
from jax.experimental import pallas as pl
from jax.experimental.pallas import tpu as pltpu

D_MODEL = 1024
GLA_WIDTH = 512
POOL_WIDTH = 512
GLA_HEADS = 4
GLA_DV = 128
GLA_DK = 64
GLA_KEY_WIDTH = 256
GLA_GATE_RANK = 16
GLA_TAU = 16.0
GLA_CHUNK = 64
POOL_WINDOWS = (2, 4, 8, 16)
POOL_GROUP_WIDTH = 128
PEER_HEADS = 8
PEER_NKEYS = 128
PEER_HALF = 128
PEER_TOPK = 16
N_MOD = 6
EPS = 1e-6

LANES = 128
VMEM_LIMIT = 48 * 1024 * 1024
ROW_TILE = 512

Z_Q, Z_K, Z_V, Z_R, Z_P, Z_A = 0, 256, 512, 1024, 1536, 2048
Z_COLS = 2176
POOL_HALO = 8
GLA_UNROLL = 8


def _mod_body(c_ref, w_ref, b_ref, o_ref):
    c = c_ref[...]
    cond = c * (1.0 / (1.0 + jnp.exp(-c)))
    o_ref[...] = jnp.dot(cond.astype(jnp.bfloat16), w_ref[...].astype(jnp.bfloat16),
                         preferred_element_type=jnp.float32) + b_ref[...]


def _modulation(c, w_mod, b_mod):
    B, D = c.shape
    n = w_mod.shape[1]
    return pl.pallas_call(
        _mod_body,
        grid=(n // D,),
        in_specs=[
            pl.BlockSpec((B, D), lambda j: (0, 0)),
            pl.BlockSpec((D, D), lambda j: (0, j)),
            pl.BlockSpec((1, D), lambda j: (0, j)),
        ],
        out_specs=pl.BlockSpec((B, D), lambda j: (0, j)),
        out_shape=jax.ShapeDtypeStruct((B, n), jnp.float32),
        name="modulation",
    )(c, w_mod, b_mod.reshape(1, n))


def _in_proj_body(x_ref, g_ref, sc_ref, sh_ref, w_ref, z_ref):
    x = x_ref[0]
    h = x * lax.rsqrt(jnp.mean(x * x, axis=-1, keepdims=True) + EPS) * g_ref[...]
    h = h * (1.0 + sc_ref[0]) + sh_ref[0]
    z_ref[0] = jnp.dot(h.astype(jnp.bfloat16), w_ref[...], preferred_element_type=jnp.float32)


def _in_proj(x, g, sc, sh, w_in):
    B, S, D = x.shape
    w = jnp.concatenate([w_in[:, :Z_P], w_in[:, Z_P + 32:], w_in[:, Z_P:Z_P + 32],
                         jnp.zeros((D, Z_COLS - Z_A - 32), w_in.dtype)], axis=1).astype(jnp.bfloat16)
    mod_spec = pl.BlockSpec((1, 1, D), lambda b, s: (b, 0, 0))
    return pl.pallas_call(
        _in_proj_body,
        grid=(B, S // ROW_TILE),
        in_specs=[
            pl.BlockSpec((1, ROW_TILE, D), lambda b, s: (b, s, 0)),
            pl.BlockSpec((1, D), lambda b, s: (0, 0)),
            mod_spec,
            mod_spec,
            pl.BlockSpec((D, Z_COLS), lambda b, s: (0, 0)),
        ],
        out_specs=pl.BlockSpec((1, ROW_TILE, Z_COLS), lambda b, s: (b, s, 0)),
        out_shape=jax.ShapeDtypeStruct((B, S, Z_COLS), jnp.float32),
        compiler_params=pltpu.CompilerParams(vmem_limit_bytes=VMEM_LIMIT),
        name="in_proj",
    )(x, g.reshape(1, D), sc, sh, w)


def _log_sigmoid(x):
    return jnp.minimum(x, 0.0) - jnp.log1p(jnp.exp(-jnp.abs(x)))


def _bf16_dot(a, b, dims):
    return lax.dot_general(a.astype(jnp.bfloat16), b.astype(jnp.bfloat16), (dims, ((), ())),
                           preferred_element_type=jnp.float32)


def _gla_body(q_ref, k_ref, v_ref, r_ref, a_ref, wf_ref, bf_ref, wb_ref, bb_ref, gh_ref, o_ref,
              of_ref, st_ref):
    C = GLA_CHUNK
    n_chunks = q_ref.shape[1] // C
    row = lax.broadcasted_iota(jnp.int32, (C, C), 0)
    col = lax.broadcasted_iota(jnp.int32, (C, C), 1)
    lane = lax.broadcasted_iota(jnp.int32, (1, LANES), 1)
    head_lanes = [lane < GLA_DK, lane >= GLA_DK]

    def chunk_group(chunk_ids, w_ref, b_ref, cum_mask, keep_mask, total_row, emit):
        rows = [pl.ds(pl.multiple_of(n * C, C), C) for n in chunk_ids]
        log_a = [_log_sigmoid(_bf16_dot(a_ref[0, r, :], w_ref[0], ((1,), (0,))) + b_ref[0]) * (1.0 / GLA_TAU)
                 for r in rows]
        b = [jnp.dot(cum_mask, la, precision=lax.Precision.HIGHEST, preferred_element_type=jnp.float32)
             for la in log_a]
        qm, k_inv, k_end, gamma = [], [], [], []
        for r, bc in zip(rows, b):
            b_tot = bc[total_row:total_row + 1, :]
            q_dec = q_ref[0, r, :] * (GLA_DK ** -0.5) * jnp.exp(bc)
            k = k_ref[0, r, :]
            qm.append([jnp.where(head_lanes[hd], q_dec, 0.0) for hd in range(2)])
            k_inv.append(k * jnp.exp(-bc))
            k_end.append(k * jnp.exp(b_tot - bc))
            gamma.append(jnp.exp(b_tot))
        v = [[v_ref[0, r, hd * GLA_DV:(hd + 1) * GLA_DV] for hd in range(2)] for r in rows]
        scores = [[jnp.where(keep_mask, _bf16_dot(qm[u][hd], k_inv[u], ((1,), (1,))), 0.0) for hd in range(2)]
                  for u in range(len(rows))]
        o_intra = [[_bf16_dot(scores[u][hd], v[u][hd], ((1,), (0,))) for hd in range(2)] for u in range(len(rows))]
        kv = [[_bf16_dot(v[u][hd], k_end[u], ((0,), (0,))) for hd in range(2)] for u in range(len(rows))]
        for u, r in enumerate(rows):
            for hd in range(2):
                state = st_ref[hd]
                emit(r, hd, o_intra[u][hd] + _bf16_dot(qm[u][hd], state, ((1,), (1,))))
                st_ref[hd] = state * gamma[u] + kv[u][hd]

    st_ref[...] = jnp.zeros_like(st_ref)
    prefix = (row >= col).astype(jnp.float32)

    def emit_fwd(rows, hd, o):
        of_ref[rows, hd * GLA_DV:(hd + 1) * GLA_DV] = o

    def fwd(i, carry):
        chunk_group([i * GLA_UNROLL + u for u in range(GLA_UNROLL)], wf_ref, bf_ref, prefix, row >= col, C - 1,
                    emit_fwd)
        return carry
    lax.fori_loop(0, n_chunks // GLA_UNROLL, fwd, 0)

    st_ref[...] = jnp.zeros_like(st_ref)
    suffix = (row <= col).astype(jnp.float32)

    def emit_bwd(rows, hd, o):
        cols = slice(hd * GLA_DV, (hd + 1) * GLA_DV)
        tot = of_ref[rows, cols] + o
        y = tot * lax.rsqrt(jnp.mean(tot * tot, axis=-1, keepdims=True) + EPS) * gh_ref[0, :, cols]
        r = r_ref[0, rows, cols]
        o_ref[0, rows, cols] = y * (r * (1.0 / (1.0 + jnp.exp(-r))))

    def bwd(i, carry):
        chunk_group([n_chunks - 1 - (i * GLA_UNROLL + u) for u in range(GLA_UNROLL)], wb_ref, bb_ref, suffix,
                    row < col, 0, emit_bwd)
        return carry
    lax.fori_loop(0, n_chunks // GLA_UNROLL, bwd, 0)


def _gla(z, w_af, b_af, w_ab, b_ab, g_head):
    B, S, _ = z.shape
    pairs = GLA_HEADS // 2

    def pad_w(w, first_row):
        wp = jnp.zeros((pairs, LANES, LANES), jnp.float32)
        return wp.at[:, first_row:first_row + GLA_GATE_RANK, :].set(
            jnp.transpose(w.reshape(GLA_GATE_RANK, pairs, LANES), (1, 0, 2)))

    def lane_blk(off):
        return pl.BlockSpec((1, S, LANES), lambda b, j: (b, 0, off // LANES + j))

    def wide_blk(off):
        return pl.BlockSpec((1, S, 2 * LANES), lambda b, j: (b, 0, off // (2 * LANES) + j))

    w_spec = pl.BlockSpec((1, LANES, LANES), lambda b, j: (j, 0, 0))
    b_spec = pl.BlockSpec((1, 1, LANES), lambda b, j: (j, 0, 0))
    return pl.pallas_call(
        _gla_body,
        grid=(B, pairs),
        in_specs=[
            lane_blk(Z_Q), lane_blk(Z_K), wide_blk(Z_V), wide_blk(Z_R),
            pl.BlockSpec((1, S, LANES), lambda b, j: (b, 0, Z_A // LANES)),
            w_spec, b_spec, w_spec, b_spec,
            pl.BlockSpec((1, 1, 2 * LANES), lambda b, j: (j, 0, 0)),
        ],
        out_specs=pl.BlockSpec((1, S, 2 * LANES), lambda b, j: (b, 0, j)),
        out_shape=jax.ShapeDtypeStruct((B, S, GLA_WIDTH), jnp.float32),
        scratch_shapes=[
            pltpu.VMEM((S, 2 * LANES), jnp.float32),
            pltpu.VMEM((2, GLA_DV, LANES), jnp.float32),
        ],
        compiler_params=pltpu.CompilerParams(vmem_limit_bytes=VMEM_LIMIT),
        name="gla",
    )(z, z, z, z, z,
      pad_w(w_af, 0), b_af.reshape(pairs, 1, LANES),
      pad_w(w_ab, GLA_GATE_RANK), b_ab.reshape(pairs, 1, LANES),
      g_head.reshape(pairs, 1, 2 * LANES))


def _mix_out_body(x_ref, o_ref, p_ref, pp_ref, pn_ref, wp_ref, ps_ref, wo_ref, g_ref, gate_ref, out_ref, pe_ref):
    s = pl.program_id(1)
    n_s = pl.num_programs(1)
    seq_len = n_s * ROW_TILE
    pe_ref[pl.ds(0, POOL_HALO), :] = jnp.where(s > 0, pp_ref[0], 0.0)
    pe_ref[pl.ds(POOL_HALO, ROW_TILE), :] = p_ref[0]
    pe_ref[pl.ds(POOL_HALO + ROW_TILE, POOL_HALO), :] = jnp.where(s < n_s - 1, pn_ref[0], 0.0)
    pos = s * ROW_TILE + lax.broadcasted_iota(jnp.int32, (ROW_TILE, POOL_GROUP_WIDTH), 0)
    mix = jnp.dot(o_ref[0].astype(jnp.bfloat16), wo_ref[pl.ds(0, GLA_WIDTH), :],
                  preferred_element_type=jnp.float32)
    for gi, w in enumerate(POOL_WINDOWS):
        cols = pl.ds(gi * POOL_GROUP_WIDTH, POOL_GROUP_WIDTH)
        total = pe_ref[pl.ds(POOL_HALO - w // 2, ROW_TILE), cols]
        for d in range(-w // 2 + 1, w // 2):
            total = total + pe_ref[pl.ds(POOL_HALO + d, ROW_TILE), cols]
        count = (jnp.minimum(pos + w // 2, seq_len) - jnp.maximum(pos - w // 2, 0)).astype(jnp.float32)
        pooled = total / count - pe_ref[pl.ds(POOL_HALO, ROW_TILE), cols]
        y = jnp.dot(pooled.astype(jnp.bfloat16), wp_ref[gi], preferred_element_type=jnp.float32)
        y = y * ps_ref[:, gi * POOL_GROUP_WIDTH:(gi + 1) * POOL_GROUP_WIDTH]
        mix = mix + jnp.dot(y.astype(jnp.bfloat16),
                            wo_ref[pl.ds(GLA_WIDTH + gi * POOL_GROUP_WIDTH, POOL_GROUP_WIDTH), :],
                            preferred_element_type=jnp.float32)
    yn = mix * lax.rsqrt(jnp.mean(mix * mix, axis=-1, keepdims=True) + EPS) * g_ref[...]
    out_ref[0] = x_ref[0] + gate_ref[0] * yn


def _mix_out(x, o_gla, z, w_pool, pool_scale, w_out, g, gate):
    B, S, D = x.shape
    halo_per_tile = ROW_TILE // POOL_HALO
    last_halo = S // POOL_HALO - 1
    row_spec = pl.BlockSpec((1, ROW_TILE, D), lambda b, s: (b, s, 0))
    return pl.pallas_call(
        _mix_out_body,
        grid=(B, S // ROW_TILE),
        in_specs=[
            row_spec,
            pl.BlockSpec((1, ROW_TILE, GLA_WIDTH), lambda b, s: (b, s, 0)),
            pl.BlockSpec((1, ROW_TILE, POOL_WIDTH), lambda b, s: (b, s, Z_P // POOL_WIDTH)),
            pl.BlockSpec((1, POOL_HALO, POOL_WIDTH),
                         lambda b, s: (b, jnp.maximum(s * halo_per_tile - 1, 0), Z_P // POOL_WIDTH)),
            pl.BlockSpec((1, POOL_HALO, POOL_WIDTH),
                         lambda b, s: (b, jnp.minimum((s + 1) * halo_per_tile, last_halo), Z_P // POOL_WIDTH)),
            pl.BlockSpec(w_pool.shape, lambda b, s: (0, 0, 0)),
            pl.BlockSpec((1, POOL_WIDTH), lambda b, s: (0, 0)),
            pl.BlockSpec((D, D), lambda b, s: (0, 0)),
            pl.BlockSpec((1, D), lambda b, s: (0, 0)),
            pl.BlockSpec((1, 1, D), lambda b, s: (b, 0, 0)),
        ],
        out_specs=row_spec,
        out_shape=jax.ShapeDtypeStruct((B, S, D), jnp.float32),
        scratch_shapes=[pltpu.VMEM((ROW_TILE + 2 * POOL_HALO, POOL_WIDTH), jnp.float32)],
        compiler_params=pltpu.CompilerParams(vmem_limit_bytes=VMEM_LIMIT),
        name="mix_out",
    )(x, o_gla, z, z, z, w_pool.astype(jnp.bfloat16), pool_scale.reshape(1, POOL_WIDTH),
      w_out.astype(jnp.bfloat16), g.reshape(1, D), gate)


def _peer_ffn(x, g_pre, sc2, sh2, w_q, sub_keys, u_tab, v_tab):
    B, S, D = x.shape
    h2, offsets_t, gates_t = _peer_retrieve(x, g_pre, sc2, sh2, w_q, sub_keys)
    row_offsets = offsets_t.T
    coef_dup = _peer_u_pass(row_offsets, h2.reshape(B * S, D // LANES, LANES), gates_t, _pack_table(u_tab))
    return _peer_v_pass(row_offsets, coef_dup, _pack_table(v_tab))


NEG_INF = float("-inf")
RETR_TOKENS = 256


def _top16_rows(s, row_id):
    vals, ids = [], []
    for _ in range(PEER_TOPK):
        m = jnp.max(s, axis=0, keepdims=True)
        r = jnp.min(jnp.where(s == m, row_id, jnp.iinfo(jnp.int32).max), axis=0, keepdims=True)
        vals.append(m)
        ids.append(r)
        s = jnp.where(row_id == r, NEG_INF, s)
    return vals, ids


def _peer_retrieve_body(x_ref, g_ref, sc_ref, sh_ref, wq_ref, keys_ref, h_ref, idx_ref, gate_ref, q_ref):
    x = x_ref[0]
    h = x * lax.rsqrt(jnp.mean(x * x, axis=-1, keepdims=True) + EPS) * g_ref[...]
    h = h * (1.0 + sc_ref[0]) + sh_ref[0]
    for j in range(h.shape[1] // LANES):
        h_ref[pl.ds(j, ROW_TILE, stride=8), :] = h[:, j * LANES:(j + 1) * LANES]
    q_ref[...] = jnp.dot(h.astype(jnp.bfloat16), wq_ref[...],
                         preferred_element_type=jnp.float32).astype(jnp.bfloat16)
    key_iota = lax.broadcasted_iota(jnp.int32, (PEER_NKEYS, RETR_TOKENS), 0)
    sub8 = lax.broadcasted_iota(jnp.int32, (8, RETR_TOKENS), 0)
    cand_id = jnp.concatenate([sub8, sub8 + 8] + [sub8 + a * PEER_TOPK for a in range(1, 8)]
                              + [(sub8 + 8) * PEER_TOPK], axis=0)
    chunks = ROW_TILE // RETR_TOKENS

    def head_chunk(i, carry):
        hd = lax.shift_right_logical(i, chunks.bit_length() - 1)
        ch = i & (chunks - 1)
        tok = pl.ds(pl.multiple_of(ch * RETR_TOKENS, RETR_TOKENS), RETR_TOKENS)
        vals, rows = [], []
        for p in range(2):
            col = pl.multiple_of((hd * 2 + p) * PEER_HALF, PEER_HALF)
            qs = q_ref[tok, pl.ds(col, PEER_HALF)]
            s = lax.dot_general(keys_ref[hd, p], qs, (((1,), (1,)), ((), ())),
                                preferred_element_type=jnp.float32)
            v, r = _top16_rows(s, key_iota)
            vals.append(v)
            rows.append(r)
        v1_hi = jnp.concatenate(vals[0][8:], axis=0)
        v2_lo = jnp.concatenate(vals[1][:8], axis=0)
        v2_hi = jnp.concatenate(vals[1][8:], axis=0)
        cand = jnp.concatenate([vals[0][0] + v2_lo, vals[0][0] + v2_hi]
                               + [vals[0][a] + v2_lo for a in range(1, 8)]
                               + [v1_hi + vals[1][0]], axis=0)
        top_s, pos = _top16_rows(cand, cand_id)
        pos = jnp.concatenate(pos, axis=0)
        pa = pos >> 4
        pb = pos & (PEER_TOPK - 1)
        i1 = jnp.zeros_like(pos)
        i2 = jnp.zeros_like(pos)
        for a in range(PEER_TOPK):
            i1 = jnp.where(pa == a, rows[0][a], i1)
            i2 = jnp.where(pb == a, rows[1][a], i2)
        ts = jnp.concatenate(top_s, axis=0)
        e = jnp.exp(ts - top_s[0])
        gates = e / jnp.sum(e, axis=0, keepdims=True)
        out_rows = pl.ds(pl.multiple_of(hd * PEER_TOPK, PEER_TOPK), PEER_TOPK)
        idx_ref[out_rows, tok] = (i1 * PEER_NKEYS + i2) * PEER_CHUNKS
        gate_ref[out_rows, tok] = gates
        return carry

    lax.fori_loop(0, PEER_HEADS * chunks, head_chunk, 0)


def _peer_retrieve(x, g_pre, sc2, sh2, w_q, sub_keys):
    B, S, D = x.shape
    T = B * S
    nq = w_q.shape[1]
    spb = S // ROW_TILE
    tok_spec = pl.BlockSpec((1, ROW_TILE, D), lambda b, s: (b, s, 0))
    mod_spec = pl.BlockSpec((1, 1, D), lambda b, s: (b, 0, 0))
    slot_spec = pl.BlockSpec((PEER_SLOTS, ROW_TILE), lambda b, s: (0, b * spb + s))
    return pl.pallas_call(
        _peer_retrieve_body,
        grid=(B, spb),
        in_specs=[
            tok_spec,
            pl.BlockSpec((1, D), lambda b, s: (0, 0)),
            mod_spec,
            mod_spec,
            pl.BlockSpec((D, nq), lambda b, s: (0, 0)),
            pl.BlockSpec(sub_keys.shape, lambda b, s: (0, 0, 0, 0)),
        ],
        out_specs=[pl.BlockSpec((ROW_TILE * D // LANES, LANES), lambda b, s: (b * spb + s, 0)), slot_spec, slot_spec],
        out_shape=[
            jax.ShapeDtypeStruct((T * D // LANES, LANES), jnp.float32),
            jax.ShapeDtypeStruct((PEER_SLOTS, T), jnp.int32),
            jax.ShapeDtypeStruct((PEER_SLOTS, T), jnp.float32),
        ],
        scratch_shapes=[pltpu.VMEM((ROW_TILE, nq), jnp.bfloat16)],
        compiler_params=pltpu.CompilerParams(vmem_limit_bytes=VMEM_LIMIT),
        name="peer_retrieve",
    )(x, g_pre.reshape(1, D), sc2, sh2, w_q.astype(jnp.bfloat16), sub_keys.astype(jnp.bfloat16))


PEER_SLOTS = PEER_HEADS * PEER_TOPK
PEER_CHUNKS = D_MODEL // 256
PEER_TB = 256
TILE_STRIDE = PEER_SLOTS + 8


PACK_ROWS = 256


def _bf16_bits(x):
    return pltpu.bitcast(x.astype(jnp.bfloat16).astype(jnp.float32), jnp.uint32)


def _pack_table_body(x_ref, o_ref):
    half = x_ref.shape[1] // 2
    for c in range(PEER_CHUNKS):
        lo = _bf16_bits(x_ref[:, c * LANES:(c + 1) * LANES])
        hi = _bf16_bits(x_ref[:, half + c * LANES:half + (c + 1) * LANES])
        o_ref[pl.ds(c, PACK_ROWS, stride=PEER_CHUNKS), :] = lax.shift_right_logical(lo, jnp.uint32(16)) | hi


def _pack_table(tab):
    E, D = tab.shape
    return pl.pallas_call(
        _pack_table_body,
        grid=(E // PACK_ROWS,),
        in_specs=[pl.BlockSpec((PACK_ROWS, D), lambda i: (i, 0))],
        out_specs=pl.BlockSpec((PACK_ROWS * PEER_CHUNKS, LANES), lambda i: (i, 0)),
        out_shape=jax.ShapeDtypeStruct((E * PEER_CHUNKS, LANES), jnp.uint32),
        name="pack_table",
    )(tab)


IDX_SUB = 16
N_SUB = PEER_TB // IDX_SUB


def _gather_token(stage_ref, half, i, tab_ref, tile_ref):
    for k in range(PEER_SLOTS):
        row = pl.multiple_of(stage_ref[half, i, k], PEER_CHUNKS)
        tile_ref[pl.ds(k, PEER_CHUNKS, stride=TILE_STRIDE), :] = tab_ref[pl.ds(row, PEER_CHUNKS), :]


def _tile_chunk_f32(tile_ref, c):
    w = tile_ref[pl.ds(c * TILE_STRIDE, PEER_SLOTS), :]
    return pltpu.bitcast(w, jnp.bfloat16).astype(jnp.float32)


def _token_stream(idx_ref, tab_ref, stage_ref, sems, tiles, consume):
    def stage_copy(sub, half):
        first = pl.multiple_of(jnp.minimum(sub, N_SUB - 1) * IDX_SUB, IDX_SUB)
        return pltpu.make_async_copy(idx_ref.at[pl.ds(first, IDX_SUB), :], stage_ref.at[half], sems.at[half])

    stage_copy(0, 0).start()
    stage_copy(1, 1).start()
    stage_copy(0, 0).wait()
    _gather_token(stage_ref, 0, 0, tab_ref, tiles[0])

    def two_subs(j, carry):
        for half in range(2):
            sub = 2 * j + half
            for i in range(IDX_SUB):
                if i + 1 < IDX_SUB:
                    _gather_token(stage_ref, half, i + 1, tab_ref, tiles[(i + 1) % 2])
                else:
                    stage_copy(sub + 1, 1 - half).wait()
                    stage_copy(sub + 2, half).start()
                    _gather_token(stage_ref, 1 - half, 0, tab_ref, tiles[(i + 1) % 2])
                consume(tiles[i % 2], sub * IDX_SUB + i, half == 1 and i == IDX_SUB - 1)
        return carry
    lax.fori_loop(0, N_SUB // 2, two_subs, 0)
    stage_copy(N_SUB + 1, 1).wait()


def _token_block_cols(t):
    return pl.ds(pl.multiple_of(lax.shift_right_logical(t, 7) * LANES, LANES), LANES)


def _stage_scratch():
    return [
        pltpu.SMEM((2, IDX_SUB, PEER_SLOTS), jnp.int32),
        pltpu.SemaphoreType.DMA((2,)),
        pltpu.VMEM((PEER_CHUNKS * TILE_STRIDE, LANES), jnp.uint32),
        pltpu.VMEM((PEER_CHUNKS * TILE_STRIDE, LANES), jnp.uint32),
    ]


def _peer_u_body(idx_ref, h_ref, gate_ref, tab_ref, coef_ref, stage_ref, sems, tile_a, tile_b, sblk_ref):
    lane = lax.broadcasted_iota(jnp.int32, (2 * PEER_SLOTS, LANES), 1)
    odd = (lax.broadcasted_iota(jnp.int32, (8, LANES), 0) & 1) == 1

    def consume(tile_ref, t, may_end_block):
        acc = jnp.zeros((2 * PEER_SLOTS // 8, 8, LANES), jnp.float32)
        for c in range(PEER_CHUNKS):
            h_lo = jnp.broadcast_to(h_ref[t, pl.ds(c, 1), :], (8, LANES))
            h_hi = jnp.broadcast_to(h_ref[t, pl.ds(PEER_CHUNKS + c, 1), :], (8, LANES))
            pattern = jnp.where(odd, h_hi, h_lo)
            x = _tile_chunk_f32(tile_ref, c).reshape(2 * PEER_SLOTS // 8, 8, LANES)
            acc = acc + x * pattern[None]
        s = jnp.sum(acc.reshape(2 * PEER_SLOTS, LANES), axis=1, keepdims=True)
        tl = t & (LANES - 1)
        sblk_ref[...] = jnp.where(lane == tl, s, sblk_ref[...])
        if not may_end_block:
            return

        @pl.when(tl == LANES - 1)
        def _():
            even_rows = sblk_ref[pl.ds(0, PEER_SLOTS, stride=2), :]
            odd_rows = sblk_ref[pl.ds(1, PEER_SLOTS, stride=2), :]
            a = even_rows + odd_rows
            cols = _token_block_cols(t)
            coef = gate_ref[:, cols] * (0.5 * a * (1.0 + lax.erf(a * (2.0 ** -0.5))))
            sblk_ref[pl.ds(0, PEER_SLOTS, stride=2), :] = coef
            sblk_ref[pl.ds(1, PEER_SLOTS, stride=2), :] = coef
            coef_ref[:, cols] = sblk_ref[...]

    _token_stream(idx_ref, tab_ref, stage_ref, sems, (tile_a, tile_b), consume)


def _peer_u_pass(row_offsets, ht, gates_t, tab):
    T = ht.shape[0]
    return pl.pallas_call(
        _peer_u_body,
        grid=(T // PEER_TB,),
        in_specs=[
            pl.BlockSpec((PEER_TB, PEER_SLOTS), lambda i: (i, 0)),
            pl.BlockSpec((PEER_TB, 8, LANES), lambda i: (i, 0, 0)),
            pl.BlockSpec((PEER_SLOTS, PEER_TB), lambda i: (0, i)),
            pl.BlockSpec(memory_space=pltpu.VMEM),
        ],
        out_specs=pl.BlockSpec((2 * PEER_SLOTS, PEER_TB), lambda i: (0, i)),
        out_shape=jax.ShapeDtypeStruct((2 * PEER_SLOTS, T), jnp.float32),
        scratch_shapes=_stage_scratch() + [pltpu.VMEM((2 * PEER_SLOTS, LANES), jnp.float32)],
        compiler_params=pltpu.CompilerParams(vmem_limit_bytes=VMEM_LIMIT),
        name="peer_u_pass",
    )(row_offsets, ht, gates_t, tab)


def _peer_v_body(idx_ref, coef_ref, tab_ref, y_ref, stage_ref, sems, tile_a, tile_b):
    lane = lax.broadcasted_iota(jnp.int32, (2 * PEER_SLOTS, LANES), 1)

    def consume(tile_ref, t, may_end_block):
        del may_end_block
        tl = t & (LANES - 1)
        coef_blk = coef_ref[:, _token_block_cols(t)]
        ccol = jnp.sum(jnp.where(lane == tl, coef_blk, 0.0), axis=1, keepdims=True)
        ccol = ccol.reshape(2 * PEER_SLOTS // 8, 8, 1)
        for c in range(PEER_CHUNKS):
            x = _tile_chunk_f32(tile_ref, c).reshape(2 * PEER_SLOTS // 8, 8, LANES)
            part = jnp.sum(x * ccol, axis=0)
            part = part + pltpu.roll(part, 4, axis=0)
            part = part + pltpu.roll(part, 2, axis=0)
            y_ref[t, pl.ds(c, 1), :] = part[0:1]
            y_ref[t, pl.ds(PEER_CHUNKS + c, 1), :] = part[1:2]

    _token_stream(idx_ref, tab_ref, stage_ref, sems, (tile_a, tile_b), consume)


def _peer_v_pass(row_offsets, coef_dup, tab):
    T = row_offsets.shape[0]
    return pl.pallas_call(
        _peer_v_body,
        grid=(T // PEER_TB,),
        in_specs=[
            pl.BlockSpec((PEER_TB, PEER_SLOTS), lambda i: (i, 0)),
            pl.BlockSpec((2 * PEER_SLOTS, PEER_TB), lambda i: (0, i)),
            pl.BlockSpec(memory_space=pltpu.VMEM),
        ],
        out_specs=pl.BlockSpec((PEER_TB, 8, LANES), lambda i: (i, 0, 0)),
        out_shape=jax.ShapeDtypeStruct((T, 8, LANES), jnp.float32),
        scratch_shapes=_stage_scratch(),
        compiler_params=pltpu.CompilerParams(vmem_limit_bytes=VMEM_LIMIT),
        name="peer_v_pass",
    )(row_offsets, coef_dup, tab)


def _post_residual_body(x_ref, y_ref, g_ref, gate_ref, o_ref):
    pieces = [y_ref[pl.ds(j, ROW_TILE, stride=8), :] for j in range(x_ref.shape[2] // LANES)]
    sq = pieces[0] * pieces[0]
    for p in pieces[1:]:
        sq = sq + p * p
    inv = lax.rsqrt(jnp.sum(sq, axis=-1, keepdims=True) * (1.0 / x_ref.shape[2]) + EPS)
    for j, p in enumerate(pieces):
        cols = slice(j * LANES, (j + 1) * LANES)
        o_ref[0, :, cols] = x_ref[0, :, cols] + gate_ref[0, :, cols] * (p * inv * g_ref[:, cols])


def _post_residual(x, y, g, gate):
    B, S, D = x.shape
    pieces = D // LANES
    spb = S // ROW_TILE
    row_spec = pl.BlockSpec((1, ROW_TILE, D), lambda b, s: (b, s, 0))
    return pl.pallas_call(
        _post_residual_body,
        grid=(B, spb),
        in_specs=[
            row_spec,
            pl.BlockSpec((ROW_TILE * pieces, LANES), lambda b, s: (b * spb + s, 0)),
            pl.BlockSpec((1, D), lambda b, s: (0, 0)),
            pl.BlockSpec((1, 1, D), lambda b, s: (b, 0, 0)),
        ],
        out_specs=row_spec,
        out_shape=jax.ShapeDtypeStruct((B, S, D), x.dtype),
        name="post_residual",
    )(x, y.reshape(B * S * pieces, LANES), g.reshape(1, D), gate)


def kernel(x, c, w_mod, b_mod, g_pre_mix, g_post_mix, w_in, w_alpha_f, b_alpha_f, w_alpha_b, b_alpha_b,
           g_gla_head, w_pool, pool_scale, w_out, g_pre_ffn, g_post_ffn, w_peer_q, peer_sub_keys,
           peer_u, peer_v):
    for l in range(w_mod.shape[0]):
        mod = _modulation(c, w_mod[l], b_mod[l])
        sh1, sc1, gt1, sh2, sc2, gt2 = [m[:, None, :] for m in jnp.split(mod, N_MOD, axis=-1)]
        z = _in_proj(x, g_pre_mix[l], sc1, sh1, w_in[l])
        o_gla = _gla(z, w_alpha_f[l], b_alpha_f[l], w_alpha_b[l], b_alpha_b[l], g_gla_head[l])
        x = _mix_out(x, o_gla, z, w_pool[l], pool_scale[l], w_out[l], g_post_mix[l], gt1)
        y = _peer_ffn(x, g_pre_ffn[l], sc2, sh2, w_peer_q[l], peer_sub_keys[l], peer_u[l], peer_v[l])
        x = _post_residual(x, y, g_post_ffn[l], gt2)
    return x
```

```python
import jax
import jax.numpy as jnp
from jax import lax
from jax.experimental import pallas as pl
from jax.experimental.pallas import tpu as pltpu

D_MODEL = 1024
GLA_WIDTH = 512
POOL_WIDTH = 512
GLA_HEADS = 4
GLA_DV = 128
GLA_DK = 64
GLA_KEY_WIDTH = 256
GLA_GATE_RANK = 16
GLA_TAU = 16.0
GLA_CHUNK = 64
POOL_WINDOWS = (2, 4, 8, 16)
POOL_GROUP_WIDTH = 128
PEER_HEADS = 8
PEER_NKEYS = 128
PEER_HALF = 128
PEER_TOPK = 16
N_MOD = 6
EPS = 1e-6

LANES = 128
VMEM_LIMIT = 48 * 1024 * 1024
ROW_TILE = 512

Z_Q, Z_K, Z_V, Z_R, Z_P, Z_A = 0, 256, 512, 1024, 1536, 2048
Z_COLS = 2176
POOL_HALO = 8
GLA_UNROLL = 8


def _mod_body(c_ref, w_ref, b_ref, o_ref):
    c = c_ref[...]
    cond = c * (1.0 / (1.0 + jnp.exp(-c)))
    o_ref[...] = jnp.dot(cond.astype(jnp.bfloat16), w_ref[...].astype(jnp.bfloat16),
                         preferred_element_type=jnp.float32) + b_ref[...]


def _modulation(c, w_mod, b_mod):
    B, D = c.shape
    n = w_mod.shape[1]
    return pl.pallas_call(
        _mod_body,
        grid=(n // D,),
        in_specs=[
            pl.BlockSpec((B, D), lambda j: (0, 0)),
            pl.BlockSpec((D, D), lambda j: (0, j)),
            pl.BlockSpec((1, D), lambda j: (0, j)),
        ],
        out_specs=pl.BlockSpec((B, D), lambda j: (0, j)),
        out_shape=jax.ShapeDtypeStruct((B, n), jnp.float32),
        name="modulation",
    )(c, w_mod, b_mod.reshape(1, n))


def _in_proj_body(x_ref, g_ref, sc_ref, sh_ref, w_ref, z_ref):
    x = x_ref[0]
    h = x * lax.rsqrt(jnp.mean(x * x, axis=-1, keepdims=True) + EPS) * g_ref[...]
    h = h * (1.0 + sc_ref[0]) + sh_ref[0]
    z_ref[0] = jnp.dot(h.astype(jnp.bfloat16), w_ref[...], preferred_element_type=jnp.float32)


def _in_proj(x, g, sc, sh, w_in):
    B, S, D = x.shape
    w = jnp.concatenate([w_in[:, :Z_P], w_in[:, Z_P + 32:], w_in[:, Z_P:Z_P + 32],
                         jnp.zeros((D, Z_COLS - Z_A - 32), w_in.dtype)], axis=1).astype(jnp.bfloat16)
    mod_spec = pl.BlockSpec((1, 1, D), lambda b, s: (b, 0, 0))
    return pl.pallas_call(
        _in_proj_body,
        grid=(B, S // ROW_TILE),
        in_specs=[
            pl.BlockSpec((1, ROW_TILE, D), lambda b, s: (b, s, 0)),
            pl.BlockSpec((1, D), lambda b, s: (0, 0)),
            mod_spec,
            mod_spec,
            pl.BlockSpec((D, Z_COLS), lambda b, s: (0, 0)),
        ],
        out_specs=pl.BlockSpec((1, ROW_TILE, Z_COLS), lambda b, s: (b, s, 0)),
        out_shape=jax.ShapeDtypeStruct((B, S, Z_COLS), jnp.float32),
        compiler_params=pltpu.CompilerParams(vmem_limit_bytes=VMEM_LIMIT),
        name="in_proj",
    )(x, g.reshape(1, D), sc, sh, w)


def _log_sigmoid(x):
    return jnp.minimum(x, 0.0) - jnp.log1p(jnp.exp(-jnp.abs(x)))


def _bf16_dot(a, b, dims):
    return lax.dot_general(a.astype(jnp.bfloat16), b.astype(jnp.bfloat16), (dims, ((), ())),
                           preferred_element_type=jnp.float32)


def _gla_body(q_ref, k_ref, v_ref, r_ref, a_ref, wf_ref, bf_ref, wb_ref, bb_ref, gh_ref, o_ref,
              of_ref, st_ref):
    C = GLA_CHUNK
    n_chunks = q_ref.shape[1] // C
    row = lax.broadcasted_iota(jnp.int32, (C, C), 0)
    col = lax.broadcasted_iota(jnp.int32, (C, C), 1)
    lane = lax.broadcasted_iota(jnp.int32, (1, LANES), 1)
    head_lanes = [lane < GLA_DK, lane >= GLA_DK]

    def chunk_group(chunk_ids, w_ref, b_ref, cum_mask, keep_mask, total_row, emit):
        rows = [pl.ds(pl.multiple_of(n * C, C), C) for n in chunk_ids]
        log_a = [_log_sigmoid(_bf16_dot(a_ref[0, r, :], w_ref[0], ((1,), (0,))) + b_ref[0]) * (1.0 / GLA_TAU)
                 for r in rows]
        b = [jnp.dot(cum_mask, la, precision=lax.Precision.HIGHEST, preferred_element_type=jnp.float32)
             for la in log_a]
        qm, k_inv, k_end, gamma = [], [], [], []
        for r, bc in zip(rows, b):
            b_tot = bc[total_row:total_row + 1, :]
            q_dec = q_ref[0, r, :] * (GLA_DK ** -0.5) * jnp.exp(bc)
            k = k_ref[0, r, :]
            qm.append([jnp.where(head_lanes[hd], q_dec, 0.0) for hd in range(2)])
            k_inv.append(k * jnp.exp(-bc))
            k_end.append(k * jnp.exp(b_tot - bc))
            gamma.append(jnp.exp(b_tot))
        v = [[v_ref[0, r, hd * GLA_DV:(hd + 1) * GLA_DV] for hd in range(2)] for r in rows]
        scores = [[jnp.where(keep_mask, _bf16_dot(qm[u][hd], k_inv[u], ((1,), (1,))), 0.0) for hd in range(2)]
                  for u in range(len(rows))]
        o_intra = [[_bf16_dot(scores[u][hd], v[u][hd], ((1,), (0,))) for hd in range(2)] for u in range(len(rows))]
        kv = [[_bf16_dot(v[u][hd], k_end[u], ((0,), (0,))) for hd in range(2)] for u in range(len(rows))]
        for u, r in enumerate(rows):
            for hd in range(2):
                state = st_ref[hd]
                emit(r, hd, o_intra[u][hd] + _bf16_dot(qm[u][hd], state, ((1,), (1,))))
                st_ref[hd] = state * gamma[u] + kv[u][hd]

    st_ref[...] = jnp.zeros_like(st_ref)
    prefix = (row >= col).astype(jnp.float32)

    def emit_fwd(rows, hd, o):
        of_ref[rows, hd * GLA_DV:(hd + 1) * GLA_DV] = o

    def fwd(i, carry):
        chunk_group([i * GLA_UNROLL + u for u in range(GLA_UNROLL)], wf_ref, bf_ref, prefix, row >= col, C - 1,
                    emit_fwd)
        return carry
    lax.fori_loop(0, n_chunks // GLA_UNROLL, fwd, 0)

    st_ref[...] = jnp.zeros_like(st_ref)
    suffix = (row <= col).astype(jnp.float32)

    def emit_bwd(rows, hd, o):
        cols = slice(hd * GLA_DV, (hd + 1) * GLA_DV)
        tot = of_ref[rows, cols] + o
        y = tot * lax.rsqrt(jnp.mean(tot * tot, axis=-1, keepdims=True) + EPS) * gh_ref[0, :, cols]
        r = r_ref[0, rows, cols]
        o_ref[0, rows, cols] = y * (r * (1.0 / (1.0 + jnp.exp(-r))))

    def bwd(i, carry):
        chunk_group([n_chunks - 1 - (i * GLA_UNROLL + u) for u in range(GLA_UNROLL)], wb_ref, bb_ref, suffix,
                    row < col, 0, emit_bwd)
        return carry
    lax.fori_loop(0, n_chunks // GLA_UNROLL, bwd, 0)


def _gla(z, w_af, b_af, w_ab, b_ab, g_head):
    B, S, _ = z.shape
    pairs = GLA_HEADS // 2

    def pad_w(w, first_row):
        wp = jnp.zeros((pairs, LANES, LANES), jnp.float32)
        return wp.at[:, first_row:first_row + GLA_GATE_RANK, :].set(
            jnp.transpose(w.reshape(GLA_GATE_RANK, pairs, LANES), (1, 0, 2)))

    def lane_blk(off):
        return pl.BlockSpec((1, S, LANES), lambda b, j: (b, 0, off // LANES + j))

    def wide_blk(off):
        return pl.BlockSpec((1, S, 2 * LANES), lambda b, j: (b, 0, off // (2 * LANES) + j))

    w_spec = pl.BlockSpec((1, LANES, LANES), lambda b, j: (j, 0, 0))
    b_spec = pl.BlockSpec((1, 1, LANES), lambda b, j: (j, 0, 0))
    return pl.pallas_call(
        _gla_body,
        grid=(B, pairs),
        in_specs=[
            lane_blk(Z_Q), lane_blk(Z_K), wide_blk(Z_V), wide_blk(Z_R),
            pl.BlockSpec((1, S, LANES), lambda b, j: (b, 0, Z_A // LANES)),
            w_spec, b_spec, w_spec, b_spec,
            pl.BlockSpec((1, 1, 2 * LANES), lambda b, j: (j, 0, 0)),
        ],
        out_specs=pl.BlockSpec((1, S, 2 * LANES), lambda b, j: (b, 0, j)),
        out_shape=jax.ShapeDtypeStruct((B, S, GLA_WIDTH), jnp.float32),
        scratch_shapes=[
            pltpu.VMEM((S, 2 * LANES), jnp.float32),
            pltpu.VMEM((2, GLA_DV, LANES), jnp.float32),
        ],
        compiler_params=pltpu.CompilerParams(vmem_limit_bytes=VMEM_LIMIT),
        name="gla",
    )(z, z, z, z, z,
      pad_w(w_af, 0), b_af.reshape(pairs, 1, LANES),
      pad_w(w_ab, GLA_GATE_RANK), b_ab.reshape(pairs, 1, LANES),
      g_head.reshape(pairs, 1, 2 * LANES))


def _mix_out_body(x_ref, o_ref, p_ref, pp_ref, pn_ref, wp_ref, ps_ref, wo_ref, g_ref, gate_ref, out_ref, pe_ref):
    s = pl.program_id(1)
    n_s = pl.num_programs(1)
    seq_len = n_s * ROW_TILE
    pe_ref[pl.ds(0, POOL_HALO), :] = jnp.where(s > 0, pp_ref[0], 0.0)
    pe_ref[pl.ds(POOL_HALO, ROW_TILE), :] = p_ref[0]
    pe_ref[pl.ds(POOL_HALO + ROW_TILE, POOL_HALO), :] = jnp.where(s < n_s - 1, pn_ref[0], 0.0)
    pos = s * ROW_TILE + lax.broadcasted_iota(jnp.int32, (ROW_TILE, POOL_GROUP_WIDTH), 0)
    mix = jnp.dot(o_ref[0].astype(jnp.bfloat16), wo_ref[pl.ds(0, GLA_WIDTH), :],
                  preferred_element_type=jnp.float32)
    for gi, w in enumerate(POOL_WINDOWS):
        cols = pl.ds(gi * POOL_GROUP_WIDTH, POOL_GROUP_WIDTH)
        total = pe_ref[pl.ds(POOL_HALO - w // 2, ROW_TILE), cols]
        for d in range(-w // 2 + 1, w // 2):
            total = total + pe_ref[pl.ds(POOL_HALO + d, ROW_TILE), cols]
        count = (jnp.minimum(pos + w // 2, seq_len) - jnp.maximum(pos - w // 2, 0)).astype(jnp.float32)
        pooled = total / count - pe_ref[pl.ds(POOL_HALO, ROW_TILE), cols]
        y = jnp.dot(pooled.astype(jnp.bfloat16), wp_ref[gi], preferred_element_type=jnp.float32)
        y = y * ps_ref[:, gi * POOL_GROUP_WIDTH:(gi + 1) * POOL_GROUP_WIDTH]
        mix = mix + jnp.dot(y.astype(jnp.bfloat16),
                            wo_ref[pl.ds(GLA_WIDTH + gi * POOL_GROUP_WIDTH, POOL_GROUP_WIDTH), :],
                            preferred_element_type=jnp.float32)
    yn = mix * lax.rsqrt(jnp.mean(mix * mix, axis=-1, keepdims=True) + EPS) * g_ref[...]
    out_ref[0] = x_ref[0] + gate_ref[0] * yn


def _mix_out(x, o_gla, z, w_pool, pool_scale, w_out, g, gate):
    B, S, D = x.shape
    halo_per_tile = ROW_TILE // POOL_HALO
    last_halo = S // POOL_HALO - 1
    row_spec = pl.BlockSpec((1, ROW_TILE, D), lambda b, s: (b, s, 0))
    return pl.pallas_call(
        _mix_out_body,
        grid=(B, S // ROW_TILE),
        in_specs=[
            row_spec,
            pl.BlockSpec((1, ROW_TILE, GLA_WIDTH), lambda b, s: (b, s, 0)),
            pl.BlockSpec((1, ROW_TILE, POOL_WIDTH), lambda b, s: (b, s, Z_P // POOL_WIDTH)),
            pl.BlockSpec((1, POOL_HALO, POOL_WIDTH),
                         lambda b, s: (b, jnp.maximum(s * halo_per_tile - 1, 0), Z_P // POOL_WIDTH)),
            pl.BlockSpec((1, POOL_HALO, POOL_WIDTH),
                         lambda b, s: (b, jnp.minimum((s + 1) * halo_per_tile, last_halo), Z_P // POOL_WIDTH)),
            pl.BlockSpec(w_pool.shape, lambda b, s: (0, 0, 0)),
            pl.BlockSpec((1, POOL_WIDTH), lambda b, s: (0, 0)),
            pl.BlockSpec((D, D), lambda b, s: (0, 0)),
            pl.BlockSpec((1, D), lambda b, s: (0, 0)),
            pl.BlockSpec((1, 1, D), lambda b, s: (b, 0, 0)),
        ],
        out_specs=row_spec,
        out_shape=jax.ShapeDtypeStruct((B, S, D), jnp.float32),
        scratch_shapes=[pltpu.VMEM((ROW_TILE + 2 * POOL_HALO, POOL_WIDTH), jnp.float32)],
        compiler_params=pltpu.CompilerParams(vmem_limit_bytes=VMEM_LIMIT),
        name="mix_out",
    )(x, o_gla, z, z, z, w_pool.astype(jnp.bfloat16), pool_scale.reshape(1, POOL_WIDTH),
      w_out.astype(jnp.bfloat16), g.reshape(1, D), gate)


def _peer_ffn(x, g_pre, sc2, sh2, w_q, sub_keys, u_tab, v_tab):
    B, S, D = x.shape
    h2, offsets_t, gates_t = _peer_retrieve(x, g_pre, sc2, sh2, w_q, sub_keys)
    row_offsets = offsets_t.T
    coef_dup = _peer_u_pass(row_offsets, h2.reshape(B * S, D // LANES, LANES), gates_t, _pack_table(u_tab))
    return _peer_v_pass(row_offsets, coef_dup, _pack_table(v_tab))


NEG_INF = float("-inf")
RETR_TOKENS = 256


def _top16_rows(s, row_id):
    vals, ids = [], []
    for _ in range(PEER_TOPK):
        m = jnp.max(s, axis=0, keepdims=True)
        r = jnp.min(jnp.where(s == m, row_id, jnp.iinfo(jnp.int32).max), axis=0, keepdims=True)
        vals.append(m)
        ids.append(r)
        s = jnp.where(row_id == r, NEG_INF, s)
    return vals, ids


def _peer_retrieve_body(x_ref, g_ref, sc_ref, sh_ref, wq_ref, keys_ref, h_ref, idx_ref, gate_ref, q_ref):
    x = x_ref[0]
    h = x * lax.rsqrt(jnp.mean(x * x, axis=-1, keepdims=True) + EPS) * g_ref[...]
    h = h * (1.0 + sc_ref[0]) + sh_ref[0]
    for j in range(h.shape[1] // LANES):
        h_ref[pl.ds(j, ROW_TILE, stride=8), :] = h[:, j * LANES:(j + 1) * LANES]
    q_ref[...] = jnp.dot(h.astype(jnp.bfloat16), wq_ref[...],
                         preferred_element_type=jnp.float32).astype(jnp.bfloat16)
    key_iota = lax.broadcasted_iota(jnp.int32, (PEER_NKEYS, RETR_TOKENS), 0)
    sub8 = lax.broadcasted_iota(jnp.int32, (8, RETR_TOKENS), 0)
    cand_id = jnp.concatenate([sub8, sub8 + 8] + [sub8 + a * PEER_TOPK for a in range(1, 8)]
                              + [(sub8 + 8) * PEER_TOPK], axis=0)
    chunks = ROW_TILE // RETR_TOKENS

    def head_chunk(i, carry):
        hd = lax.shift_right_logical(i, chunks.bit_length() - 1)
        ch = i & (chunks - 1)
        tok = pl.ds(pl.multiple_of(ch * RETR_TOKENS, RETR_TOKENS), RETR_TOKENS)
        vals, rows = [], []
        for p in range(2):
            col = pl.multiple_of((hd * 2 + p) * PEER_HALF, PEER_HALF)
            qs = q_ref[tok, pl.ds(col, PEER_HALF)]
            s = lax.dot_general(keys_ref[hd, p], qs, (((1,), (1,)), ((), ())),
                                preferred_element_type=jnp.float32)
            v, r = _top16_rows(s, key_iota)
            vals.append(v)
            rows.append(r)
        v1_hi = jnp.concatenate(vals[0][8:], axis=0)
        v2_lo = jnp.concatenate(vals[1][:8], axis=0)
        v2_hi = jnp.concatenate(vals[1][8:], axis=0)
        cand = jnp.concatenate([vals[0][0] + v2_lo, vals[0][0] + v2_hi]
                               + [vals[0][a] + v2_lo for a in range(1, 8)]
                               + [v1_hi + vals[1][0]], axis=0)
        top_s, pos = _top16_rows(cand, cand_id)
        pos = jnp.concatenate(pos, axis=0)
        pa = pos >> 4
        pb = pos & (PEER_TOPK - 1)
        i1 = jnp.zeros_like(pos)
        i2 = jnp.zeros_like(pos)
        for a in range(PEER_TOPK):
            i1 = jnp.where(pa == a, rows[0][a], i1)
            i2 = jnp.where(pb == a, rows[1][a], i2)
        ts = jnp.concatenate(top_s, axis=0)
        e = jnp.exp(ts - top_s[0])
        gates = e / jnp.sum(e, axis=0, keepdims=True)
        out_rows = pl.ds(pl.multiple_of(hd * PEER_TOPK, PEER_TOPK), PEER_TOPK)
        idx_ref[out_rows, tok] = (i1 * PEER_NKEYS + i2) * PEER_CHUNKS
        gate_ref[out_rows, tok] = gates
        return carry

    lax.fori_loop(0, PEER_HEADS * chunks, head_chunk, 0)


def _peer_retrieve(x, g_pre, sc2, sh2, w_q, sub_keys):
    B, S, D = x.shape
    T = B * S
    nq = w_q.shape[1]
    spb = S // ROW_TILE
    tok_spec = pl.BlockSpec((1, ROW_TILE, D), lambda b, s: (b, s, 0))
    mod_spec = pl.BlockSpec((1, 1, D), lambda b, s: (b, 0, 0))
    slot_spec = pl.BlockSpec((PEER_SLOTS, ROW_TILE), lambda b, s: (0, b * spb + s))
    return pl.pallas_call(
        _peer_retrieve_body,
        grid=(B, spb),
        in_specs=[
            tok_spec,
            pl.BlockSpec((1, D), lambda b, s: (0, 0)),
            mod_spec,
            mod_spec,
            pl.BlockSpec((D, nq), lambda b, s: (0, 0)),
            pl.BlockSpec(sub_keys.shape, lambda b, s: (0, 0, 0, 0)),
        ],
        out_specs=[pl.BlockSpec((ROW_TILE * D // LANES, LANES), lambda b, s: (b * spb + s, 0)), slot_spec, slot_spec],
        out_shape=[
            jax.ShapeDtypeStruct((T * D // LANES, LANES), jnp.float32),
            jax.ShapeDtypeStruct((PEER_SLOTS, T), jnp.int32),
            jax.ShapeDtypeStruct((PEER_SLOTS, T), jnp.float32),
        ],
        scratch_shapes=[pltpu.VMEM((ROW_TILE, nq), jnp.bfloat16)],
        compiler_params=pltpu.CompilerParams(vmem_limit_bytes=VMEM_LIMIT),
        name="peer_retrieve",
    )(x, g_pre.reshape(1, D), sc2, sh2, w_q.astype(jnp.bfloat16), sub_keys.astype(jnp.bfloat16))


PEER_SLOTS = PEER_HEADS * PEER_TOPK
PEER_CHUNKS = D_MODEL // 256
PEER_TB = 256


PACK_ROWS = 256


def _bf16_bits(x):
    return pltpu.bitcast(x.astype(jnp.bfloat16).astype(jnp.float32), jnp.uint32)


def _pack_table_body(x_ref, o_ref):
    half = x_ref.shape[1] // 2
    for c in range(PEER_CHUNKS):
        lo = _bf16_bits(x_ref[:, c * LANES:(c + 1) * LANES])
        hi = _bf16_bits(x_ref[:, half + c * LANES:half + (c + 1) * LANES])
        o_ref[pl.ds(c, PACK_ROWS, stride=PEER_CHUNKS), :] = lax.shift_right_logical(lo, jnp.uint32(16)) | hi


def _pack_table(tab):
    E, D = tab.shape
    return pl.pallas_call(
        _pack_table_body,
        grid=(E // PACK_ROWS,),
        in_specs=[pl.BlockSpec((PACK_ROWS, D), lambda i: (i, 0))],
        out_specs=pl.BlockSpec((PACK_ROWS * PEER_CHUNKS, LANES), lambda i: (i, 0)),
        out_shape=jax.ShapeDtypeStruct((E * PEER_CHUNKS, LANES), jnp.uint32),
        name="pack_table",
    )(tab)


IDX_SUB = 16
N_SUB = PEER_TB // IDX_SUB


def _gather_token(stage_ref, half, i, tab_ref, tile_ref):
    for k in range(PEER_SLOTS):
        row = pl.multiple_of(stage_ref[half, i, k], PEER_CHUNKS)
        tile_ref[pl.ds(k * PEER_CHUNKS, PEER_CHUNKS), :] = tab_ref[pl.ds(row, PEER_CHUNKS), :]


def _tile_chunk_f32(tile_ref, c):
    w = tile_ref[pl.ds(c, PEER_SLOTS, stride=PEER_CHUNKS), :]
    return pltpu.bitcast(w, jnp.bfloat16).astype(jnp.float32)


def _token_stream(idx_ref, tab_ref, stage_ref, sems, tiles, consume):
    def stage_copy(sub, half):
        first = pl.multiple_of(jnp.minimum(sub, N_SUB - 1) * IDX_SUB, IDX_SUB)
        return pltpu.make_async_copy(idx_ref.at[pl.ds(first, IDX_SUB), :], stage_ref.at[half], sems.at[half])

    stage_copy(0, 0).start()
    stage_copy(1, 1).start()
    stage_copy(0, 0).wait()
    _gather_token(stage_ref, 0, 0, tab_ref, tiles[0])

    def two_subs(j, carry):
        for half in range(2):
            sub = 2 * j + half
            for i in range(IDX_SUB):
                if i + 1 < IDX_SUB:
                    _gather_token(stage_ref, half, i + 1, tab_ref, tiles[(i + 1) % 2])
                else:
                    stage_copy(sub + 1, 1 - half).wait()
                    stage_copy(sub + 2, half).start()
                    _gather_token(stage_ref, 1 - half, 0, tab_ref, tiles[(i + 1) % 2])
                consume(tiles[i % 2], sub * IDX_SUB + i, half == 1 and i == IDX_SUB - 1)
        return carry
    lax.fori_loop(0, N_SUB // 2, two_subs, 0)
    stage_copy(N_SUB + 1, 1).wait()


def _token_block_cols(t):
    return pl.ds(pl.multiple_of(lax.shift_right_logical(t, 7) * LANES, LANES), LANES)


def _stage_scratch():
    return [
        pltpu.SMEM((2, IDX_SUB, PEER_SLOTS), jnp.int32),
        pltpu.SemaphoreType.DMA((2,)),
        pltpu.VMEM((PEER_SLOTS * PEER_CHUNKS, LANES), jnp.uint32),
        pltpu.VMEM((PEER_SLOTS * PEER_CHUNKS, LANES), jnp.uint32),
    ]


def _peer_u_body(idx_ref, h_ref, gate_ref, tab_ref, coef_ref, stage_ref, sems, tile_a, tile_b, sblk_ref):
    lane = lax.broadcasted_iota(jnp.int32, (2 * PEER_SLOTS, LANES), 1)
    odd = (lax.broadcasted_iota(jnp.int32, (8, LANES), 0) & 1) == 1

    def consume(tile_ref, t, may_end_block):
        acc = jnp.zeros((2 * PEER_SLOTS // 8, 8, LANES), jnp.float32)
        for c in range(PEER_CHUNKS):
            h_lo = jnp.broadcast_to(h_ref[t, pl.ds(c, 1), :], (8, LANES))
            h_hi = jnp.broadcast_to(h_ref[t, pl.ds(PEER_CHUNKS + c, 1), :], (8, LANES))
            pattern = jnp.where(odd, h_hi, h_lo)
            x = _tile_chunk_f32(tile_ref, c).reshape(2 * PEER_SLOTS // 8, 8, LANES)
            acc = acc + x * pattern[None]
        s = jnp.sum(acc.reshape(2 * PEER_SLOTS, LANES), axis=1, keepdims=True)
        tl = t & (LANES - 1)
        sblk_ref[...] = jnp.where(lane == tl, s, sblk_ref[...])
        if not may_end_block:
            return

        @pl.when(tl == LANES - 1)
        def _():
            even_rows = sblk_ref[pl.ds(0, PEER_SLOTS, stride=2), :]
            odd_rows = sblk_ref[pl.ds(1, PEER_SLOTS, stride=2), :]
            a = even_rows + odd_rows
            cols = _token_block_cols(t)
            coef = gate_ref[:, cols] * (0.5 * a * (1.0 + lax.erf(a * (2.0 ** -0.5))))
            sblk_ref[pl.ds(0, PEER_SLOTS, stride=2), :] = coef
            sblk_ref[pl.ds(1, PEER_SLOTS, stride=2), :] = coef
            coef_ref[:, cols] = sblk_ref[...]

    _token_stream(idx_ref, tab_ref, stage_ref, sems, (tile_a, tile_b), consume)


def _peer_u_pass(row_offsets, ht, gates_t, tab):
    T = ht.shape[0]
    return pl.pallas_call(
        _peer_u_body,
        grid=(T // PEER_TB,),
        in_specs=[
            pl.BlockSpec((PEER_TB, PEER_SLOTS), lambda i: (i, 0)),
            pl.BlockSpec((PEER_TB, 8, LANES), lambda i: (i, 0, 0)),
            pl.BlockSpec((PEER_SLOTS, PEER_TB), lambda i: (0, i)),
            pl.BlockSpec(memory_space=pltpu.VMEM),
        ],
        out_specs=pl.BlockSpec((2 * PEER_SLOTS, PEER_TB), lambda i: (0, i)),
        out_shape=jax.ShapeDtypeStruct((2 * PEER_SLOTS, T), jnp.float32),
        scratch_shapes=_stage_scratch() + [pltpu.VMEM((2 * PEER_SLOTS, LANES), jnp.float32)],
        compiler_params=pltpu.CompilerParams(vmem_limit_bytes=VMEM_LIMIT),
        name="peer_u_pass",
    )(row_offsets, ht, gates_t, tab)


def _peer_v_body(idx_ref, coef_ref, tab_ref, y_ref, stage_ref, sems, tile_a, tile_b):
    lane = lax.broadcasted_iota(jnp.int32, (2 * PEER_SLOTS, LANES), 1)

    def consume(tile_ref, t, may_end_block):
        del may_end_block
        tl = t & (LANES - 1)
        coef_blk = coef_ref[:, _token_block_cols(t)]
        ccol = jnp.sum(jnp.where(lane == tl, coef_blk, 0.0), axis=1, keepdims=True)
        ccol = ccol.reshape(2 * PEER_SLOTS // 8, 8, 1)
        for c in range(PEER_CHUNKS):
            x = _tile_chunk_f32(tile_ref, c).reshape(2 * PEER_SLOTS // 8, 8, LANES)
            part = jnp.sum(x * ccol, axis=0)
            part = part + pltpu.roll(part, 4, axis=0)
            part = part + pltpu.roll(part, 2, axis=0)
            y_ref[t, pl.ds(c, 1), :] = part[0:1]
            y_ref[t, pl.ds(PEER_CHUNKS + c, 1), :] = part[1:2]

    _token_stream(idx_ref, tab_ref, stage_ref, sems, (tile_a, tile_b), consume)


def _peer_v_pass(row_offsets, coef_dup, tab):
    T = row_offsets.shape[0]
    return pl.pallas_call(
        _peer_v_body,
        grid=(T // PEER_TB,),
        in_specs=[
            pl.BlockSpec((PEER_TB, PEER_SLOTS), lambda i: (i, 0)),
            pl.BlockSpec((2 * PEER_SLOTS, PEER_TB), lambda i: (0, i)),
            pl.BlockSpec(memory_space=pltpu.VMEM),
        ],
        out_specs=pl.BlockSpec((PEER_TB, 8, LANES), lambda i: (i, 0, 0)),
        out_shape=jax.ShapeDtypeStruct((T, 8, LANES), jnp.float32),
        scratch_shapes=_stage_scratch(),
        compiler_params=pltpu.CompilerParams(vmem_limit_bytes=VMEM_LIMIT),
        name="peer_v_pass",
    )(row_offsets, coef_dup, tab)


def _post_residual_body(x_ref, y_ref, g_ref, gate_ref, o_ref):
    pieces = [y_ref[pl.ds(j, ROW_TILE, stride=8), :] for j in range(x_ref.shape[2] // LANES)]
    sq = pieces[0] * pieces[0]
    for p in pieces[1:]:
        sq = sq + p * p
    inv = lax.rsqrt(jnp.sum(sq, axis=-1, keepdims=True) * (1.0 / x_ref.shape[2]) + EPS)
    for j, p in enumerate(pieces):
        cols = slice(j * LANES, (j + 1) * LANES)
        o_ref[0, :, cols] = x_ref[0, :, cols] + gate_ref[0, :, cols] * (p * inv * g_ref[:, cols])


def _post_residual(x, y, g, gate):
    B, S, D = x.shape
    pieces = D // LANES
    spb = S // ROW_TILE
    row_spec = pl.BlockSpec((1, ROW_TILE, D), lambda b, s: (b, s, 0))
    return pl.pallas_call(
        _post_residual_body,
        grid=(B, spb),
        in_specs=[
            row_spec,
            pl.BlockSpec((ROW_TILE * pieces, LANES), lambda b, s: (b * spb + s, 0)),
            pl.BlockSpec((1, D), lambda b, s: (0, 0)),
            pl.BlockSpec((1, 1, D), lambda b, s: (b, 0, 0)),
        ],
        out_specs=row_spec,
        out_shape=jax.ShapeDtypeStruct((B, S, D), x.dtype),
        name="post_residual",
    )(x, y.reshape(B * S * pieces, LANES), g.reshape(1, D), gate)


def kernel(x, c, w_mod, b_mod, g_pre_mix, g_post_mix, w_in, w_alpha_f, b_alpha_f, w_alpha_b, b_alpha_b,
           g_gla_head, w_pool, pool_scale, w_out, g_pre_ffn, g_post_ffn, w_peer_q, peer_sub_keys,
           peer_u, peer_v):
    for l in range(w_mod.shape[0]):
        mod = _modulation(c, w_mod[l], b_mod[l])
        sh1, sc1, gt1, sh2, sc2, gt2 = [m[:, None, :] for m in jnp.split(mod, N_MOD, axis=-1)]
        z = _in_proj(x, g_pre_mix[l], sc1, sh1, w_in[l])
        o_gla = _gla(z, w_alpha_f[l], b_alpha_f[l], w_alpha_b[l], b_alpha_b[l], g_gla_head[l])
        x = _mix_out(x, o_gla, z, w_pool[l], pool_scale[l], w_out[l], g_post_mix[l], gt1)
        y = _peer_ffn(x, g_pre_ffn[l], sc2, sh2, w_peer_q[l], peer_sub_keys[l], peer_u[l], peer_v[l])
        x = _post_residual(x, y, g_post_ffn[l], gt2)
    return x
```

```python
import jax
import jax.numpy as jnp
from jax import lax
from jax.experimental import pallas as pl
from jax.experimental.pallas import tpu as pltpu

D_MODEL = 1024
GLA_WIDTH = 512
POOL_WIDTH = 512
GLA_HEADS = 4
GLA_DV = 128
GLA_DK = 64
GLA_KEY_WIDTH = 256
GLA_GATE_RANK = 16
GLA_TAU = 16.0
GLA_CHUNK = 64
POOL_WINDOWS = (2, 4, 8, 16)
POOL_GROUP_WIDTH = 128
PEER_HEADS = 8
PEER_NKEYS = 128
PEER_HALF = 128
PEER_TOPK = 16
N_MOD = 6
EPS = 1e-6

LANES = 128
VMEM_LIMIT = 48 * 1024 * 1024
ROW_TILE = 512

Z_Q, Z_K, Z_V, Z_R, Z_P, Z_A = 0, 256, 512, 1024, 1536, 2048
Z_COLS = 2176
POOL_HALO = 8
GLA_UNROLL = 8


def _mod_body(c_ref, w_ref, b_ref, o_ref):
    c = c_ref[...]
    cond = c * (1.0 / (1.0 + jnp.exp(-c)))
    o_ref[...] = jnp.dot(cond.astype(jnp.bfloat16), w_ref[...].astype(jnp.bfloat16),
                         preferred_element_type=jnp.float32) + b_ref[...]


def _modulation(c, w_mod, b_mod):
    B, D = c.shape
    n = w_mod.shape[1]
    return pl.pallas_call(
        _mod_body,
        grid=(n // D,),
        in_specs=[
            pl.BlockSpec((B, D), lambda j: (0, 0)),
            pl.BlockSpec((D, D), lambda j: (0, j)),
            pl.BlockSpec((1, D), lambda j: (0, j)),
        ],
        out_specs=pl.BlockSpec((B, D), lambda j: (0, j)),
        out_shape=jax.ShapeDtypeStruct((B, n), jnp.float32),
        name="modulation",
    )(c, w_mod, b_mod.reshape(1, n))


def _in_proj_body(x_ref, g_ref, sc_ref, sh_ref, w_ref, z_ref):
    x = x_ref[0]
    h = x * lax.rsqrt(jnp.mean(x * x, axis=-1, keepdims=True) + EPS) * g_ref[...]
    h = h * (1.0 + sc_ref[0]) + sh_ref[0]
    z_ref[0] = jnp.dot(h.astype(jnp.bfloat16), w_ref[...], preferred_element_type=jnp.float32)


def _in_proj(x, g, sc, sh, w_in):
    B, S, D = x.shape
    w = jnp.concatenate([w_in[:, :Z_P], w_in[:, Z_P + 32:], w_in[:, Z_P:Z_P + 32],
                         jnp.zeros((D, Z_COLS - Z_A - 32), w_in.dtype)], axis=1).astype(jnp.bfloat16)
    mod_spec = pl.BlockSpec((1, 1, D), lambda b, s: (b, 0, 0))
    return pl.pallas_call(
        _in_proj_body,
        grid=(B, S // ROW_TILE),
        in_specs=[
            pl.BlockSpec((1, ROW_TILE, D), lambda b, s: (b, s, 0)),
            pl.BlockSpec((1, D), lambda b, s: (0, 0)),
            mod_spec,
            mod_spec,
            pl.BlockSpec((D, Z_COLS), lambda b, s: (0, 0)),
        ],
        out_specs=pl.BlockSpec((1, ROW_TILE, Z_COLS), lambda b, s: (b, s, 0)),
        out_shape=jax.ShapeDtypeStruct((B, S, Z_COLS), jnp.float32),
        compiler_params=pltpu.CompilerParams(vmem_limit_bytes=VMEM_LIMIT),
        name="in_proj",
    )(x, g.reshape(1, D), sc, sh, w)


def _log_sigmoid(x):
    return jnp.minimum(x, 0.0) - jnp.log1p(jnp.exp(-jnp.abs(x)))


def _bf16_dot(a, b, dims):
    return lax.dot_general(a.astype(jnp.bfloat16), b.astype(jnp.bfloat16), (dims, ((), ())),
                           preferred_element_type=jnp.float32)


def _gla_body(q_ref, k_ref, v_ref, r_ref, a_ref, wf_ref, bf_ref, wb_ref, bb_ref, gh_ref, o_ref,
              of_ref, st_ref):
    C = GLA_CHUNK
    n_chunks = q_ref.shape[1] // C
    row = lax.broadcasted_iota(jnp.int32, (C, C), 0)
    col = lax.broadcasted_iota(jnp.int32, (C, C), 1)
    lane = lax.broadcasted_iota(jnp.int32, (1, LANES), 1)
    head_lanes = [lane < GLA_DK, lane >= GLA_DK]

    def chunk_group(chunk_ids, w_ref, b_ref, cum_mask, keep_mask, total_row, emit):
        rows = [pl.ds(pl.multiple_of(n * C, C), C) for n in chunk_ids]
        log_a = [_log_sigmoid(_bf16_dot(a_ref[0, r, :], w_ref[0], ((1,), (0,))) + b_ref[0]) * (1.0 / GLA_TAU)
                 for r in rows]
        b = [jnp.dot(cum_mask, la, precision=lax.Precision.HIGHEST, preferred_element_type=jnp.float32)
             for la in log_a]
        qm, k_inv, k_end, gamma = [], [], [], []
        for r, bc in zip(rows, b):
            b_tot = bc[total_row:total_row + 1, :]
            q_dec = q_ref[0, r, :] * (GLA_DK ** -0.5) * jnp.exp(bc)
            k = k_ref[0, r, :]
            qm.append([jnp.where(head_lanes[hd], q_dec, 0.0) for hd in range(2)])
            k_inv.append(k * jnp.exp(-bc))
            k_end.append(k * jnp.exp(b_tot - bc))
            gamma.append(jnp.exp(b_tot))
        v = [[v_ref[0, r, hd * GLA_DV:(hd + 1) * GLA_DV] for hd in range(2)] for r in rows]
        scores = [[jnp.where(keep_mask, _bf16_dot(qm[u][hd], k_inv[u], ((1,), (1,))), 0.0) for hd in range(2)]
                  for u in range(len(rows))]
        o_intra = [[_bf16_dot(scores[u][hd], v[u][hd], ((1,), (0,))) for hd in range(2)] for u in range(len(rows))]
        kv = [[_bf16_dot(v[u][hd], k_end[u], ((0,), (0,))) for hd in range(2)] for u in range(len(rows))]
        for u, r in enumerate(rows):
            for hd in range(2):
                state = st_ref[hd]
                emit(r, hd, o_intra[u][hd] + _bf16_dot(qm[u][hd], state, ((1,), (1,))))
                st_ref[hd] = state * gamma[u] + kv[u][hd]

    st_ref[...] = jnp.zeros_like(st_ref)
    prefix = (row >= col).astype(jnp.float32)

    def emit_fwd(rows, hd, o):
        of_ref[rows, hd * GLA_DV:(hd + 1) * GLA_DV] = o

    def fwd(i, carry):
        chunk_group([i * GLA_UNROLL + u for u in range(GLA_UNROLL)], wf_ref, bf_ref, prefix, row >= col, C - 1,
                    emit_fwd)
        return carry
    lax.fori_loop(0, n_chunks // GLA_UNROLL, fwd, 0)

    st_ref[...] = jnp.zeros_like(st_ref)
    suffix = (row <= col).astype(jnp.float32)

    def emit_bwd(rows, hd, o):
        cols = slice(hd * GLA_DV, (hd + 1) * GLA_DV)
        tot = of_ref[rows, cols] + o
        y = tot * lax.rsqrt(jnp.mean(tot * tot, axis=-1, keepdims=True) + EPS) * gh_ref[0, :, cols]
        r = r_ref[0, rows, cols]
        o_ref[0, rows, cols] = y * (r * (1.0 / (1.0 + jnp.exp(-r))))

    def bwd(i, carry):
        chunk_group([n_chunks - 1 - (i * GLA_UNROLL + u) for u in range(GLA_UNROLL)], wb_ref, bb_ref, suffix,
                    row < col, 0, emit_bwd)
        return carry
    lax.fori_loop(0, n_chunks // GLA_UNROLL, bwd, 0)


def _gla(z, w_af, b_af, w_ab, b_ab, g_head):
    B, S, _ = z.shape
    pairs = GLA_HEADS // 2

    def pad_w(w, first_row):
        wp = jnp.zeros((pairs, LANES, LANES), jnp.float32)
        return wp.at[:, first_row:first_row + GLA_GATE_RANK, :].set(
            jnp.transpose(w.reshape(GLA_GATE_RANK, pairs, LANES), (1, 0, 2)))

    def lane_blk(off):
        return pl.BlockSpec((1, S, LANES), lambda b, j: (b, 0, off // LANES + j))

    def wide_blk(off):
        return pl.BlockSpec((1, S, 2 * LANES), lambda b, j: (b, 0, off // (2 * LANES) + j))

    w_spec = pl.BlockSpec((1, LANES, LANES), lambda b, j: (j, 0, 0))
    b_spec = pl.BlockSpec((1, 1, LANES), lambda b, j: (j, 0, 0))
    return pl.pallas_call(
        _gla_body,
        grid=(B, pairs),
        in_specs=[
            lane_blk(Z_Q), lane_blk(Z_K), wide_blk(Z_V), wide_blk(Z_R),
            pl.BlockSpec((1, S, LANES), lambda b, j: (b, 0, Z_A // LANES)),
            w_spec, b_spec, w_spec, b_spec,
            pl.BlockSpec((1, 1, 2 * LANES), lambda b, j: (j, 0, 0)),
        ],
        out_specs=pl.BlockSpec((1, S, 2 * LANES), lambda b, j: (b, 0, j)),
        out_shape=jax.ShapeDtypeStruct((B, S, GLA_WIDTH), jnp.float32),
        scratch_shapes=[
            pltpu.VMEM((S, 2 * LANES), jnp.float32),
            pltpu.VMEM((2, GLA_DV, LANES), jnp.float32),
        ],
        compiler_params=pltpu.CompilerParams(vmem_limit_bytes=VMEM_LIMIT),
        name="gla",
    )(z, z, z, z, z,
      pad_w(w_af, 0), b_af.reshape(pairs, 1, LANES),
      pad_w(w_ab, GLA_GATE_RANK), b_ab.reshape(pairs, 1, LANES),
      g_head.reshape(pairs, 1, 2 * LANES))


def _mix_out_body(x_ref, o_ref, p_ref, pp_ref, pn_ref, wp_ref, ps_ref, wo_ref, g_ref, gate_ref, out_ref, pe_ref):
    s = pl.program_id(1)
    n_s = pl.num_programs(1)
    seq_len = n_s * ROW_TILE
    pe_ref[pl.ds(0, POOL_HALO), :] = jnp.where(s > 0, pp_ref[0], 0.0)
    pe_ref[pl.ds(POOL_HALO, ROW_TILE), :] = p_ref[0]
    pe_ref[pl.ds(POOL_HALO + ROW_TILE, POOL_HALO), :] = jnp.where(s < n_s - 1, pn_ref[0], 0.0)
    pos = s * ROW_TILE + lax.broadcasted_iota(jnp.int32, (ROW_TILE, POOL_GROUP_WIDTH), 0)
    mix = jnp.dot(o_ref[0].astype(jnp.bfloat16), wo_ref[pl.ds(0, GLA_WIDTH), :],
                  preferred_element_type=jnp.float32)
    for gi, w in enumerate(POOL_WINDOWS):
        cols = pl.ds(gi * POOL_GROUP_WIDTH, POOL_GROUP_WIDTH)
        total = pe_ref[pl.ds(POOL_HALO - w // 2, ROW_TILE), cols]
        for d in range(-w // 2 + 1, w // 2):
            total = total + pe_ref[pl.ds(POOL_HALO + d, ROW_TILE), cols]
        count = (jnp.minimum(pos + w // 2, seq_len) - jnp.maximum(pos - w // 2, 0)).astype(jnp.float32)
        pooled = total / count - pe_ref[pl.ds(POOL_HALO, ROW_TILE), cols]
        y = jnp.dot(pooled.astype(jnp.bfloat16), wp_ref[gi], preferred_element_type=jnp.float32)
        y = y * ps_ref[:, gi * POOL_GROUP_WIDTH:(gi + 1) * POOL_GROUP_WIDTH]
        mix = mix + jnp.dot(y.astype(jnp.bfloat16),
                            wo_ref[pl.ds(GLA_WIDTH + gi * POOL_GROUP_WIDTH, POOL_GROUP_WIDTH), :],
                            preferred_element_type=jnp.float32)
    yn = mix * lax.rsqrt(jnp.mean(mix * mix, axis=-1, keepdims=True) + EPS) * g_ref[...]
    out_ref[0] = x_ref[0] + gate_ref[0] * yn


def _mix_out(x, o_gla, z, w_pool, pool_scale, w_out, g, gate):
    B, S, D = x.shape
    halo_per_tile = ROW_TILE // POOL_HALO
    last_halo = S // POOL_HALO - 1
    row_spec = pl.BlockSpec((1, ROW_TILE, D), lambda b, s: (b, s, 0))
    return pl.pallas_call(
        _mix_out_body,
        grid=(B, S // ROW_TILE),
        in_specs=[
            row_spec,
            pl.BlockSpec((1, ROW_TILE, GLA_WIDTH), lambda b, s: (b, s, 0)),
            pl.BlockSpec((1, ROW_TILE, POOL_WIDTH), lambda b, s: (b, s, Z_P // POOL_WIDTH)),
            pl.BlockSpec((1, POOL_HALO, POOL_WIDTH),
                         lambda b, s: (b, jnp.maximum(s * halo_per_tile - 1, 0), Z_P // POOL_WIDTH)),
            pl.BlockSpec((1, POOL_HALO, POOL_WIDTH),
                         lambda b, s: (b, jnp.minimum((s + 1) * halo_per_tile, last_halo), Z_P // POOL_WIDTH)),
            pl.BlockSpec(w_pool.shape, lambda b, s: (0, 0, 0)),
            pl.BlockSpec((1, POOL_WIDTH), lambda b, s: (0, 0)),
            pl.BlockSpec((D, D), lambda b, s: (0, 0)),
            pl.BlockSpec((1, D), lambda b, s: (0, 0)),
            pl.BlockSpec((1, 1, D), lambda b, s: (b, 0, 0)),
        ],
        out_specs=row_spec,
        out_shape=jax.ShapeDtypeStruct((B, S, D), jnp.float32),
        scratch_shapes=[pltpu.VMEM((ROW_TILE + 2 * POOL_HALO, POOL_WIDTH), jnp.float32)],
        compiler_params=pltpu.CompilerParams(vmem_limit_bytes=VMEM_LIMIT),
        name="mix_out",
    )(x, o_gla, z, z, z, w_pool.astype(jnp.bfloat16), pool_scale.reshape(1, POOL_WIDTH),
      w_out.astype(jnp.bfloat16), g.reshape(1, D), gate)


def _peer_ffn(x, g_pre, sc2, sh2, w_q, sub_keys, u_tab, v_tab):
    B, S, D = x.shape
    h2, offsets_t, gates_t = _peer_retrieve(x, g_pre, sc2, sh2, w_q, sub_keys)
    row_offsets = offsets_t.T
    coef_dup = _peer_u_pass(row_offsets, h2.reshape(B * S, D // LANES, LANES), gates_t, _pack_table(u_tab))
    return _peer_v_pass(row_offsets, coef_dup, _pack_table(v_tab))


NEG_INF = float("-inf")
RETR_TOKENS = 256


def _top16_rows(pieces, piece_ids):
    vals, ids = [], []
    for _ in range(PEER_TOPK):
        v, d = list(pieces), list(piece_ids)
        while len(v) > 1:
            nv, nd = [], []
            for a in range(0, len(v) - 1, 2):
                take_right = v[a + 1] > v[a]
                nv.append(jnp.where(take_right, v[a + 1], v[a]))
                nd.append(jnp.where(take_right, d[a + 1], d[a]))
            if len(v) % 2:
                nv.append(v[-1])
                nd.append(d[-1])
            v, d = nv, nd
        m = jnp.max(v[0], axis=0, keepdims=True)
        r = jnp.min(jnp.where(v[0] == m, d[0], jnp.iinfo(jnp.int32).max), axis=0, keepdims=True)
        vals.append(m)
        ids.append(r)
        pieces = [jnp.where(dj == r, NEG_INF, sj) for sj, dj in zip(pieces, piece_ids)]
    return vals, ids


def _peer_retrieve_body(x_ref, g_ref, sc_ref, sh_ref, wq_ref, keys_ref, h_ref, idx_ref, gate_ref, q_ref):
    x = x_ref[0]
    h = x * lax.rsqrt(jnp.mean(x * x, axis=-1, keepdims=True) + EPS) * g_ref[...]
    h = h * (1.0 + sc_ref[0]) + sh_ref[0]
    for j in range(h.shape[1] // LANES):
        h_ref[pl.ds(j, ROW_TILE, stride=8), :] = h[:, j * LANES:(j + 1) * LANES]
    q_ref[...] = jnp.dot(h.astype(jnp.bfloat16), wq_ref[...],
                         preferred_element_type=jnp.float32).astype(jnp.bfloat16)
    sub8 = lax.broadcasted_iota(jnp.int32, (8, RETR_TOKENS), 0)
    key_ids = [sub8 + 8 * j for j in range(PEER_NKEYS // 8)]
    cand_ids = ([sub8, sub8 + 8] + [sub8 + a * PEER_TOPK for a in range(1, 8)]
                + [(sub8 + 8) * PEER_TOPK])
    chunks = ROW_TILE // RETR_TOKENS

    def head_chunk(i, carry):
        hd = lax.shift_right_logical(i, chunks.bit_length() - 1)
        ch = i & (chunks - 1)
        tok = pl.ds(pl.multiple_of(ch * RETR_TOKENS, RETR_TOKENS), RETR_TOKENS)
        vals, rows = [], []
        for p in range(2):
            col = pl.multiple_of((hd * 2 + p) * PEER_HALF, PEER_HALF)
            qs = q_ref[tok, pl.ds(col, PEER_HALF)]
            s = lax.dot_general(keys_ref[hd, p], qs, (((1,), (1,)), ((), ())),
                                preferred_element_type=jnp.float32)
            v, r = _top16_rows([s[8 * j:8 * j + 8] for j in range(PEER_NKEYS // 8)], key_ids)
            vals.append(v)
            rows.append(r)
        v1_hi = jnp.concatenate(vals[0][8:], axis=0)
        v2_lo = jnp.concatenate(vals[1][:8], axis=0)
        v2_hi = jnp.concatenate(vals[1][8:], axis=0)
        cand = ([vals[0][0] + v2_lo, vals[0][0] + v2_hi] + [vals[0][a] + v2_lo for a in range(1, 8)]
                + [v1_hi + vals[1][0]])
        top_s, pos = _top16_rows(cand, cand_ids)
        pos = jnp.concatenate(pos, axis=0)
        pa = pos >> 4
        pb = pos & (PEER_TOPK - 1)
        i1 = jnp.zeros_like(pos)
        i2 = jnp.zeros_like(pos)
        for a in range(PEER_TOPK):
            i1 = jnp.where(pa == a, rows[0][a], i1)
            i2 = jnp.where(pb == a, rows[1][a], i2)
        ts = jnp.concatenate(top_s, axis=0)
        e = jnp.exp(ts - top_s[0])
        gates = e / jnp.sum(e, axis=0, keepdims=True)
        out_rows = pl.ds(pl.multiple_of(hd * PEER_TOPK, PEER_TOPK), PEER_TOPK)
        idx_ref[out_rows, tok] = (i1 * PEER_NKEYS + i2) * PEER_CHUNKS
        gate_ref[out_rows, tok] = gates
        return carry

    lax.fori_loop(0, PEER_HEADS * chunks, head_chunk, 0)


def _peer_retrieve(x, g_pre, sc2, sh2, w_q, sub_keys):
    B, S, D = x.shape
    T = B * S
    nq = w_q.shape[1]
    spb = S // ROW_TILE
    tok_spec = pl.BlockSpec((1, ROW_TILE, D), lambda b, s: (b, s, 0))
    mod_spec = pl.BlockSpec((1, 1, D), lambda b, s: (b, 0, 0))
    slot_spec = pl.BlockSpec((PEER_SLOTS, ROW_TILE), lambda b, s: (0, b * spb + s))
    return pl.pallas_call(
        _peer_retrieve_body,
        grid=(B, spb),
        in_specs=[
            tok_spec,
            pl.BlockSpec((1, D), lambda b, s: (0, 0)),
            mod_spec,
            mod_spec,
            pl.BlockSpec((D, nq), lambda b, s: (0, 0)),
            pl.BlockSpec(sub_keys.shape, lambda b, s: (0, 0, 0, 0)),
        ],
        out_specs=[pl.BlockSpec((ROW_TILE * D // LANES, LANES), lambda b, s: (b * spb + s, 0)), slot_spec, slot_spec],
        out_shape=[
            jax.ShapeDtypeStruct((T * D // LANES, LANES), jnp.float32),
            jax.ShapeDtypeStruct((PEER_SLOTS, T), jnp.int32),
            jax.ShapeDtypeStruct((PEER_SLOTS, T), jnp.float32),
        ],
        scratch_shapes=[pltpu.VMEM((ROW_TILE, nq), jnp.bfloat16)],
        compiler_params=pltpu.CompilerParams(vmem_limit_bytes=VMEM_LIMIT),
        name="peer_retrieve",
    )(x, g_pre.reshape(1, D), sc2, sh2, w_q.astype(jnp.bfloat16), sub_keys.astype(jnp.bfloat16))


PEER_SLOTS = PEER_HEADS * PEER_TOPK
PEER_CHUNKS = D_MODEL // 256
PEER_TB = 256


PACK_ROWS = 256


def _bf16_bits(x):
    return pltpu.bitcast(x.astype(jnp.bfloat16).astype(jnp.float32), jnp.uint32)


def _pack_table_body(x_ref, o_ref):
    half = x_ref.shape[1] // 2
    for c in range(PEER_CHUNKS):
        lo = _bf16_bits(x_ref[:, c * LANES:(c + 1) * LANES])
        hi = _bf16_bits(x_ref[:, half + c * LANES:half + (c + 1) * LANES])
        o_ref[pl.ds(c, PACK_ROWS, stride=PEER_CHUNKS), :] = lax.shift_right_logical(lo, jnp.uint32(16)) | hi


def _pack_table(tab):
    E, D = tab.shape
    return pl.pallas_call(
        _pack_table_body,
        grid=(E // PACK_ROWS,),
        in_specs=[pl.BlockSpec((PACK_ROWS, D), lambda i: (i, 0))],
        out_specs=pl.BlockSpec((PACK_ROWS * PEER_CHUNKS, LANES), lambda i: (i, 0)),
        out_shape=jax.ShapeDtypeStruct((E * PEER_CHUNKS, LANES), jnp.uint32),
        name="pack_table",
    )(tab)


IDX_SUB = 16
N_SUB = PEER_TB // IDX_SUB


def _gather_token(stage_ref, half, i, tab_ref, tile_ref):
    for k in range(PEER_SLOTS):
        row = pl.multiple_of(stage_ref[half, i, k], PEER_CHUNKS)
        tile_ref[pl.ds(k * PEER_CHUNKS, PEER_CHUNKS), :] = tab_ref[pl.ds(row, PEER_CHUNKS), :]


def _tile_chunk_f32(tile_ref, c):
    w = tile_ref[pl.ds(c, PEER_SLOTS, stride=PEER_CHUNKS), :]
    return pltpu.bitcast(w, jnp.bfloat16).astype(jnp.float32)


def _token_stream(idx_ref, tab_ref, stage_ref, sems, tiles, consume):
    def stage_copy(sub, half):
        first = pl.multiple_of(jnp.minimum(sub, N_SUB - 1) * IDX_SUB, IDX_SUB)
        return pltpu.make_async_copy(idx_ref.at[pl.ds(first, IDX_SUB), :], stage_ref.at[half], sems.at[half])

    stage_copy(0, 0).start()
    stage_copy(1, 1).start()
    stage_copy(0, 0).wait()
    _gather_token(stage_ref, 0, 0, tab_ref, tiles[0])

    def two_subs(j, carry):
        for half in range(2):
            sub = 2 * j + half
            for i in range(IDX_SUB):
                if i + 1 < IDX_SUB:
                    _gather_token(stage_ref, half, i + 1, tab_ref, tiles[(i + 1) % 2])
                else:
                    stage_copy(sub + 1, 1 - half).wait()
                    stage_copy(sub + 2, half).start()
                    _gather_token(stage_ref, 1 - half, 0, tab_ref, tiles[(i + 1) % 2])
                consume(tiles[i % 2], sub * IDX_SUB + i, half == 1 and i == IDX_SUB - 1)
        return carry
    lax.fori_loop(0, N_SUB // 2, two_subs, 0)
    stage_copy(N_SUB + 1, 1).wait()


def _token_block_cols(t):
    return pl.ds(pl.multiple_of(lax.shift_right_logical(t, 7) * LANES, LANES), LANES)


def _stage_scratch():
    return [
        pltpu.SMEM((2, IDX_SUB, PEER_SLOTS), jnp.int32),
        pltpu.SemaphoreType.DMA((2,)),
        pltpu.VMEM((PEER_SLOTS * PEER_CHUNKS, LANES), jnp.uint32),
        pltpu.VMEM((PEER_SLOTS * PEER_CHUNKS, LANES), jnp.uint32),
    ]


def _peer_u_body(idx_ref, h_ref, gate_ref, tab_ref, coef_ref, stage_ref, sems, tile_a, tile_b, sblk_ref):
    lane = lax.broadcasted_iota(jnp.int32, (2 * PEER_SLOTS, LANES), 1)
    odd = (lax.broadcasted_iota(jnp.int32, (8, LANES), 0) & 1) == 1

    def consume(tile_ref, t, may_end_block):
        acc = jnp.zeros((2 * PEER_SLOTS // 8, 8, LANES), jnp.float32)
        for c in range(PEER_CHUNKS):
            h_lo = jnp.broadcast_to(h_ref[t, pl.ds(c, 1), :], (8, LANES))
            h_hi = jnp.broadcast_to(h_ref[t, pl.ds(PEER_CHUNKS + c, 1), :], (8, LANES))
            pattern = jnp.where(odd, h_hi, h_lo)
            x = _tile_chunk_f32(tile_ref, c).reshape(2 * PEER_SLOTS // 8, 8, LANES)
            acc = acc + x * pattern[None]
        s = jnp.sum(acc.reshape(2 * PEER_SLOTS, LANES), axis=1, keepdims=True)
        tl = t & (LANES - 1)
        sblk_ref[...] = jnp.where(lane == tl, s, sblk_ref[...])
        if not may_end_block:
            return

        @pl.when(tl == LANES - 1)
        def _():
            even_rows = sblk_ref[pl.ds(0, PEER_SLOTS, stride=2), :]
            odd_rows = sblk_ref[pl.ds(1, PEER_SLOTS, stride=2), :]
            a = even_rows + odd_rows
            cols = _token_block_cols(t)
            coef = gate_ref[:, cols] * (0.5 * a * (1.0 + lax.erf(a * (2.0 ** -0.5))))
            sblk_ref[pl.ds(0, PEER_SLOTS, stride=2), :] = coef
            sblk_ref[pl.ds(1, PEER_SLOTS, stride=2), :] = coef
            coef_ref[:, cols] = sblk_ref[...]

    _token_stream(idx_ref, tab_ref, stage_ref, sems, (tile_a, tile_b), consume)


def _peer_u_pass(row_offsets, ht, gates_t, tab):
    T = ht.shape[0]
    return pl.pallas_call(
        _peer_u_body,
        grid=(T // PEER_TB,),
        in_specs=[
            pl.BlockSpec((PEER_TB, PEER_SLOTS), lambda i: (i, 0)),
            pl.BlockSpec((PEER_TB, 8, LANES), lambda i: (i, 0, 0)),
            pl.BlockSpec((PEER_SLOTS, PEER_TB), lambda i: (0, i)),
            pl.BlockSpec(memory_space=pltpu.VMEM),
        ],
        out_specs=pl.BlockSpec((2 * PEER_SLOTS, PEER_TB), lambda i: (0, i)),
        out_shape=jax.ShapeDtypeStruct((2 * PEER_SLOTS, T), jnp.float32),
        scratch_shapes=_stage_scratch() + [pltpu.VMEM((2 * PEER_SLOTS, LANES), jnp.float32)],
        compiler_params=pltpu.CompilerParams(vmem_limit_bytes=VMEM_LIMIT),
        name="peer_u_pass",
    )(row_offsets, ht, gates_t, tab)


def _peer_v_body(idx_ref, coef_ref, tab_ref, y_ref, stage_ref, sems, tile_a, tile_b):
    lane = lax.broadcasted_iota(jnp.int32, (2 * PEER_SLOTS, LANES), 1)

    def consume(tile_ref, t, may_end_block):
        del may_end_block
        tl = t & (LANES - 1)
        coef_blk = coef_ref[:, _token_block_cols(t)]
        ccol = jnp.sum(jnp.where(lane == tl, coef_blk, 0.0), axis=1, keepdims=True)
        ccol = ccol.reshape(2 * PEER_SLOTS // 8, 8, 1)
        for c in range(PEER_CHUNKS):
            x = _tile_chunk_f32(tile_ref, c).reshape(2 * PEER_SLOTS // 8, 8, LANES)
            part = jnp.sum(x * ccol, axis=0)
            part = part + pltpu.roll(part, 4, axis=0)
            part = part + pltpu.roll(part, 2, axis=0)
            y_ref[t, pl.ds(c, 1), :] = part[0:1]
            y_ref[t, pl.ds(PEER_CHUNKS + c, 1), :] = part[1:2]

    _token_stream(idx_ref, tab_ref, stage_ref, sems, (tile_a, tile_b), consume)


def _peer_v_pass(row_offsets, coef_dup, tab):
    T = row_offsets.shape[0]
    return pl.pallas_call(
        _peer_v_body,
        grid=(T // PEER_TB,),
        in_specs=[
            pl.BlockSpec((PEER_TB, PEER_SLOTS), lambda i: (i, 0)),
            pl.BlockSpec((2 * PEER_SLOTS, PEER_TB), lambda i: (0, i)),
            pl.BlockSpec(memory_space=pltpu.VMEM),
        ],
        out_specs=pl.BlockSpec((PEER_TB, 8, LANES), lambda i: (i, 0, 0)),
        out_shape=jax.ShapeDtypeStruct((T, 8, LANES), jnp.float32),
        scratch_shapes=_stage_scratch(),
        compiler_params=pltpu.CompilerParams(vmem_limit_bytes=VMEM_LIMIT),
        name="peer_v_pass",
    )(row_offsets, coef_dup, tab)


def _post_residual_body(x_ref, y_ref, g_ref, gate_ref, o_ref):
    pieces = [y_ref[pl.ds(j, ROW_TILE, stride=8), :] for j in range(x_ref.shape[2] // LANES)]
    sq = pieces[0] * pieces[0]
    for p in pieces[1:]:
        sq = sq + p * p
    inv = lax.rsqrt(jnp.sum(sq, axis=-1, keepdims=True) * (1.0 / x_ref.shape[2]) + EPS)
    for j, p in enumerate(pieces):
        cols = slice(j * LANES, (j + 1) * LANES)
        o_ref[0, :, cols] = x_ref[0, :, cols] + gate_ref[0, :, cols] * (p * inv * g_ref[:, cols])


def _post_residual(x, y, g, gate):
    B, S, D = x.shape
    pieces = D // LANES
    spb = S // ROW_TILE
    row_spec = pl.BlockSpec((1, ROW_TILE, D), lambda b, s: (b, s, 0))
    return pl.pallas_call(
        _post_residual_body,
        grid=(B, spb),
        in_specs=[
            row_spec,
            pl.BlockSpec((ROW_TILE * pieces, LANES), lambda b, s: (b * spb + s, 0)),
            pl.BlockSpec((1, D), lambda b, s: (0, 0)),
            pl.BlockSpec((1, 1, D), lambda b, s: (b, 0, 0)),
        ],
        out_specs=row_spec,
        out_shape=jax.ShapeDtypeStruct((B, S, D), x.dtype),
        name="post_residual",
    )(x, y.reshape(B * S * pieces, LANES), g.reshape(1, D), gate)


def kernel(x, c, w_mod, b_mod, g_pre_mix, g_post_mix, w_in, w_alpha_f, b_alpha_f, w_alpha_b, b_alpha_b,
           g_gla_head, w_pool, pool_scale, w_out, g_pre_ffn, g_post_ffn, w_peer_q, peer_sub_keys,
           peer_u, peer_v):
    for l in range(w_mod.shape[0]):
        mod = _modulation(c, w_mod[l], b_mod[l])
        sh1, sc1, gt1, sh2, sc2, gt2 = [m[:, None, :] for m in jnp.split(mod, N_MOD, axis=-1)]
        z = _in_proj(x, g_pre_mix[l], sc1, sh1, w_in[l])
        o_gla = _gla(z, w_alpha_f[l], b_alpha_f[l], w_alpha_b[l], b_alpha_b[l], g_gla_head[l])
        x = _mix_out(x, o_gla, z, w_pool[l], pool_scale[l], w_out[l], g_post_mix[l], gt1)
        y = _peer_ffn(x, g_pre_ffn[l], sc2, sh2, w_peer_q[l], peer_sub_keys[l], peer_u[l], peer_v[l])
        x = _post_residual(x, y, g_post_ffn[l], gt2)
    return x
```

```python
import jax
import jax.numpy as jnp
from jax import lax
from jax.experimental import pallas as pl
from jax.experimental.pallas import tpu as pltpu

D_MODEL = 1024
GLA_WIDTH = 512
POOL_WIDTH = 512
GLA_HEADS = 4
GLA_DV = 128
GLA_DK = 64
GLA_KEY_WIDTH = 256
GLA_GATE_RANK = 16
GLA_TAU = 16.0
GLA_CHUNK = 64
POOL_WINDOWS = (2, 4, 8, 16)
POOL_GROUP_WIDTH = 128
PEER_HEADS = 8
PEER_NKEYS = 128
PEER_HALF = 128
PEER_TOPK = 16
N_MOD = 6
EPS = 1e-6

LANES = 128
VMEM_LIMIT = 48 * 1024 * 1024
ROW_TILE = 512

Z_Q, Z_K, Z_V, Z_R, Z_P, Z_A = 0, 256, 512, 1024, 1536, 2048
Z_COLS = 2176
POOL_HALO = 8
GLA_UNROLL = 8


def _mod_body(c_ref, w_ref, b_ref, o_ref):
    c = c_ref[...]
    cond = c * (1.0 / (1.0 + jnp.exp(-c)))
    o_ref[...] = jnp.dot(cond.astype(jnp.bfloat16), w_ref[...].astype(jnp.bfloat16),
                         preferred_element_type=jnp.float32) + b_ref[...]


def _modulation(c, w_mod, b_mod):
    B, D = c.shape
    n = w_mod.shape[1]
    return pl.pallas_call(
        _mod_body,
        grid=(n // D,),
        in_specs=[
            pl.BlockSpec((B, D), lambda j: (0, 0)),
            pl.BlockSpec((D, D), lambda j: (0, j)),
            pl.BlockSpec((1, D), lambda j: (0, j)),
        ],
        out_specs=pl.BlockSpec((B, D), lambda j: (0, j)),
        out_shape=jax.ShapeDtypeStruct((B, n), jnp.float32),
        name="modulation",
    )(c, w_mod, b_mod.reshape(1, n))


def _in_proj_body(x_ref, g_ref, sc_ref, sh_ref, w_ref, z_ref):
    x = x_ref[0]
    h = x * lax.rsqrt(jnp.mean(x * x, axis=-1, keepdims=True) + EPS) * g_ref[...]
    h = h * (1.0 + sc_ref[0]) + sh_ref[0]
    z_ref[0] = jnp.dot(h.astype(jnp.bfloat16), w_ref[...], preferred_element_type=jnp.float32)


def _in_proj(x, g, sc, sh, w_in):
    B, S, D = x.shape
    w = jnp.concatenate([w_in[:, :Z_P], w_in[:, Z_P + 32:], w_in[:, Z_P:Z_P + 32],
                         jnp.zeros((D, Z_COLS - Z_A - 32), w_in.dtype)], axis=1).astype(jnp.bfloat16)
    mod_spec = pl.BlockSpec((1, 1, D), lambda b, s: (b, 0, 0))
    return pl.pallas_call(
        _in_proj_body,
        grid=(B, S // ROW_TILE),
        in_specs=[
            pl.BlockSpec((1, ROW_TILE, D), lambda b, s: (b, s, 0)),
            pl.BlockSpec((1, D), lambda b, s: (0, 0)),
            mod_spec,
            mod_spec,
            pl.BlockSpec((D, Z_COLS), lambda b, s: (0, 0)),
        ],
        out_specs=pl.BlockSpec((1, ROW_TILE, Z_COLS), lambda b, s: (b, s, 0)),
        out_shape=jax.ShapeDtypeStruct((B, S, Z_COLS), jnp.float32),
        compiler_params=pltpu.CompilerParams(vmem_limit_bytes=VMEM_LIMIT),
        name="in_proj",
    )(x, g.reshape(1, D), sc, sh, w)


def _log_sigmoid(x):
    return jnp.minimum(x, 0.0) - jnp.log1p(jnp.exp(-jnp.abs(x)))


def _bf16_dot(a, b, dims):
    return lax.dot_general(a.astype(jnp.bfloat16), b.astype(jnp.bfloat16), (dims, ((), ())),
                           preferred_element_type=jnp.float32)


def _gla_body(q_ref, k_ref, v_ref, r_ref, a_ref, wf_ref, bf_ref, wb_ref, bb_ref, gh_ref, o_ref,
              of_ref, st_ref):
    C = GLA_CHUNK
    n_chunks = q_ref.shape[1] // C
    row = lax.broadcasted_iota(jnp.int32, (C, C), 0)
    col = lax.broadcasted_iota(jnp.int32, (C, C), 1)
    lane = lax.broadcasted_iota(jnp.int32, (1, LANES), 1)
    head_lanes = [lane < GLA_DK, lane >= GLA_DK]

    def chunk_group(chunk_ids, w_ref, b_ref, cum_mask, keep_mask, total_row, emit):
        rows = [pl.ds(pl.multiple_of(n * C, C), C) for n in chunk_ids]
        log_a = [_log_sigmoid(_bf16_dot(a_ref[0, r, :], w_ref[0], ((1,), (0,))) + b_ref[0]) * (1.0 / GLA_TAU)
                 for r in rows]
        b = [jnp.dot(cum_mask, la, precision=lax.Precision.HIGHEST, preferred_element_type=jnp.float32)
             for la in log_a]
        qm, k_inv, k_end, gamma = [], [], [], []
        for r, bc in zip(rows, b):
            b_tot = bc[total_row:total_row + 1, :]
            q_dec = q_ref[0, r, :] * (GLA_DK ** -0.5) * jnp.exp(bc)
            k = k_ref[0, r, :]
            qm.append([jnp.where(head_lanes[hd], q_dec, 0.0) for hd in range(2)])
            k_inv.append(k * jnp.exp(-bc))
            k_end.append(k * jnp.exp(b_tot - bc))
            gamma.append(jnp.exp(b_tot))
        v = [[v_ref[0, r, hd * GLA_DV:(hd + 1) * GLA_DV] for hd in range(2)] for r in rows]
        scores = [[jnp.where(keep_mask, _bf16_dot(qm[u][hd], k_inv[u], ((1,), (1,))), 0.0) for hd in range(2)]
                  for u in range(len(rows))]
        o_intra = [[_bf16_dot(scores[u][hd], v[u][hd], ((1,), (0,))) for hd in range(2)] for u in range(len(rows))]
        kv = [[_bf16_dot(v[u][hd], k_end[u], ((0,), (0,))) for hd in range(2)] for u in range(len(rows))]
        for u, r in enumerate(rows):
            for hd in range(2):
                state = st_ref[hd]
                emit(r, hd, o_intra[u][hd] + _bf16_dot(qm[u][hd], state, ((1,), (1,))))
                st_ref[hd] = state * gamma[u] + kv[u][hd]

    st_ref[...] = jnp.zeros_like(st_ref)
    prefix = (row >= col).astype(jnp.float32)

    def emit_fwd(rows, hd, o):
        of_ref[rows, hd * GLA_DV:(hd + 1) * GLA_DV] = o

    def fwd(i, carry):
        chunk_group([i * GLA_UNROLL + u for u in range(GLA_UNROLL)], wf_ref, bf_ref, prefix, row >= col, C - 1,
                    emit_fwd)
        return carry
    lax.fori_loop(0, n_chunks // GLA_UNROLL, fwd, 0)

    st_ref[...] = jnp.zeros_like(st_ref)
    suffix = (row <= col).astype(jnp.float32)

    def emit_bwd(rows, hd, o):
        cols = slice(hd * GLA_DV, (hd + 1) * GLA_DV)
        tot = of_ref[rows, cols] + o
        y = tot * lax.rsqrt(jnp.mean(tot * tot, axis=-1, keepdims=True) + EPS) * gh_ref[0, :, cols]
        r = r_ref[0, rows, cols]
        o_ref[0, rows, cols] = y * (r * (1.0 / (1.0 + jnp.exp(-r))))

    def bwd(i, carry):
        chunk_group([n_chunks - 1 - (i * GLA_UNROLL + u) for u in range(GLA_UNROLL)], wb_ref, bb_ref, suffix,
                    row < col, 0, emit_bwd)
        return carry
    lax.fori_loop(0, n_chunks // GLA_UNROLL, bwd, 0)


def _gla(z, w_af, b_af, w_ab, b_ab, g_head):
    B, S, _ = z.shape
    pairs = GLA_HEADS // 2

    def pad_w(w, first_row):
        wp = jnp.zeros((pairs, LANES, LANES), jnp.float32)
        return wp.at[:, first_row:first_row + GLA_GATE_RANK, :].set(
            jnp.transpose(w.reshape(GLA_GATE_RANK, pairs, LANES), (1, 0, 2)))

    def lane_blk(off):
        return pl.BlockSpec((1, S, LANES), lambda b, j: (b, 0, off // LANES + j))

    def wide_blk(off):
        return pl.BlockSpec((1, S, 2 * LANES), lambda b, j: (b, 0, off // (2 * LANES) + j))

    w_spec = pl.BlockSpec((1, LANES, LANES), lambda b, j: (j, 0, 0))
    b_spec = pl.BlockSpec((1, 1, LANES), lambda b, j: (j, 0, 0))
    return pl.pallas_call(
        _gla_body,
        grid=(B, pairs),
        in_specs=[
            lane_blk(Z_Q), lane_blk(Z_K), wide_blk(Z_V), wide_blk(Z_R),
            pl.BlockSpec((1, S, LANES), lambda b, j: (b, 0, Z_A // LANES)),
            w_spec, b_spec, w_spec, b_spec,
            pl.BlockSpec((1, 1, 2 * LANES), lambda b, j: (j, 0, 0)),
        ],
        out_specs=pl.BlockSpec((1, S, 2 * LANES), lambda b, j: (b, 0, j)),
        out_shape=jax.ShapeDtypeStruct((B, S, GLA_WIDTH), jnp.float32),
        scratch_shapes=[
            pltpu.VMEM((S, 2 * LANES), jnp.float32),
            pltpu.VMEM((2, GLA_DV, LANES), jnp.float32),
        ],
        compiler_params=pltpu.CompilerParams(vmem_limit_bytes=VMEM_LIMIT),
        name="gla",
    )(z, z, z, z, z,
      pad_w(w_af, 0), b_af.reshape(pairs, 1, LANES),
      pad_w(w_ab, GLA_GATE_RANK), b_ab.reshape(pairs, 1, LANES),
      g_head.reshape(pairs, 1, 2 * LANES))


def _mix_out_body(x_ref, o_ref, p_ref, pp_ref, pn_ref, wp_ref, ps_ref, wo_ref, g_ref, gate_ref, out_ref, pe_ref):
    s = pl.program_id(1)
    n_s = pl.num_programs(1)
    seq_len = n_s * ROW_TILE
    pe_ref[pl.ds(0, POOL_HALO), :] = jnp.where(s > 0, pp_ref[0], 0.0)
    pe_ref[pl.ds(POOL_HALO, ROW_TILE), :] = p_ref[0]
    pe_ref[pl.ds(POOL_HALO + ROW_TILE, POOL_HALO), :] = jnp.where(s < n_s - 1, pn_ref[0], 0.0)
    pos = s * ROW_TILE + lax.broadcasted_iota(jnp.int32, (ROW_TILE, POOL_GROUP_WIDTH), 0)
    mix = jnp.dot(o_ref[0].astype(jnp.bfloat16), wo_ref[pl.ds(0, GLA_WIDTH), :],
                  preferred_element_type=jnp.float32)
    for gi, w in enumerate(POOL_WINDOWS):
        cols = pl.ds(gi * POOL_GROUP_WIDTH, POOL_GROUP_WIDTH)
        total = pe_ref[pl.ds(POOL_HALO - w // 2, ROW_TILE), cols]
        for d in range(-w // 2 + 1, w // 2):
            total = total + pe_ref[pl.ds(POOL_HALO + d, ROW_TILE), cols]
        count = (jnp.minimum(pos + w // 2, seq_len) - jnp.maximum(pos - w // 2, 0)).astype(jnp.float32)
        pooled = total / count - pe_ref[pl.ds(POOL_HALO, ROW_TILE), cols]
        y = jnp.dot(pooled.astype(jnp.bfloat16), wp_ref[gi], preferred_element_type=jnp.float32)
        y = y * ps_ref[:, gi * POOL_GROUP_WIDTH:(gi + 1) * POOL_GROUP_WIDTH]
        mix = mix + jnp.dot(y.astype(jnp.bfloat16),
                            wo_ref[pl.ds(GLA_WIDTH + gi * POOL_GROUP_WIDTH, POOL_GROUP_WIDTH), :],
                            preferred_element_type=jnp.float32)
    yn = mix * lax.rsqrt(jnp.mean(mix * mix, axis=-1, keepdims=True) + EPS) * g_ref[...]
    out_ref[0] = x_ref[0] + gate_ref[0] * yn


def _mix_out(x, o_gla, z, w_pool, pool_scale, w_out, g, gate):
    B, S, D = x.shape
    halo_per_tile = ROW_TILE // POOL_HALO
    last_halo = S // POOL_HALO - 1
    row_spec = pl.BlockSpec((1, ROW_TILE, D), lambda b, s: (b, s, 0))
    return pl.pallas_call(
        _mix_out_body,
        grid=(B, S // ROW_TILE),
        in_specs=[
            row_spec,
            pl.BlockSpec((1, ROW_TILE, GLA_WIDTH), lambda b, s: (b, s, 0)),
            pl.BlockSpec((1, ROW_TILE, POOL_WIDTH), lambda b, s: (b, s, Z_P // POOL_WIDTH)),
            pl.BlockSpec((1, POOL_HALO, POOL_WIDTH),
                         lambda b, s: (b, jnp.maximum(s * halo_per_tile - 1, 0), Z_P // POOL_WIDTH)),
            pl.BlockSpec((1, POOL_HALO, POOL_WIDTH),
                         lambda b, s: (b, jnp.minimum((s + 1) * halo_per_tile, last_halo), Z_P // POOL_WIDTH)),
            pl.BlockSpec(w_pool.shape, lambda b, s: (0, 0, 0)),
            pl.BlockSpec((1, POOL_WIDTH), lambda b, s: (0, 0)),
            pl.BlockSpec((D, D), lambda b, s: (0, 0)),
            pl.BlockSpec((1, D), lambda b, s: (0, 0)),
            pl.BlockSpec((1, 1, D), lambda b, s: (b, 0, 0)),
        ],
        out_specs=row_spec,
        out_shape=jax.ShapeDtypeStruct((B, S, D), jnp.float32),
        scratch_shapes=[pltpu.VMEM((ROW_TILE + 2 * POOL_HALO, POOL_WIDTH), jnp.float32)],
        compiler_params=pltpu.CompilerParams(vmem_limit_bytes=VMEM_LIMIT),
        name="mix_out",
    )(x, o_gla, z, z, z, w_pool.astype(jnp.bfloat16), pool_scale.reshape(1, POOL_WIDTH),
      w_out.astype(jnp.bfloat16), g.reshape(1, D), gate)


def _peer_ffn(x, g_pre, sc2, sh2, w_q, sub_keys, u_tab, v_tab):
    B, S, D = x.shape
    h2, offsets_t, gates_t = _peer_retrieve(x, g_pre, sc2, sh2, w_q, sub_keys)
    row_offsets = offsets_t.T
    coef_dup = _peer_u_pass(row_offsets, h2.reshape(B * S, D // LANES, LANES), gates_t, _pack_table(u_tab))
    return _peer_v_pass(row_offsets, coef_dup, _pack_table(v_tab))


NEG_INF = float("-inf")
RETR_TOKENS = 512


def _top16_rows(pieces, piece_ids):
    vals, ids = [], []
    for _ in range(PEER_TOPK):
        v, d = list(pieces), list(piece_ids)
        while len(v) > 1:
            nv, nd = [], []
            for a in range(0, len(v) - 1, 2):
                take_right = v[a + 1] > v[a]
                nv.append(jnp.where(take_right, v[a + 1], v[a]))
                nd.append(jnp.where(take_right, d[a + 1], d[a]))
            if len(v) % 2:
                nv.append(v[-1])
                nd.append(d[-1])
            v, d = nv, nd
        m = jnp.max(v[0], axis=0, keepdims=True)
        r = jnp.min(jnp.where(v[0] == m, d[0], jnp.iinfo(jnp.int32).max), axis=0, keepdims=True)
        vals.append(m)
        ids.append(r)
        pieces = [jnp.where(dj == r, NEG_INF, sj) for sj, dj in zip(pieces, piece_ids)]
    return vals, ids


def _peer_retrieve_body(x_ref, g_ref, sc_ref, sh_ref, wq_ref, keys_ref, h_ref, idx_ref, gate_ref, q_ref):
    x = x_ref[0]
    h = x * lax.rsqrt(jnp.mean(x * x, axis=-1, keepdims=True) + EPS) * g_ref[...]
    h = h * (1.0 + sc_ref[0]) + sh_ref[0]
    for j in range(h.shape[1] // LANES):
        h_ref[pl.ds(j, ROW_TILE, stride=8), :] = h[:, j * LANES:(j + 1) * LANES]
    q_ref[...] = jnp.dot(h.astype(jnp.bfloat16), wq_ref[...],
                         preferred_element_type=jnp.float32).astype(jnp.bfloat16)
    sub8 = lax.broadcasted_iota(jnp.int32, (8, RETR_TOKENS), 0)
    key_ids = [sub8 + 8 * j for j in range(PEER_NKEYS // 8)]
    cand_ids = ([sub8, sub8 + 8] + [sub8 + a * PEER_TOPK for a in range(1, 8)]
                + [(sub8 + 8) * PEER_TOPK])
    chunks = ROW_TILE // RETR_TOKENS

    def head_chunk(i, carry):
        hd = lax.shift_right_logical(i, chunks.bit_length() - 1)
        ch = i & (chunks - 1)
        tok = pl.ds(pl.multiple_of(ch * RETR_TOKENS, RETR_TOKENS), RETR_TOKENS)
        vals, rows = [], []
        for p in range(2):
            col = pl.multiple_of((hd * 2 + p) * PEER_HALF, PEER_HALF)
            qs = q_ref[tok, pl.ds(col, PEER_HALF)]
            s = lax.dot_general(keys_ref[hd, p], qs, (((1,), (1,)), ((), ())),
                                preferred_element_type=jnp.float32)
            v, r = _top16_rows([s[8 * j:8 * j + 8] for j in range(PEER_NKEYS // 8)], key_ids)
            vals.append(v)
            rows.append(r)
        v1_hi = jnp.concatenate(vals[0][8:], axis=0)
        v2_lo = jnp.concatenate(vals[1][:8], axis=0)
        v2_hi = jnp.concatenate(vals[1][8:], axis=0)
        cand = ([vals[0][0] + v2_lo, vals[0][0] + v2_hi] + [vals[0][a] + v2_lo for a in range(1, 8)]
                + [v1_hi + vals[1][0]])
        top_s, pos = _top16_rows(cand, cand_ids)
        pos = jnp.concatenate(pos, axis=0)
        pa = pos >> 4
        pb = pos & (PEER_TOPK - 1)
        i1 = jnp.zeros_like(pos)
        i2 = jnp.zeros_like(pos)
        for a in range(PEER_TOPK):
            i1 = jnp.where(pa == a, rows[0][a], i1)
            i2 = jnp.where(pb == a, rows[1][a], i2)
        ts = jnp.concatenate(top_s, axis=0)
        e = jnp.exp(ts - top_s[0])
        gates = e / jnp.sum(e, axis=0, keepdims=True)
        out_rows = pl.ds(pl.multiple_of(hd * PEER_TOPK, PEER_TOPK), PEER_TOPK)
        idx_ref[out_rows, tok] = (i1 * PEER_NKEYS + i2) * PEER_CHUNKS
        gate_ref[out_rows, tok] = gates
        return carry

    lax.fori_loop(0, PEER_HEADS * chunks, head_chunk, 0)


def _peer_retrieve(x, g_pre, sc2, sh2, w_q, sub_keys):
    B, S, D = x.shape
    T = B * S
    nq = w_q.shape[1]
    spb = S // ROW_TILE
    tok_spec = pl.BlockSpec((1, ROW_TILE, D), lambda b, s: (b, s, 0))
    mod_spec = pl.BlockSpec((1, 1, D), lambda b, s: (b, 0, 0))
    slot_spec = pl.BlockSpec((PEER_SLOTS, ROW_TILE), lambda b, s: (0, b * spb + s))
    return pl.pallas_call(
        _peer_retrieve_body,
        grid=(B, spb),
        in_specs=[
            tok_spec,
            pl.BlockSpec((1, D), lambda b, s: (0, 0)),
            mod_spec,
            mod_spec,
            pl.BlockSpec((D, nq), lambda b, s: (0, 0)),
            pl.BlockSpec(sub_keys.shape, lambda b, s: (0, 0, 0, 0)),
        ],
        out_specs=[pl.BlockSpec((ROW_TILE * D // LANES, LANES), lambda b, s: (b * spb + s, 0)), slot_spec, slot_spec],
        out_shape=[
            jax.ShapeDtypeStruct((T * D // LANES, LANES), jnp.float32),
            jax.ShapeDtypeStruct((PEER_SLOTS, T), jnp.int32),
            jax.ShapeDtypeStruct((PEER_SLOTS, T), jnp.float32),
        ],
        scratch_shapes=[pltpu.VMEM((ROW_TILE, nq), jnp.bfloat16)],
        compiler_params=pltpu.CompilerParams(vmem_limit_bytes=VMEM_LIMIT),
        name="peer_retrieve",
    )(x, g_pre.reshape(1, D), sc2, sh2, w_q.astype(jnp.bfloat16), sub_keys.astype(jnp.bfloat16))


PEER_SLOTS = PEER_HEADS * PEER_TOPK
PEER_CHUNKS = D_MODEL // 256
PEER_TB = 256


PACK_ROWS = 256


def _bf16_bits(x):
    return pltpu.bitcast(x.astype(jnp.bfloat16).astype(jnp.float32), jnp.uint32)


def _pack_table_body(x_ref, o_ref):
    half = x_ref.shape[1] // 2
    for c in range(PEER_CHUNKS):
        lo = _bf16_bits(x_ref[:, c * LANES:(c + 1) * LANES])
        hi = _bf16_bits(x_ref[:, half + c * LANES:half + (c + 1) * LANES])
        o_ref[pl.ds(c, PACK_ROWS, stride=PEER_CHUNKS), :] = lax.shift_right_logical(lo, jnp.uint32(16)) | hi


def _pack_table(tab):
    E, D = tab.shape
    return pl.pallas_call(
        _pack_table_body,
        grid=(E // PACK_ROWS,),
        in_specs=[pl.BlockSpec((PACK_ROWS, D), lambda i: (i, 0))],
        out_specs=pl.BlockSpec((PACK_ROWS * PEER_CHUNKS, LANES), lambda i: (i, 0)),
        out_shape=jax.ShapeDtypeStruct((E * PEER_CHUNKS, LANES), jnp.uint32),
        name="pack_table",
    )(tab)


IDX_SUB = 16
N_SUB = PEER_TB // IDX_SUB


def _gather_token(stage_ref, half, i, tab_ref, tile_ref):
    for k in range(PEER_SLOTS):
        row = pl.multiple_of(stage_ref[half, i, k], PEER_CHUNKS)
        tile_ref[pl.ds(k * PEER_CHUNKS, PEER_CHUNKS), :] = tab_ref[pl.ds(row, PEER_CHUNKS), :]


def _tile_chunk_f32(tile_ref, c):
    w = tile_ref[pl.ds(c, PEER_SLOTS, stride=PEER_CHUNKS), :]
    return pltpu.bitcast(w, jnp.bfloat16).astype(jnp.float32)


def _token_stream(idx_ref, tab_ref, stage_ref, sems, tiles, consume):
    def stage_copy(sub, half):
        first = pl.multiple_of(jnp.minimum(sub, N_SUB - 1) * IDX_SUB, IDX_SUB)
        return pltpu.make_async_copy(idx_ref.at[pl.ds(first, IDX_SUB), :], stage_ref.at[half], sems.at[half])

    stage_copy(0, 0).start()
    stage_copy(1, 1).start()
    stage_copy(0, 0).wait()
    _gather_token(stage_ref, 0, 0, tab_ref, tiles[0])

    def two_subs(j, carry):
        for half in range(2):
            sub = 2 * j + half
            for i in range(IDX_SUB):
                if i + 1 < IDX_SUB:
                    _gather_token(stage_ref, half, i + 1, tab_ref, tiles[(i + 1) % 2])
                else:
                    stage_copy(sub + 1, 1 - half).wait()
                    stage_copy(sub + 2, half).start()
                    _gather_token(stage_ref, 1 - half, 0, tab_ref, tiles[(i + 1) % 2])
                consume(tiles[i % 2], sub * IDX_SUB + i, half == 1 and i == IDX_SUB - 1)
        return carry
    lax.fori_loop(0, N_SUB // 2, two_subs, 0)
    stage_copy(N_SUB + 1, 1).wait()


def _token_block_cols(t):
    return pl.ds(pl.multiple_of(lax.shift_right_logical(t, 7) * LANES, LANES), LANES)


def _stage_scratch():
    return [
        pltpu.SMEM((2, IDX_SUB, PEER_SLOTS), jnp.int32),
        pltpu.SemaphoreType.DMA((2,)),
        pltpu.VMEM((PEER_SLOTS * PEER_CHUNKS, LANES), jnp.uint32),
        pltpu.VMEM((PEER_SLOTS * PEER_CHUNKS, LANES), jnp.uint32),
    ]


def _peer_u_body(idx_ref, h_ref, gate_ref, tab_ref, coef_ref, stage_ref, sems, tile_a, tile_b, sblk_ref):
    lane = lax.broadcasted_iota(jnp.int32, (2 * PEER_SLOTS, LANES), 1)
    odd = (lax.broadcasted_iota(jnp.int32, (8, LANES), 0) & 1) == 1

    def consume(tile_ref, t, may_end_block):
        acc = jnp.zeros((2 * PEER_SLOTS // 8, 8, LANES), jnp.float32)
        for c in range(PEER_CHUNKS):
            h_lo = jnp.broadcast_to(h_ref[t, pl.ds(c, 1), :], (8, LANES))
            h_hi = jnp.broadcast_to(h_ref[t, pl.ds(PEER_CHUNKS + c, 1), :], (8, LANES))
            pattern = jnp.where(odd, h_hi, h_lo)
            x = _tile_chunk_f32(tile_ref, c).reshape(2 * PEER_SLOTS // 8, 8, LANES)
            acc = acc + x * pattern[None]
        s = jnp.sum(acc.reshape(2 * PEER_SLOTS, LANES), axis=1, keepdims=True)
        tl = t & (LANES - 1)
        sblk_ref[...] = jnp.where(lane == tl, s, sblk_ref[...])
        if not may_end_block:
            return

        @pl.when(tl == LANES - 1)
        def _():
            even_rows = sblk_ref[pl.ds(0, PEER_SLOTS, stride=2), :]
            odd_rows = sblk_ref[pl.ds(1, PEER_SLOTS, stride=2), :]
            a = even_rows + odd_rows
            cols = _token_block_cols(t)
            coef = gate_ref[:, cols] * (0.5 * a * (1.0 + lax.erf(a * (2.0 ** -0.5))))
            sblk_ref[pl.ds(0, PEER_SLOTS, stride=2), :] = coef
            sblk_ref[pl.ds(1, PEER_SLOTS, stride=2), :] = coef
            coef_ref[:, cols] = sblk_ref[...]

    _token_stream(idx_ref, tab_ref, stage_ref, sems, (tile_a, tile_b), consume)


def _peer_u_pass(row_offsets, ht, gates_t, tab):
    T = ht.shape[0]
    return pl.pallas_call(
        _peer_u_body,
        grid=(T // PEER_TB,),
        in_specs=[
            pl.BlockSpec((PEER_TB, PEER_SLOTS), lambda i: (i, 0)),
            pl.BlockSpec((PEER_TB, 8, LANES), lambda i: (i, 0, 0)),
            pl.BlockSpec((PEER_SLOTS, PEER_TB), lambda i: (0, i)),
            pl.BlockSpec(memory_space=pltpu.VMEM),
        ],
        out_specs=pl.BlockSpec((2 * PEER_SLOTS, PEER_TB), lambda i: (0, i)),
        out_shape=jax.ShapeDtypeStruct((2 * PEER_SLOTS, T), jnp.float32),
        scratch_shapes=_stage_scratch() + [pltpu.VMEM((2 * PEER_SLOTS, LANES), jnp.float32)],
        compiler_params=pltpu.CompilerParams(vmem_limit_bytes=VMEM_LIMIT),
        name="peer_u_pass",
    )(row_offsets, ht, gates_t, tab)


def _peer_v_body(idx_ref, coef_ref, tab_ref, y_ref, stage_ref, sems, tile_a, tile_b):
    lane = lax.broadcasted_iota(jnp.int32, (2 * PEER_SLOTS, LANES), 1)

    def consume(tile_ref, t, may_end_block):
        del may_end_block
        tl = t & (LANES - 1)
        coef_blk = coef_ref[:, _token_block_cols(t)]
        ccol = jnp.sum(jnp.where(lane == tl, coef_blk, 0.0), axis=1, keepdims=True)
        ccol = ccol.reshape(2 * PEER_SLOTS // 8, 8, 1)
        for c in range(PEER_CHUNKS):
            x = _tile_chunk_f32(tile_ref, c).reshape(2 * PEER_SLOTS // 8, 8, LANES)
            part = jnp.sum(x * ccol, axis=0)
            part = part + pltpu.roll(part, 4, axis=0)
            part = part + pltpu.roll(part, 2, axis=0)
            y_ref[t, pl.ds(c, 1), :] = part[0:1]
            y_ref[t, pl.ds(PEER_CHUNKS + c, 1), :] = part[1:2]

    _token_stream(idx_ref, tab_ref, stage_ref, sems, (tile_a, tile_b), consume)


def _peer_v_pass(row_offsets, coef_dup, tab):
    T = row_offsets.shape[0]
    return pl.pallas_call(
        _peer_v_body,
        grid=(T // PEER_TB,),
        in_specs=[
            pl.BlockSpec((PEER_TB, PEER_SLOTS), lambda i: (i, 0)),
            pl.BlockSpec((2 * PEER_SLOTS, PEER_TB), lambda i: (0, i)),
            pl.BlockSpec(memory_space=pltpu.VMEM),
        ],
        out_specs=pl.BlockSpec((PEER_TB, 8, LANES), lambda i: (i, 0, 0)),
        out_shape=jax.ShapeDtypeStruct((T, 8, LANES), jnp.float32),
        scratch_shapes=_stage_scratch(),
        compiler_params=pltpu.CompilerParams(vmem_limit_bytes=VMEM_LIMIT),
        name="peer_v_pass",
    )(row_offsets, coef_dup, tab)


def _post_residual_body(x_ref, y_ref, g_ref, gate_ref, o_ref):
    pieces = [y_ref[pl.ds(j, ROW_TILE, stride=8), :] for j in range(x_ref.shape[2] // LANES)]
    sq = pieces[0] * pieces[0]
    for p in pieces[1:]:
        sq = sq + p * p
    inv = lax.rsqrt(jnp.sum(sq, axis=-1, keepdims=True) * (1.0 / x_ref.shape[2]) + EPS)
    for j, p in enumerate(pieces):
        cols = slice(j * LANES, (j + 1) * LANES)
        o_ref[0, :, cols] = x_ref[0, :, cols] + gate_ref[0, :, cols] * (p * inv * g_ref[:, cols])


def _post_residual(x, y, g, gate):
    B, S, D = x.shape
    pieces = D // LANES
    spb = S // ROW_TILE
    row_spec = pl.BlockSpec((1, ROW_TILE, D), lambda b, s: (b, s, 0))
    return pl.pallas_call(
        _post_residual_body,
        grid=(B, spb),
        in_specs=[
            row_spec,
            pl.BlockSpec((ROW_TILE * pieces, LANES), lambda b, s: (b * spb + s, 0)),
            pl.BlockSpec((1, D), lambda b, s: (0, 0)),
            pl.BlockSpec((1, 1, D), lambda b, s: (b, 0, 0)),
        ],
        out_specs=row_spec,
        out_shape=jax.ShapeDtypeStruct((B, S, D), x.dtype),
        name="post_residual",
    )(x, y.reshape(B * S * pieces, LANES), g.reshape(1, D), gate)


def kernel(x, c, w_mod, b_mod, g_pre_mix, g_post_mix, w_in, w_alpha_f, b_alpha_f, w_alpha_b, b_alpha_b,
           g_gla_head, w_pool, pool_scale, w_out, g_pre_ffn, g_post_ffn, w_peer_q, peer_sub_keys,
           peer_u, peer_v):
    for l in range(w_mod.shape[0]):
        mod = _modulation(c, w_mod[l], b_mod[l])
        sh1, sc1, gt1, sh2, sc2, gt2 = [m[:, None, :] for m in jnp.split(mod, N_MOD, axis=-1)]
        z = _in_proj(x, g_pre_mix[l], sc1, sh1, w_in[l])
        o_gla = _gla(z, w_alpha_f[l], b_alpha_f[l], w_alpha_b[l], b_alpha_b[l], g_gla_head[l])
        x = _mix_out(x, o_gla, z, w_pool[l], pool_scale[l], w_out[l], g_post_mix[l], gt1)
        y = _peer_ffn(x, g_pre_ffn[l], sc2, sh2, w_peer_q[l], peer_sub_keys[l], peer_u[l], peer_v[l])
        x = _post_residual(x, y, g_post_ffn[l], gt2)
    return x
```

```python
import jax
import jax.numpy as jnp
from jax import lax
from jax.experimental import pallas as pl
from jax.experimental.pallas import tpu as pltpu

D_MODEL = 1024
GLA_WIDTH = 512
POOL_WIDTH = 512
GLA_HEADS = 4
GLA_DV = 128
GLA_DK = 64
GLA_KEY_WIDTH = 256
GLA_GATE_RANK = 16
GLA_TAU = 16.0
GLA_CHUNK = 64
POOL_WINDOWS = (2, 4, 8, 16)
POOL_GROUP_WIDTH = 128
PEER_HEADS = 8
PEER_NKEYS = 128
PEER_HALF = 128
PEER_TOPK = 16
N_MOD = 6
EPS = 1e-6

LANES = 128
VMEM_LIMIT = 48 * 1024 * 1024
ROW_TILE = 512

Z_Q, Z_K, Z_V, Z_R, Z_P, Z_A = 0, 256, 512, 1024, 1536, 2048
Z_COLS = 2176
POOL_HALO = 8
GLA_UNROLL = 8


def _mod_body(c_ref, w_ref, b_ref, o_ref):
    c = c_ref[...]
    cond = c * (1.0 / (1.0 + jnp.exp(-c)))
    o_ref[...] = jnp.dot(cond.astype(jnp.bfloat16), w_ref[...].astype(jnp.bfloat16),
                         preferred_element_type=jnp.float32) + b_ref[...]


def _modulation(c, w_mod, b_mod):
    B, D = c.shape
    n = w_mod.shape[1]
    return pl.pallas_call(
        _mod_body,
        grid=(n // D,),
        in_specs=[
            pl.BlockSpec((B, D), lambda j: (0, 0)),
            pl.BlockSpec((D, D), lambda j: (0, j)),
            pl.BlockSpec((1, D), lambda j: (0, j)),
        ],
        out_specs=pl.BlockSpec((B, D), lambda j: (0, j)),
        out_shape=jax.ShapeDtypeStruct((B, n), jnp.float32),
        name="modulation",
    )(c, w_mod, b_mod.reshape(1, n))


def _in_proj_body(x_ref, g_ref, sc_ref, sh_ref, w_ref, z_ref):
    x = x_ref[0]
    h = x * lax.rsqrt(jnp.mean(x * x, axis=-1, keepdims=True) + EPS) * g_ref[...]
    h = h * (1.0 + sc_ref[0]) + sh_ref[0]
    z_ref[0] = jnp.dot(h.astype(jnp.bfloat16), w_ref[...], preferred_element_type=jnp.float32)


def _in_proj(x, g, sc, sh, w_in):
    B, S, D = x.shape
    w = jnp.concatenate([w_in[:, :Z_P], w_in[:, Z_P + 32:], w_in[:, Z_P:Z_P + 32],
                         jnp.zeros((D, Z_COLS - Z_A - 32), w_in.dtype)], axis=1).astype(jnp.bfloat16)
    mod_spec = pl.BlockSpec((1, 1, D), lambda b, s: (b, 0, 0))
    return pl.pallas_call(
        _in_proj_body,
        grid=(B, S // ROW_TILE),
        in_specs=[
            pl.BlockSpec((1, ROW_TILE, D), lambda b, s: (b, s, 0)),
            pl.BlockSpec((1, D), lambda b, s: (0, 0)),
            mod_spec,
            mod_spec,
            pl.BlockSpec((D, Z_COLS), lambda b, s: (0, 0)),
        ],
        out_specs=pl.BlockSpec((1, ROW_TILE, Z_COLS), lambda b, s: (b, s, 0)),
        out_shape=jax.ShapeDtypeStruct((B, S, Z_COLS), jnp.float32),
        compiler_params=pltpu.CompilerParams(vmem_limit_bytes=VMEM_LIMIT),
        name="in_proj",
    )(x, g.reshape(1, D), sc, sh, w)


def _log_sigmoid(x):
    return jnp.minimum(x, 0.0) - jnp.log1p(jnp.exp(-jnp.abs(x)))


def _bf16_dot(a, b, dims):
    return lax.dot_general(a.astype(jnp.bfloat16), b.astype(jnp.bfloat16), (dims, ((), ())),
                           preferred_element_type=jnp.float32)


def _gla_body(q_ref, k_ref, v_ref, r_ref, a_ref, wf_ref, bf_ref, wb_ref, bb_ref, gh_ref, o_ref,
              of_ref, st_ref):
    C = GLA_CHUNK
    n_chunks = q_ref.shape[1] // C
    row = lax.broadcasted_iota(jnp.int32, (C, C), 0)
    col = lax.broadcasted_iota(jnp.int32, (C, C), 1)
    lane = lax.broadcasted_iota(jnp.int32, (1, LANES), 1)
    head_lanes = [lane < GLA_DK, lane >= GLA_DK]

    def chunk_group(chunk_ids, w_ref, b_ref, cum_mask, keep_mask, total_row, emit):
        rows = [pl.ds(pl.multiple_of(n * C, C), C) for n in chunk_ids]
        log_a = [_log_sigmoid(_bf16_dot(a_ref[0, r, :], w_ref[0], ((1,), (0,))) + b_ref[0]) * (1.0 / GLA_TAU)
                 for r in rows]
        b = [jnp.dot(cum_mask, la, precision=lax.Precision.HIGHEST, preferred_element_type=jnp.float32)
             for la in log_a]
        qm, k_inv, k_end, gamma = [], [], [], []
        for r, bc in zip(rows, b):
            b_tot = bc[total_row:total_row + 1, :]
            q_dec = q_ref[0, r, :] * (GLA_DK ** -0.5) * jnp.exp(bc)
            k = k_ref[0, r, :]
            qm.append([jnp.where(head_lanes[hd], q_dec, 0.0) for hd in range(2)])
            k_inv.append(k * jnp.exp(-bc))
            k_end.append(k * jnp.exp(b_tot - bc))
            gamma.append(jnp.exp(b_tot))
        v = [[v_ref[0, r, hd * GLA_DV:(hd + 1) * GLA_DV] for hd in range(2)] for r in rows]
        scores = [[jnp.where(keep_mask, _bf16_dot(qm[u][hd], k_inv[u], ((1,), (1,))), 0.0) for hd in range(2)]
                  for u in range(len(rows))]
        o_intra = [[_bf16_dot(scores[u][hd], v[u][hd], ((1,), (0,))) for hd in range(2)] for u in range(len(rows))]
        kv = [[_bf16_dot(v[u][hd], k_end[u], ((0,), (0,))) for hd in range(2)] for u in range(len(rows))]
        for u, r in enumerate(rows):
            for hd in range(2):
                state = st_ref[hd]
                emit(r, hd, o_intra[u][hd] + _bf16_dot(qm[u][hd], state, ((1,), (1,))))
                st_ref[hd] = state * gamma[u] + kv[u][hd]

    st_ref[...] = jnp.zeros_like(st_ref)
    prefix = (row >= col).astype(jnp.float32)

    def emit_fwd(rows, hd, o):
        of_ref[rows, hd * GLA_DV:(hd + 1) * GLA_DV] = o

    def fwd(i, carry):
        chunk_group([i * GLA_UNROLL + u for u in range(GLA_UNROLL)], wf_ref, bf_ref, prefix, row >= col, C - 1,
                    emit_fwd)
        return carry
    lax.fori_loop(0, n_chunks // GLA_UNROLL, fwd, 0)

    st_ref[...] = jnp.zeros_like(st_ref)
    suffix = (row <= col).astype(jnp.float32)

    def emit_bwd(rows, hd, o):
        cols = slice(hd * GLA_DV, (hd + 1) * GLA_DV)
        tot = of_ref[rows, cols] + o
        y = tot * lax.rsqrt(jnp.mean(tot * tot, axis=-1, keepdims=True) + EPS) * gh_ref[0, :, cols]
        r = r_ref[0, rows, cols]
        o_ref[0, rows, cols] = y * (r * (1.0 / (1.0 + jnp.exp(-r))))

    def bwd(i, carry):
        chunk_group([n_chunks - 1 - (i * GLA_UNROLL + u) for u in range(GLA_UNROLL)], wb_ref, bb_ref, suffix,
                    row < col, 0, emit_bwd)
        return carry
    lax.fori_loop(0, n_chunks // GLA_UNROLL, bwd, 0)


def _gla(z, w_af, b_af, w_ab, b_ab, g_head):
    B, S, _ = z.shape
    pairs = GLA_HEADS // 2

    def pad_w(w, first_row):
        wp = jnp.zeros((pairs, LANES, LANES), jnp.float32)
        return wp.at[:, first_row:first_row + GLA_GATE_RANK, :].set(
            jnp.transpose(w.reshape(GLA_GATE_RANK, pairs, LANES), (1, 0, 2)))

    def lane_blk(off):
        return pl.BlockSpec((1, S, LANES), lambda b, j: (b, 0, off // LANES + j))

    def wide_blk(off):
        return pl.BlockSpec((1, S, 2 * LANES), lambda b, j: (b, 0, off // (2 * LANES) + j))

    w_spec = pl.BlockSpec((1, LANES, LANES), lambda b, j: (j, 0, 0))
    b_spec = pl.BlockSpec((1, 1, LANES), lambda b, j: (j, 0, 0))
    return pl.pallas_call(
        _gla_body,
        grid=(B, pairs),
        in_specs=[
            lane_blk(Z_Q), lane_blk(Z_K), wide_blk(Z_V), wide_blk(Z_R),
            pl.BlockSpec((1, S, LANES), lambda b, j: (b, 0, Z_A // LANES)),
            w_spec, b_spec, w_spec, b_spec,
            pl.BlockSpec((1, 1, 2 * LANES), lambda b, j: (j, 0, 0)),
        ],
        out_specs=pl.BlockSpec((1, S, 2 * LANES), lambda b, j: (b, 0, j)),
        out_shape=jax.ShapeDtypeStruct((B, S, GLA_WIDTH), jnp.float32),
        scratch_shapes=[
            pltpu.VMEM((S, 2 * LANES), jnp.float32),
            pltpu.VMEM((2, GLA_DV, LANES), jnp.float32),
        ],
        compiler_params=pltpu.CompilerParams(vmem_limit_bytes=VMEM_LIMIT),
        name="gla",
    )(z, z, z, z, z,
      pad_w(w_af, 0), b_af.reshape(pairs, 1, LANES),
      pad_w(w_ab, GLA_GATE_RANK), b_ab.reshape(pairs, 1, LANES),
      g_head.reshape(pairs, 1, 2 * LANES))


def _mix_out_body(x_ref, o_ref, p_ref, pp_ref, pn_ref, wp_ref, ps_ref, wo_ref, g_ref, gate_ref, out_ref, pe_ref):
    s = pl.program_id(1)
    n_s = pl.num_programs(1)
    seq_len = n_s * ROW_TILE
    pe_ref[pl.ds(0, POOL_HALO), :] = jnp.where(s > 0, pp_ref[0], 0.0)
    pe_ref[pl.ds(POOL_HALO, ROW_TILE), :] = p_ref[0]
    pe_ref[pl.ds(POOL_HALO + ROW_TILE, POOL_HALO), :] = jnp.where(s < n_s - 1, pn_ref[0], 0.0)
    pos = s * ROW_TILE + lax.broadcasted_iota(jnp.int32, (ROW_TILE, POOL_GROUP_WIDTH), 0)
    mix = jnp.dot(o_ref[0].astype(jnp.bfloat16), wo_ref[pl.ds(0, GLA_WIDTH), :],
                  preferred_element_type=jnp.float32)
    for gi, w in enumerate(POOL_WINDOWS):
        cols = pl.ds(gi * POOL_GROUP_WIDTH, POOL_GROUP_WIDTH)
        total = pe_ref[pl.ds(POOL_HALO - w // 2, ROW_TILE), cols]
        for d in range(-w // 2 + 1, w // 2):
            total = total + pe_ref[pl.ds(POOL_HALO + d, ROW_TILE), cols]
        count = (jnp.minimum(pos + w // 2, seq_len) - jnp.maximum(pos - w // 2, 0)).astype(jnp.float32)
        pooled = total / count - pe_ref[pl.ds(POOL_HALO, ROW_TILE), cols]
        y = jnp.dot(pooled.astype(jnp.bfloat16), wp_ref[gi], preferred_element_type=jnp.float32)
        y = y * ps_ref[:, gi * POOL_GROUP_WIDTH:(gi + 1) * POOL_GROUP_WIDTH]
        mix = mix + jnp.dot(y.astype(jnp.bfloat16),
                            wo_ref[pl.ds(GLA_WIDTH + gi * POOL_GROUP_WIDTH, POOL_GROUP_WIDTH), :],
                            preferred_element_type=jnp.float32)
    yn = mix * lax.rsqrt(jnp.mean(mix * mix, axis=-1, keepdims=True) + EPS) * g_ref[...]
    out_ref[0] = x_ref[0] + gate_ref[0] * yn


def _mix_out(x, o_gla, z, w_pool, pool_scale, w_out, g, gate):
    B, S, D = x.shape
    halo_per_tile = ROW_TILE // POOL_HALO
    last_halo = S // POOL_HALO - 1
    row_spec = pl.BlockSpec((1, ROW_TILE, D), lambda b, s: (b, s, 0))
    return pl.pallas_call(
        _mix_out_body,
        grid=(B, S // ROW_TILE),
        in_specs=[
            row_spec,
            pl.BlockSpec((1, ROW_TILE, GLA_WIDTH), lambda b, s: (b, s, 0)),
            pl.BlockSpec((1, ROW_TILE, POOL_WIDTH), lambda b, s: (b, s, Z_P // POOL_WIDTH)),
            pl.BlockSpec((1, POOL_HALO, POOL_WIDTH),
                         lambda b, s: (b, jnp.maximum(s * halo_per_tile - 1, 0), Z_P // POOL_WIDTH)),
            pl.BlockSpec((1, POOL_HALO, POOL_WIDTH),
                         lambda b, s: (b, jnp.minimum((s + 1) * halo_per_tile, last_halo), Z_P // POOL_WIDTH)),
            pl.BlockSpec(w_pool.shape, lambda b, s: (0, 0, 0)),
            pl.BlockSpec((1, POOL_WIDTH), lambda b, s: (0, 0)),
            pl.BlockSpec((D, D), lambda b, s: (0, 0)),
            pl.BlockSpec((1, D), lambda b, s: (0, 0)),
            pl.BlockSpec((1, 1, D), lambda b, s: (b, 0, 0)),
        ],
        out_specs=row_spec,
        out_shape=jax.ShapeDtypeStruct((B, S, D), jnp.float32),
        scratch_shapes=[pltpu.VMEM((ROW_TILE + 2 * POOL_HALO, POOL_WIDTH), jnp.float32)],
        compiler_params=pltpu.CompilerParams(vmem_limit_bytes=VMEM_LIMIT),
        name="mix_out",
    )(x, o_gla, z, z, z, w_pool.astype(jnp.bfloat16), pool_scale.reshape(1, POOL_WIDTH),
      w_out.astype(jnp.bfloat16), g.reshape(1, D), gate)


def _peer_ffn(x, g_pre, sc2, sh2, w_q, sub_keys, u_tab, v_tab):
    B, S, D = x.shape
    h2, offsets_t, gates_t = _peer_retrieve(x, g_pre, sc2, sh2, w_q, sub_keys)
    row_offsets = offsets_t.T
    coef_dup = _peer_u_pass(row_offsets, h2.reshape(B * S, D // LANES, LANES), gates_t, _pack_table(u_tab))
    return _peer_v_pass(row_offsets, coef_dup, _pack_table(v_tab))


NEG_INF = float("-inf")
RETR_TOKENS = 512


def _top16_rows(pieces, piece_ids):
    vals, ids = [], []
    for _ in range(PEER_TOPK):
        v, d = list(pieces), list(piece_ids)
        while len(v) > 1:
            nv, nd = [], []
            for a in range(0, len(v) - 1, 2):
                take_right = v[a + 1] > v[a]
                nv.append(jnp.where(take_right, v[a + 1], v[a]))
                nd.append(jnp.where(take_right, d[a + 1], d[a]))
            if len(v) % 2:
                nv.append(v[-1])
                nd.append(d[-1])
            v, d = nv, nd
        m = jnp.max(v[0], axis=0, keepdims=True)
        r = jnp.min(jnp.where(v[0] == m, d[0], jnp.iinfo(jnp.int32).max), axis=0, keepdims=True)
        vals.append(m)
        ids.append(r)
        pieces = [jnp.where(dj == r, NEG_INF, sj) for sj, dj in zip(pieces, piece_ids)]
    return vals, ids


def _peer_retrieve_body(x_ref, g_ref, sc_ref, sh_ref, wq_ref, keys_ref, h_ref, idx_ref, gate_ref, q_ref):
    x = x_ref[0]
    h = x * lax.rsqrt(jnp.mean(x * x, axis=-1, keepdims=True) + EPS) * g_ref[...]
    h = h * (1.0 + sc_ref[0]) + sh_ref[0]
    for j in range(h.shape[1] // LANES):
        h_ref[pl.ds(j, ROW_TILE, stride=8), :] = h[:, j * LANES:(j + 1) * LANES]
    q_ref[...] = jnp.dot(h.astype(jnp.bfloat16), wq_ref[...],
                         preferred_element_type=jnp.float32).astype(jnp.bfloat16)
    sub8 = lax.broadcasted_iota(jnp.int32, (8, RETR_TOKENS), 0)
    key_ids = [sub8 + 8 * j for j in range(PEER_NKEYS // 8)]
    cand_ids = ([sub8, sub8 + 8] + [sub8 + a * PEER_TOPK for a in range(1, 8)]
                + [(sub8 + 8) * PEER_TOPK])
    chunks = ROW_TILE // RETR_TOKENS

    def head_chunk(i, carry):
        hd = lax.shift_right_logical(i, chunks.bit_length() - 1)
        ch = i & (chunks - 1)
        tok = pl.ds(pl.multiple_of(ch * RETR_TOKENS, RETR_TOKENS), RETR_TOKENS)
        vals, rows = [], []
        for p in range(2):
            col = pl.multiple_of((hd * 2 + p) * PEER_HALF, PEER_HALF)
            qs = q_ref[tok, pl.ds(col, PEER_HALF)]
            s = lax.dot_general(keys_ref[hd, p], qs, (((1,), (1,)), ((), ())),
                                preferred_element_type=jnp.float32)
            v, r = _top16_rows([s[8 * j:8 * j + 8] for j in range(PEER_NKEYS // 8)], key_ids)
            vals.append(v)
            rows.append(r)
        v1_hi = jnp.concatenate(vals[0][8:], axis=0)
        v2_lo = jnp.concatenate(vals[1][:8], axis=0)
        v2_hi = jnp.concatenate(vals[1][8:], axis=0)
        cand = ([vals[0][0] + v2_lo, vals[0][0] + v2_hi] + [vals[0][a] + v2_lo for a in range(1, 8)]
                + [v1_hi + vals[1][0]])
        top_s, pos = _top16_rows(cand, cand_ids)
        pos = jnp.concatenate(pos, axis=0)
        pa = pos >> 4
        pb = pos & (PEER_TOPK - 1)
        i1 = jnp.zeros_like(pos)
        i2 = jnp.zeros_like(pos)
        for a in range(PEER_TOPK):
            i1 = jnp.where(pa == a, rows[0][a], i1)
            i2 = jnp.where(pb == a, rows[1][a], i2)
        ts = jnp.concatenate(top_s, axis=0)
        e = jnp.exp(ts - top_s[0])
        gates = e / jnp.sum(e, axis=0, keepdims=True)
        out_rows = pl.ds(pl.multiple_of(hd * PEER_TOPK, PEER_TOPK), PEER_TOPK)
        idx_ref[out_rows, tok] = (i1 * PEER_NKEYS + i2) * PEER_CHUNKS
        gate_ref[out_rows, tok] = gates
        return carry

    lax.fori_loop(0, PEER_HEADS * chunks, head_chunk, 0)


def _peer_retrieve(x, g_pre, sc2, sh2, w_q, sub_keys):
    B, S, D = x.shape
    T = B * S
    nq = w_q.shape[1]
    spb = S // ROW_TILE
    tok_spec = pl.BlockSpec((1, ROW_TILE, D), lambda b, s: (b, s, 0))
    mod_spec = pl.BlockSpec((1, 1, D), lambda b, s: (b, 0, 0))
    slot_spec = pl.BlockSpec((PEER_SLOTS, ROW_TILE), lambda b, s: (0, b * spb + s))
    return pl.pallas_call(
        _peer_retrieve_body,
        grid=(B, spb),
        in_specs=[
            tok_spec,
            pl.BlockSpec((1, D), lambda b, s: (0, 0)),
            mod_spec,
            mod_spec,
            pl.BlockSpec((D, nq), lambda b, s: (0, 0)),
            pl.BlockSpec(sub_keys.shape, lambda b, s: (0, 0, 0, 0)),
        ],
        out_specs=[pl.BlockSpec((ROW_TILE * D // LANES, LANES), lambda b, s: (b * spb + s, 0)), slot_spec, slot_spec],
        out_shape=[
            jax.ShapeDtypeStruct((T * D // LANES, LANES), jnp.float32),
            jax.ShapeDtypeStruct((PEER_SLOTS, T), jnp.int32),
            jax.ShapeDtypeStruct((PEER_SLOTS, T), jnp.float32),
        ],
        scratch_shapes=[pltpu.VMEM((ROW_TILE, nq), jnp.bfloat16)],
        compiler_params=pltpu.CompilerParams(vmem_limit_bytes=VMEM_LIMIT),
        name="peer_retrieve",
    )(x, g_pre.reshape(1, D), sc2, sh2, w_q.astype(jnp.bfloat16), sub_keys.astype(jnp.bfloat16))


PEER_SLOTS = PEER_HEADS * PEER_TOPK
PEER_CHUNKS = D_MODEL // 256
PEER_TB = 256


PACK_ROWS = 1024


def _bf16_bits(x):
    return pltpu.bitcast(x.astype(jnp.bfloat16).astype(jnp.float32), jnp.uint32)


def _pack_table_body(x_ref, o_ref):
    half = x_ref.shape[1] // 2
    for c in range(PEER_CHUNKS):
        lo = _bf16_bits(x_ref[:, c * LANES:(c + 1) * LANES])
        hi = _bf16_bits(x_ref[:, half + c * LANES:half + (c + 1) * LANES])
        o_ref[pl.ds(c, PACK_ROWS, stride=PEER_CHUNKS), :] = lax.shift_right_logical(lo, jnp.uint32(16)) | hi


def _pack_table(tab):
    E, D = tab.shape
    return pl.pallas_call(
        _pack_table_body,
        grid=(E // PACK_ROWS,),
        in_specs=[pl.BlockSpec((PACK_ROWS, D), lambda i: (i, 0))],
        out_specs=pl.BlockSpec((PACK_ROWS * PEER_CHUNKS, LANES), lambda i: (i, 0)),
        out_shape=jax.ShapeDtypeStruct((E * PEER_CHUNKS, LANES), jnp.uint32),
        compiler_params=pltpu.CompilerParams(vmem_limit_bytes=VMEM_LIMIT),
        name="pack_table",
    )(tab)


IDX_SUB = 16
N_SUB = PEER_TB // IDX_SUB


def _gather_token(stage_ref, half, i, tab_ref, tile_ref):
    for k in range(PEER_SLOTS):
        row = pl.multiple_of(stage_ref[half, i, k], PEER_CHUNKS)
        tile_ref[pl.ds(k * PEER_CHUNKS, PEER_CHUNKS), :] = tab_ref[pl.ds(row, PEER_CHUNKS), :]


def _tile_chunk_f32(tile_ref, c):
    w = tile_ref[pl.ds(c, PEER_SLOTS, stride=PEER_CHUNKS), :]
    return pltpu.bitcast(w, jnp.bfloat16).astype(jnp.float32)


def _token_stream(idx_ref, tab_ref, stage_ref, sems, tiles, consume):
    def stage_copy(sub, half):
        first = pl.multiple_of(jnp.minimum(sub, N_SUB - 1) * IDX_SUB, IDX_SUB)
        return pltpu.make_async_copy(idx_ref.at[pl.ds(first, IDX_SUB), :], stage_ref.at[half], sems.at[half])

    stage_copy(0, 0).start()
    stage_copy(1, 1).start()
    stage_copy(0, 0).wait()
    _gather_token(stage_ref, 0, 0, tab_ref, tiles[0])

    def two_subs(j, carry):
        for half in range(2):
            sub = 2 * j + half
            for i in range(IDX_SUB):
                if i + 1 < IDX_SUB:
                    _gather_token(stage_ref, half, i + 1, tab_ref, tiles[(i + 1) % 2])
                else:
                    stage_copy(sub + 1, 1 - half).wait()
                    stage_copy(sub + 2, half).start()
                    _gather_token(stage_ref, 1 - half, 0, tab_ref, tiles[(i + 1) % 2])
                consume(tiles[i % 2], sub * IDX_SUB + i, half == 1 and i == IDX_SUB - 1)
        return carry
    lax.fori_loop(0, N_SUB // 2, two_subs, 0)
    stage_copy(N_SUB + 1, 1).wait()


def _token_block_cols(t):
    return pl.ds(pl.multiple_of(lax.shift_right_logical(t, 7) * LANES, LANES), LANES)


def _stage_scratch():
    return [
        pltpu.SMEM((2, IDX_SUB, PEER_SLOTS), jnp.int32),
        pltpu.SemaphoreType.DMA((2,)),
        pltpu.VMEM((PEER_SLOTS * PEER_CHUNKS, LANES), jnp.uint32),
        pltpu.VMEM((PEER_SLOTS * PEER_CHUNKS, LANES), jnp.uint32),
    ]


def _peer_u_body(idx_ref, h_ref, gate_ref, tab_ref, coef_ref, stage_ref, sems, tile_a, tile_b, sblk_ref):
    lane = lax.broadcasted_iota(jnp.int32, (2 * PEER_SLOTS, LANES), 1)
    odd = (lax.broadcasted_iota(jnp.int32, (8, LANES), 0) & 1) == 1

    def consume(tile_ref, t, may_end_block):
        acc = jnp.zeros((2 * PEER_SLOTS // 8, 8, LANES), jnp.float32)
        for c in range(PEER_CHUNKS):
            h_lo = jnp.broadcast_to(h_ref[t, pl.ds(c, 1), :], (8, LANES))
            h_hi = jnp.broadcast_to(h_ref[t, pl.ds(PEER_CHUNKS + c, 1), :], (8, LANES))
            pattern = jnp.where(odd, h_hi, h_lo)
            x = _tile_chunk_f32(tile_ref, c).reshape(2 * PEER_SLOTS // 8, 8, LANES)
            acc = acc + x * pattern[None]
        s = jnp.sum(acc.reshape(2 * PEER_SLOTS, LANES), axis=1, keepdims=True)
        tl = t & (LANES - 1)
        sblk_ref[...] = jnp.where(lane == tl, s, sblk_ref[...])
        if not may_end_block:
            return

        @pl.when(tl == LANES - 1)
        def _():
            even_rows = sblk_ref[pl.ds(0, PEER_SLOTS, stride=2), :]
            odd_rows = sblk_ref[pl.ds(1, PEER_SLOTS, stride=2), :]
            a = even_rows + odd_rows
            cols = _token_block_cols(t)
            coef = gate_ref[:, cols] * (0.5 * a * (1.0 + lax.erf(a * (2.0 ** -0.5))))
            sblk_ref[pl.ds(0, PEER_SLOTS, stride=2), :] = coef
            sblk_ref[pl.ds(1, PEER_SLOTS, stride=2), :] = coef
            coef_ref[:, cols] = sblk_ref[...]

    _token_stream(idx_ref, tab_ref, stage_ref, sems, (tile_a, tile_b), consume)


def _peer_u_pass(row_offsets, ht, gates_t, tab):
    T = ht.shape[0]
    return pl.pallas_call(
        _peer_u_body,
        grid=(T // PEER_TB,),
        in_specs=[
            pl.BlockSpec((PEER_TB, PEER_SLOTS), lambda i: (i, 0)),
            pl.BlockSpec((PEER_TB, 8, LANES), lambda i: (i, 0, 0)),
            pl.BlockSpec((PEER_SLOTS, PEER_TB), lambda i: (0, i)),
            pl.BlockSpec(memory_space=pltpu.VMEM),
        ],
        out_specs=pl.BlockSpec((2 * PEER_SLOTS, PEER_TB), lambda i: (0, i)),
        out_shape=jax.ShapeDtypeStruct((2 * PEER_SLOTS, T), jnp.float32),
        scratch_shapes=_stage_scratch() + [pltpu.VMEM((2 * PEER_SLOTS, LANES), jnp.float32)],
        compiler_params=pltpu.CompilerParams(vmem_limit_bytes=VMEM_LIMIT),
        name="peer_u_pass",
    )(row_offsets, ht, gates_t, tab)


def _peer_v_body(idx_ref, coef_ref, tab_ref, y_ref, stage_ref, sems, tile_a, tile_b):
    lane = lax.broadcasted_iota(jnp.int32, (2 * PEER_SLOTS, LANES), 1)

    def consume(tile_ref, t, may_end_block):
        del may_end_block
        tl = t & (LANES - 1)
        coef_blk = coef_ref[:, _token_block_cols(t)]
        ccol = jnp.sum(jnp.where(lane == tl, coef_blk, 0.0), axis=1, keepdims=True)
        ccol = ccol.reshape(2 * PEER_SLOTS // 8, 8, 1)
        for c in range(PEER_CHUNKS):
            x = _tile_chunk_f32(tile_ref, c).reshape(2 * PEER_SLOTS // 8, 8, LANES)
            part = jnp.sum(x * ccol, axis=0)
            part = part + pltpu.roll(part, 4, axis=0)
            part = part + pltpu.roll(part, 2, axis=0)
            y_ref[t, pl.ds(c, 1), :] = part[0:1]
            y_ref[t, pl.ds(PEER_CHUNKS + c, 1), :] = part[1:2]

    _token_stream(idx_ref, tab_ref, stage_ref, sems, (tile_a, tile_b), consume)


def _peer_v_pass(row_offsets, coef_dup, tab):
    T = row_offsets.shape[0]
    return pl.pallas_call(
        _peer_v_body,
        grid=(T // PEER_TB,),
        in_specs=[
            pl.BlockSpec((PEER_TB, PEER_SLOTS), lambda i: (i, 0)),
            pl.BlockSpec((2 * PEER_SLOTS, PEER_TB), lambda i: (0, i)),
            pl.BlockSpec(memory_space=pltpu.VMEM),
        ],
        out_specs=pl.BlockSpec((PEER_TB, 8, LANES), lambda i: (i, 0, 0)),
        out_shape=jax.ShapeDtypeStruct((T, 8, LANES), jnp.float32),
        scratch_shapes=_stage_scratch(),
        compiler_params=pltpu.CompilerParams(vmem_limit_bytes=VMEM_LIMIT),
        name="peer_v_pass",
    )(row_offsets, coef_dup, tab)


def _post_residual_body(x_ref, y_ref, g_ref, gate_ref, o_ref):
    pieces = [y_ref[pl.ds(j, ROW_TILE, stride=8), :] for j in range(x_ref.shape[2] // LANES)]
    sq = pieces[0] * pieces[0]
    for p in pieces[1:]:
        sq = sq + p * p
    inv = lax.rsqrt(jnp.sum(sq, axis=-1, keepdims=True) * (1.0 / x_ref.shape[2]) + EPS)
    for j, p in enumerate(pieces):
        cols = slice(j * LANES, (j + 1) * LANES)
        o_ref[0, :, cols] = x_ref[0, :, cols] + gate_ref[0, :, cols] * (p * inv * g_ref[:, cols])


def _post_residual(x, y, g, gate):
    B, S, D = x.shape
    pieces = D // LANES
    spb = S // ROW_TILE
    row_spec = pl.BlockSpec((1, ROW_TILE, D), lambda b, s: (b, s, 0))
    return pl.pallas_call(
        _post_residual_body,
        grid=(B, spb),
        in_specs=[
            row_spec,
            pl.BlockSpec((ROW_TILE * pieces, LANES), lambda b, s: (b * spb + s, 0)),
            pl.BlockSpec((1, D), lambda b, s: (0, 0)),
            pl.BlockSpec((1, 1, D), lambda b, s: (b, 0, 0)),
        ],
        out_specs=row_spec,
        out_shape=jax.ShapeDtypeStruct((B, S, D), x.dtype),
        name="post_residual",
    )(x, y.reshape(B * S * pieces, LANES), g.reshape(1, D), gate)


def kernel(x, c, w_mod, b_mod, g_pre_mix, g_post_mix, w_in, w_alpha_f, b_alpha_f, w_alpha_b, b_alpha_b,
           g_gla_head, w_pool, pool_scale, w_out, g_pre_ffn, g_post_ffn, w_peer_q, peer_sub_keys,
           peer_u, peer_v):
    for l in range(w_mod.shape[0]):
        mod = _modulation(c, w_mod[l], b_mod[l])
        sh1, sc1, gt1, sh2, sc2, gt2 = [m[:, None, :] for m in jnp.split(mod, N_MOD, axis=-1)]
        z = _in_proj(x, g_pre_mix[l], sc1, sh1, w_in[l])
        o_gla = _gla(z, w_alpha_f[l], b_alpha_f[l], w_alpha_b[l], b_alpha_b[l], g_gla_head[l])
        x = _mix_out(x, o_gla, z, w_pool[l], pool_scale[l], w_out[l], g_post_mix[l], gt1)
        y = _peer_ffn(x, g_pre_ffn[l], sc2, sh2, w_peer_q[l], peer_sub_keys[l], peer_u[l], peer_v[l])
        x = _post_residual(x, y, g_post_ffn[l], gt2)
    return x
```
